```python
import jax, jax.numpy as jnp
from jax import lax
import numpy as np

D_MODEL = 1024
BATCH = 2
SEQ = 8192
DEPTH = 2
DEC_BATCH = 128
DEC_SEQ = 4
PAST_LEN = 2048
PAGE_SIZE = 128

HEAD_DIM = 64
ATT_WIDTH = D_MODEL // 2
N_ATT_HEADS = ATT_WIDTH // HEAD_DIM
CONV_CH = D_MODEL - ATT_WIDTH
CONV_WIDTH = 31
DILATION_PATTERNS = ((128, 1), (512, 4), (2048, 16))
MAX_WINDOW = max(w for w, _ in DILATION_PATTERNS)
ROT_DIM = HEAD_DIM // 4
ROPE_THETA = 500000.0
ATTN_SCALE = HEAD_DIM ** -0.5
PROJ_WIDTH = 3 * ATT_WIDTH + 2 * CONV_CH
D_FF = 7 * D_MODEL // 2
N_EXPERTS = 8
TOP_K = 2
D_FF_EXPERT = 7 * D_MODEL // 2
N_DENSE = (DEPTH + 1) // 2
N_MOE = DEPTH // 2
Q_BLOCK = 128
EPS = 1e-6

kernel_name = "hymba_dilated_conformer_decoder_step"


def rmsnorm(x, g):
    x32 = x.astype(jnp.float32)
    y = x32 * lax.rsqrt(jnp.mean(x32 * x32, axis=-1, keepdims=True) + EPS) * g.astype(jnp.float32)
    return y.astype(x.dtype)


def layernorm(x, g, b):
    x32 = x.astype(jnp.float32)
    mu = jnp.mean(x32, axis=-1, keepdims=True)
    xc = x32 - mu
    var = jnp.mean(xc * xc, axis=-1, keepdims=True)
    y = xc * lax.rsqrt(var + EPS) * g.astype(jnp.float32) + b.astype(jnp.float32)
    return y.astype(x.dtype)


def rope_partial(x, pos):
    half = ROT_DIM // 2
    inv_freq = ROPE_THETA ** (-jnp.arange(half, dtype=jnp.float32) * 2.0 / ROT_DIM)
    ang = pos.astype(jnp.float32)[:, None] * inv_freq[None, :]
    cos = jnp.cos(ang)[None, :, None, :]
    sin = jnp.sin(ang)[None, :, None, :]
    xr = x[..., :ROT_DIM].astype(jnp.float32)
    x1, x2 = xr[..., :half], xr[..., half:]
    rot = jnp.concatenate([x1 * cos - x2 * sin, x2 * cos + x1 * sin], axis=-1).astype(x.dtype)
    return jnp.concatenate([rot, x[..., ROT_DIM:]], axis=-1)


def dilated_attention(q, keys, vals, qrow):
    B, T, H, Dh = q.shape
    blk = Q_BLOCK if T % Q_BLOCK == 0 else T
    nb = T // blk
    qb = q.reshape(B, nb, blk, H, Dh).transpose(1, 0, 2, 3, 4)
    rb = qrow.reshape(nb, blk)

    def one_block(args):
        q_blk, r_blk = args
        outs, lses = [], []
        for window, dil in DILATION_PATTERNS:
            offs = dil * jnp.arange(window // dil + 1, dtype=jnp.int32)
            idx = r_blk[:, None] - offs[None, :]
            valid = idx >= 0
            idx = jnp.maximum(idx, 0)
            kg = keys[:, idx]
            vg = vals[:, idx]
            s = jnp.einsum('bqhd,bqkhd->bqkh', q_blk, kg).astype(jnp.float32) * ATTN_SCALE
            s = jnp.where(valid[None, :, :, None], s, -jnp.inf)
            lse = jax.nn.logsumexp(s, axis=2)
            prob = jnp.exp(s - lse[:, :, None, :])
            outs.append(jnp.einsum('bqkh,bqkhd->bqhd', prob, vg.astype(jnp.float32)))
            lses.append(lse)
        wts = jax.nn.softmax(jnp.stack(lses), axis=0)
        return jnp.sum(wts[..., None] * jnp.stack(outs), axis=0).astype(q.dtype)

    out = lax.map(one_block, (qb, rb))
    return out.transpose(1, 0, 2, 3, 4).reshape(B, T, H, Dh)


def causal_dwconv(u_full, w, b):
    y = lax.conv_general_dilated(u_full, w[:, None, :], window_strides=(1,), padding='VALID',
                                 dimension_numbers=('NWC', 'WIO', 'NWC'),
                                 feature_group_count=u_full.shape[-1])
    return y + b


def swiglu(x, w_gate, w_up, w_down):
    return (jax.nn.silu(x @ w_gate) * (x @ w_up)) @ w_down


def moe_swiglu(h, w_router, w_gate, w_up, w_down):
    B, T, D = h.shape
    xf = h.reshape(B * T, D)
    logits = (xf @ w_router).astype(jnp.float32)
    top_v, top_i = lax.top_k(logits, TOP_K)
    top_w = jax.nn.softmax(top_v, axis=-1)
    gates = jnp.sum(jax.nn.one_hot(top_i, N_EXPERTS, dtype=jnp.float32) * top_w[..., None], axis=1)
    out = jnp.zeros((B * T, D), jnp.float32)
    for e in range(N_EXPERTS):
        out = out + gates[:, e:e + 1] * swiglu(xf, w_gate[e], w_up[e], w_down[e]).astype(jnp.float32)
    return out.astype(h.dtype).reshape(B, T, D)


def layer_forward(x, c, pos0, k_hist, v_hist, conv_hist, l, p):
    B, T, _ = x.shape
    mod = (jax.nn.silu(c) @ p['w_mod'][l] + p['b_mod'][l])[:, None, :]
    sh1, sc1, gt1, sh2, sc2, gt2 = jnp.split(mod, 6, axis=-1)

    h = rmsnorm(x, p['g_pre_mix'][l]) * (1 + sc1) + sh1
    proj = h @ p['w_in'][l]
    q, k, v, a, g = jnp.split(proj, [ATT_WIDTH, 2 * ATT_WIDTH, 3 * ATT_WIDTH,
                                     3 * ATT_WIDTH + CONV_CH], axis=-1)
    pos = pos0 + jnp.arange(T, dtype=jnp.int32)
    q = rope_partial(q.reshape(B, T, N_ATT_HEADS, HEAD_DIM), pos)
    k = rope_partial(k.reshape(B, T, N_ATT_HEADS, HEAD_DIM), pos)
    v = v.reshape(B, T, N_ATT_HEADS, HEAD_DIM)
    keys = jnp.concatenate([k_hist, k], axis=1)
    vals = jnp.concatenate([v_hist, v], axis=1)
    qrow = k_hist.shape[1] + jnp.arange(T, dtype=jnp.int32)
    att = dilated_attention(q, keys, vals, qrow).reshape(B, T, ATT_WIDTH)

    u = a * jax.nn.sigmoid(g)
    u_full = jnp.concatenate([conv_hist, u], axis=1)
    cv = causal_dwconv(u_full, p['conv_w'][l], p['conv_b'][l])
    cv = jax.nn.silu(layernorm(cv, p['conv_ln_g'][l], p['conv_ln_b'][l]))

    mix = jnp.concatenate([att, cv], axis=-1) @ p['w_out'][l]
    x = x + gt1 * rmsnorm(mix, p['g_post_mix'][l])

    h2 = rmsnorm(x, p['g_pre_ffn'][l]) * (1 + sc2) + sh2
    i = l // 2
    if l % 2 == 0:
        f = swiglu(h2, p['ffn_w_gate'][i], p['ffn_w_up'][i], p['ffn_w_down'][i])
    else:
        f = moe_swiglu(h2, p['moe_w_router'][i], p['moe_w_gate'][i], p['moe_w_up'][i], p['moe_w_down'][i])
    x = x + gt2 * rmsnorm(f, p['g_post_ffn'][l])

    keep = min(MAX_WINDOW, T)
    return x, k[:, T - keep:], v[:, T - keep:], u_full[:, -(CONV_WIDTH - 1):]


def setup_inputs(seed: int = 0) -> dict:
    key = jax.random.key(seed)
    ks = jax.random.split(key, 32)
    f32 = jnp.float32
    W_BUF = min(MAX_WINDOW, PAST_LEN)

    def nrm(k, shape, s):
        return jax.random.normal(k, shape, f32) * s

    return {
        "x_prompt": nrm(ks[0], (BATCH, SEQ, D_MODEL), 1.0),
        "x_sample": nrm(ks[1], (DEC_BATCH, DEC_SEQ, D_MODEL), 1.0),
        "cache_k": nrm(ks[2], (DEPTH, DEC_BATCH, W_BUF, N_ATT_HEADS, HEAD_DIM), 1.0),
        "cache_v": nrm(ks[3], (DEPTH, DEC_BATCH, W_BUF, N_ATT_HEADS, HEAD_DIM), 1.0),
        "state_conv": nrm(ks[4], (DEPTH, DEC_BATCH, CONV_WIDTH - 1, CONV_CH), 0.5),
        "c_prompt": nrm(ks[5], (BATCH, D_MODEL), 1.0),
        "c_sample": nrm(ks[6], (DEC_BATCH, D_MODEL), 1.0),
        "w_mod": nrm(ks[7], (DEPTH, D_MODEL, 6 * D_MODEL), 0.3 * D_MODEL ** -0.5),
        "b_mod": nrm(ks[8], (DEPTH, 6 * D_MODEL), 0.02),
        "g_pre_mix": 1.0 + nrm(ks[9], (DEPTH, D_MODEL), 0.05),
        "g_post_mix": 1.0 + nrm(ks[10], (DEPTH, D_MODEL), 0.05),
        "g_pre_ffn": 1.0 + nrm(ks[11], (DEPTH, D_MODEL), 0.05),
        "g_post_ffn": 1.0 + nrm(ks[12], (DEPTH, D_MODEL), 0.05),
        "w_in": nrm(ks[13], (DEPTH, D_MODEL, PROJ_WIDTH), D_MODEL ** -0.5),
        "conv_w": nrm(ks[14], (DEPTH, CONV_WIDTH, CONV_CH), CONV_WIDTH ** -0.5),
        "conv_b": nrm(ks[15], (DEPTH, CONV_CH), 0.02),
        "conv_ln_g": 1.0 + nrm(ks[16], (DEPTH, CONV_CH), 0.05),
        "conv_ln_b": nrm(ks[17], (DEPTH, CONV_CH), 0.02),
        "w_out": nrm(ks[18], (DEPTH, D_MODEL, D_MODEL), D_MODEL ** -0.5),
        "ffn_w_gate": nrm(ks[19], (N_DENSE, D_MODEL, D_FF), D_MODEL ** -0.5),
        "ffn_w_up": nrm(ks[20], (N_DENSE, D_MODEL, D_FF), D_MODEL ** -0.5),
        "ffn_w_down": nrm(ks[21], (N_DENSE, D_FF, D_MODEL), D_FF ** -0.5),
        "moe_w_router": nrm(ks[22], (N_MOE, D_MODEL, N_EXPERTS), D_MODEL ** -0.5),
        "moe_w_gate": nrm(ks[23], (N_MOE, N_EXPERTS, D_MODEL, D_FF_EXPERT), D_MODEL ** -0.5),
        "moe_w_up": nrm(ks[24], (N_MOE, N_EXPERTS, D_MODEL, D_FF_EXPERT), D_MODEL ** -0.5),
        "moe_w_down": nrm(ks[25], (N_MOE, N_EXPERTS, D_FF_EXPERT, D_MODEL), D_FF_EXPERT ** -0.5),
    }


def reference(x_prompt, x_sample, cache_k, cache_v, state_conv, c_prompt, c_sample,
              w_mod, b_mod, g_pre_mix, g_post_mix, g_pre_ffn, g_post_ffn, w_in,
              conv_w, conv_b, conv_ln_g, conv_ln_b, w_out,
              ffn_w_gate, ffn_w_up, ffn_w_down,
              moe_w_router, moe_w_gate, moe_w_up, moe_w_down):
    p = dict(w_mod=w_mod, b_mod=b_mod, g_pre_mix=g_pre_mix, g_post_mix=g_post_mix,
             g_pre_ffn=g_pre_ffn, g_post_ffn=g_post_ffn, w_in=w_in, conv_w=conv_w, conv_b=conv_b,
             conv_ln_g=conv_ln_g, conv_ln_b=conv_ln_b, w_out=w_out,
             ffn_w_gate=ffn_w_gate, ffn_w_up=ffn_w_up, ffn_w_down=ffn_w_down,
             moe_w_router=moe_w_router, moe_w_gate=moe_w_gate, moe_w_up=moe_w_up,
             moe_w_down=moe_w_down)
    bp = x_prompt.shape[0]
    dt = x_prompt.dtype
    empty_kv = jnp.zeros((bp, 0, N_ATT_HEADS, HEAD_DIM), dt)
    zero_conv = jnp.zeros((bp, CONV_WIDTH - 1, CONV_CH), dt)

    yp, ys = x_prompt, x_sample
    kp_l, vp_l, cp_l, ks_l, vs_l, cs_l = [], [], [], [], [], []
    for l in range(DEPTH):
        yp, kp, vp, cp = layer_forward(yp, c_prompt, 0, empty_kv, empty_kv, zero_conv, l, p)
        ys, k_s, v_s, c_s = layer_forward(ys, c_sample, PAST_LEN, cache_k[l], cache_v[l],
                                          state_conv[l], l, p)
        kp_l.append(kp); vp_l.append(vp); cp_l.append(cp)
        ks_l.append(k_s); vs_l.append(v_s); cs_l.append(c_s)

    return (yp, ys, jnp.stack(kp_l), jnp.stack(vp_l), jnp.stack(cp_l),
            jnp.stack(ks_l), jnp.stack(vs_l), jnp.stack(cs_l))
```

```python
import functools

import numpy as np
import jax
import jax.numpy as jnp
from jax import lax
from jax.experimental import pallas as pl
from jax.experimental.pallas import tpu as pltpu

F32 = jnp.float32
BF16 = jnp.bfloat16

HEAD_DIM = 64
N_HEADS = 8
ATT_WIDTH = N_HEADS * HEAD_DIM
HEAD_PAIRS = ATT_WIDTH // 128
CONV_WIDTH = 31
CONV_HALO = 32
DILATION_PATTERNS = ((128, 1), (512, 4), (2048, 16))
BAND = 128
ROT_DIM = HEAD_DIM // 4
ROPE_THETA = 500000.0
ATTN_SCALE = HEAD_DIM ** -0.5
N_EXPERTS = 8
PAST_LEN = 2048
EPS = 1e-6
NEG = -1e30
MOD_ROWS = 136
VMEM_LIMIT = 56 * 1024 * 1024


def _rms(x, g):
    return x * lax.rsqrt(jnp.mean(x * x, axis=-1, keepdims=True) + EPS) * g


def _silu(x):
    return x * jax.nn.sigmoid(x)


def _params(n_axes, vmem=VMEM_LIMIT):
    return pltpu.CompilerParams(dimension_semantics=("arbitrary",) * n_axes, vmem_limit_bytes=vmem)


def _mod_kernel(c_ref, w_ref, b_ref, o_ref):
    a = _silu(c_ref[...]).astype(BF16)
    o_ref[0] = jnp.dot(a, w_ref[0].astype(BF16), preferred_element_type=F32) + b_ref[0]


def _mod_call(c_all, w_mod, b_mod):
    depth, d, d6 = w_mod.shape
    return pl.pallas_call(
        _mod_kernel,
        grid=(depth, d6 // d),
        in_specs=[pl.BlockSpec((MOD_ROWS, d), lambda l, j: (0, 0)),
                  pl.BlockSpec((1, d, d), lambda l, j: (l, 0, j)),
                  pl.BlockSpec((1, 1, d), lambda l, j: (l, 0, j))],
        out_specs=pl.BlockSpec((1, MOD_ROWS, d), lambda l, j: (l, 0, j)),
        out_shape=jax.ShapeDtypeStruct((depth, MOD_ROWS, d6), F32),
        compiler_params=_params(2),
        name="adaln_mod",
    )(c_all, w_mod, b_mod.reshape(depth, 1, d6))


def _inproj_kernel(*refs, conv, tm):
    if conv:
        (x_ref, sh_ref, sc_ref, g_ref, w_ref, cos_ref, s1_ref, s2_ref,
         cw_ref, cb_ref, lg_ref, lb_ref,
         q_ref, k_ref, v_ref, kt_ref, vt_ref, cv_ref, ul_ref, ubuf) = refs
    else:
        (x_ref, sh_ref, sc_ref, g_ref, w_ref, cos_ref, s1_ref, s2_ref,
         q_ref, k_ref, v_ref, kt_ref, vt_ref, u_ref) = refs

    h = _rms(x_ref[0], g_ref[...]) * (1.0 + sc_ref[0]) + sh_ref[0]
    proj = jnp.dot(h.astype(BF16), w_ref[...], preferred_element_type=F32)
    cos, s1, s2 = cos_ref[0], s1_ref[0], s2_ref[0]

    def rope(z):
        return z * cos + pltpu.roll(z, 128 - ROT_DIM // 2, 1) * s1 + pltpu.roll(z, ROT_DIM // 2, 1) * s2

    for hp in range(HEAD_PAIRS):
        lo, hi = hp * 128, (hp + 1) * 128
        q_ref[0, hp] = rope(proj[:, lo:hi]) * ATTN_SCALE
        kz = rope(proj[:, ATT_WIDTH + lo:ATT_WIDTH + hi])
        k_ref[0, hp] = kz
        kt_ref[0, :, lo:hi] = kz
        vz = proj[:, 2 * ATT_WIDTH + lo:2 * ATT_WIDTH + hi]
        v_ref[0, hp] = vz
        vt_ref[0, :, lo:hi] = vz

    a = proj[:, 3 * ATT_WIDTH:3 * ATT_WIDTH + 512]
    gate = proj[:, 3 * ATT_WIDTH + 512:]
    u = a * jax.nn.sigmoid(gate)
    if not conv:
        u_ref[0] = u
        return

    @pl.when(pl.program_id(1) == 0)
    def _():
        ubuf[0, 0:CONV_HALO, :] = jnp.zeros((CONV_HALO, 512), F32)

    ubuf[0, CONV_HALO:CONV_HALO + tm, :] = u
    span = tm + CONV_HALO - 8
    for s in range(1, 8):
        ubuf[s, 0:span, :] = ubuf[0, s:s + span, :]
    off0 = CONV_HALO - (CONV_WIDTH - 1)
    rows = 32

    def chunk(r, carry):
        base = pl.multiple_of(r * rows, rows)
        acc = jnp.broadcast_to(cb_ref[...], (rows, 512))
        for j in range(CONV_WIDTH):
            a, s = divmod(off0 + j, 8)
            acc = acc + cw_ref[j:j + 1, :] * ubuf[s, pl.ds(base + 8 * a, rows), :]
        mu = jnp.mean(acc, axis=-1, keepdims=True)
        xc = acc - mu
        var = jnp.mean(xc * xc, axis=-1, keepdims=True)
        y = xc * lax.rsqrt(var + EPS) * lg_ref[...] + lb_ref[...]
        cv_ref[0, pl.ds(base, rows), :] = _silu(y).astype(BF16)
        return carry

    lax.fori_loop(0, tm // rows, chunk, 0)
    tail = ubuf[0, tm:tm + CONV_HALO, :]
    ul_ref[0] = tail
    ubuf[0, 0:CONV_HALO, :] = tail


def _mod_spec(rows, chunk, index_fn):
    return pl.BlockSpec((1, rows, 1024), lambda *g: index_fn(*g) + (chunk,))


def _inproj_call(x, mod, mod_rows, mod_idx, g, w_bf, tabs, tab_rows, conv_p, *, tm, keep, name):
    b, t, d = x.shape
    nt = t // tm
    off = (t - keep) // tm
    conv = conv_p is not None
    tab_spec = pl.BlockSpec((1, tab_rows, 128), lambda bi, i: (i, 0, 0))
    in_specs = [pl.BlockSpec((1, tm, d), lambda bi, i: (bi, i, 0)),
                _mod_spec(mod_rows, 0, mod_idx), _mod_spec(mod_rows, 1, mod_idx),
                pl.BlockSpec((1, d), lambda bi, i: (0, 0)),
                pl.BlockSpec(w_bf.shape, lambda bi, i: (0, 0)),
                tab_spec, tab_spec, tab_spec]
    args = [x, mod, mod, g, w_bf, *tabs]
    hp_spec = pl.BlockSpec((1, HEAD_PAIRS, tm, 128), lambda bi, i: (bi, 0, i, 0))
    tail_spec = pl.BlockSpec((1, tm, 512), lambda bi, i: (bi, jnp.maximum(i - off, 0), 0))
    hp_shape = jax.ShapeDtypeStruct((b, HEAD_PAIRS, t, 128), F32)
    tail_shape = jax.ShapeDtypeStruct((b, keep, 512), F32)
    out_specs = [hp_spec, hp_spec, hp_spec, tail_spec, tail_spec]
    out_shape = [hp_shape, hp_shape, hp_shape, tail_shape, tail_shape]
    scratch = []
    if conv:
        small = pl.BlockSpec((1, 512), lambda bi, i: (0, 0))
        in_specs += [pl.BlockSpec((32, 512), lambda bi, i: (0, 0)), small, small, small]
        args += list(conv_p)
        out_specs += [pl.BlockSpec((1, tm, 512), lambda bi, i: (bi, i, 0)),
                      pl.BlockSpec((1, CONV_HALO, 512), lambda bi, i: (bi, 0, 0))]
        out_shape += [jax.ShapeDtypeStruct((b, t, 512), BF16),
                      jax.ShapeDtypeStruct((b, CONV_HALO, 512), F32)]
        scratch = [pltpu.VMEM((8, tm + CONV_HALO, 512), F32)]
    else:
        out_specs += [pl.BlockSpec((1, tm, 512), lambda bi, i: (bi, i, 0))]
        out_shape += [jax.ShapeDtypeStruct((b, t, 512), F32)]
    return pl.pallas_call(
        functools.partial(_inproj_kernel, conv=conv, tm=tm),
        grid=(b, nt), in_specs=in_specs, out_specs=out_specs, out_shape=out_shape,
        scratch_shapes=scratch, compiler_params=_params(2), name=name,
    )(*args)


def _attn_kernel(q_ref, kc_ref, kp_ref, vc_ref, vp_ref, o_ref, acc_s, m_s, l_s, *, tq):
    first_kj = jnp.where(pl.program_id(2) == 0, 128, 0)
    lane_lo = lax.broadcasted_iota(jnp.int32, (128, 128), 1) < HEAD_DIM
    qi = lax.broadcasted_iota(jnp.int32, (256, 256), 0)
    qi = jnp.where(qi >= 128, qi - 128, qi)
    kj = lax.broadcasted_iota(jnp.int32, (256, 256), 1)
    band = (kj >= qi) & (kj <= qi + BAND)
    band_first = band & (kj >= first_kj)

    def block(q2, k2, v2, mask):
        qa = jnp.where(lane_lo, q2, 0.0)
        qb = jnp.where(lane_lo, 0.0, q2)
        qq = jnp.concatenate([qa, qb], axis=0).astype(BF16)
        s = lax.dot_general(qq, k2.astype(BF16), (((1,), (1,)), ((), ())),
                            preferred_element_type=F32)
        s = jnp.where(mask, s, NEG)
        mb = jnp.max(s, axis=1, keepdims=True)
        p = jnp.exp(s - mb)
        lb = jnp.sum(p, axis=1, keepdims=True)
        pv = jnp.dot(p.astype(BF16), v2.astype(BF16), preferred_element_type=F32)
        mt = jnp.where(lane_lo, mb[:128], mb[128:])
        lt = jnp.where(lane_lo, lb[:128], lb[128:])
        pvt = jnp.where(lane_lo, pv[:128], pv[128:])
        return mt, lt, pvt

    def keys_first(kref_p, kref_c, c, d):
        return jnp.concatenate([kref_p[0, 0, pl.ds(tq - BAND * d + c, 128, stride=d), :],
                                kref_c[0, 0, pl.ds(c, 128, stride=d), :]], axis=0)

    def merge(rows, mt, lt, pvt):
        mp, lp, ap = m_s[rows, :], l_s[rows, :], acc_s[rows, :]
        mn = jnp.maximum(mp, mt)
        a = jnp.exp(mp - mn)
        b = jnp.exp(mt - mn)
        return mn, a * lp + b * lt, a * ap + b * pvt

    def merge_store(rows, mt, lt, pvt):
        mn, ln, an = merge(rows, mt, lt, pvt)
        m_s[rows, :] = mn
        l_s[rows, :] = ln
        acc_s[rows, :] = an

    d = 16

    def body16(c, carry):
        rows = pl.ds(c, 128, stride=16)
        mt, lt, pvt = block(q_ref[0, 0, rows, :], keys_first(kp_ref, kc_ref, c, 16),
                            keys_first(vp_ref, vc_ref, c, 16), band_first)
        m_s[rows, :] = mt
        l_s[rows, :] = lt
        acc_s[rows, :] = pvt
        return carry

    lax.fori_loop(0, 16, body16, 0)

    def body4_first(c, carry):
        rows = pl.ds(c, 128, stride=4)
        mt, lt, pvt = block(q_ref[0, 0, rows, :], keys_first(kp_ref, kc_ref, c, 4),
                            keys_first(vp_ref, vc_ref, c, 4), band_first)
        merge_store(rows, mt, lt, pvt)
        return carry

    lax.fori_loop(0, 4, body4_first, 0)

    def body4(n, carry):
        c = n % 4
        s = 1 + n // 4
        rows = pl.ds(c + 512 * s, 128, stride=4)
        krows = pl.ds(c + 512 * (s - 1), 256, stride=4)
        mt, lt, pvt = block(q_ref[0, 0, rows, :], kc_ref[0, 0, krows, :], vc_ref[0, 0, krows, :], band)
        merge_store(rows, mt, lt, pvt)
        return carry

    lax.fori_loop(0, 4 * (tq // 512 - 1), body4, 0)

    def finish(rows, mt, lt, pvt):
        _, ln, an = merge(rows, mt, lt, pvt)
        o_ref[0, 0, rows, :] = (an / ln).astype(o_ref.dtype)

    rows0 = pl.ds(0, 128)
    mt, lt, pvt = block(q_ref[0, 0, rows0, :], keys_first(kp_ref, kc_ref, 0, 1),
                        keys_first(vp_ref, vc_ref, 0, 1), band_first)
    finish(rows0, mt, lt, pvt)

    def body1(s, carry):
        rows = pl.ds(pl.multiple_of(128 * s, 128), 128)
        krows = pl.ds(pl.multiple_of(128 * (s - 1), 128), 256)
        mt, lt, pvt = block(q_ref[0, 0, rows, :], kc_ref[0, 0, krows, :], vc_ref[0, 0, krows, :], band)
        finish(rows, mt, lt, pvt)
        return carry

    lax.fori_loop(1, tq // 128, body1, 0)


def _attn_call(q, k, v, *, tq=2048):
    b, hp, t, _ = q.shape
    assert tq == BAND * 16 and t % tq == 0
    cur = pl.BlockSpec((1, 1, tq, 128), lambda bi, h, i: (bi, h, i, 0))
    prev = pl.BlockSpec((1, 1, tq, 128), lambda bi, h, i: (bi, h, jnp.maximum(i - 1, 0), 0))
    return pl.pallas_call(
        functools.partial(_attn_kernel, tq=tq),
        grid=(b, hp, t // tq),
        in_specs=[cur, cur, prev, cur, prev],
        out_specs=cur,
        out_shape=jax.ShapeDtypeStruct((b, hp, t, 128), BF16),
        scratch_shapes=[pltpu.VMEM((tq, 128), F32)] * 3,
        compiler_params=_params(3), name="dilated_attn_prompt",
    )(q, k, k, v, v)


def _sample_attn_kernel(q_ref, kn_ref, vn_ref, kc_ref, vc_ref, mh_ref, mn_ref, selq_ref, selp_ref, o_ref,
                        *, n_new):
    q = q_ref[0]
    qrep = jnp.concatenate([jnp.broadcast_to(q[t:t + 1], (N_HEADS, ATT_WIDTH)) for t in range(n_new)], axis=0)
    qbd = (qrep * selq_ref[...]).astype(BF16)
    dn = (((1,), (1,)), ((), ()))
    s_h = lax.dot_general(kc_ref[0].astype(BF16), qbd, dn, preferred_element_type=F32)
    s_n = lax.dot_general(kn_ref[0].astype(BF16), qbd, dn, preferred_element_type=F32)
    mh, mn = mh_ref[...], mn_ref[...]
    s_h = jnp.where(mh > 0, s_h, NEG)
    s_n = jnp.where(mn > 0, s_n, NEG)
    m = jnp.maximum(jnp.max(s_h, axis=0, keepdims=True), jnp.max(s_n, axis=0, keepdims=True))
    p_h = mh * jnp.exp(s_h - m)
    p_n = mn * jnp.exp(s_n - m)
    inv = 1.0 / (jnp.sum(p_h, axis=0, keepdims=True) + jnp.sum(p_n, axis=0, keepdims=True))
    p_h = (p_h * inv).astype(BF16)
    p_n = (p_n * inv).astype(BF16)
    vc, vn = vc_ref[0], vn_ref[0]
    for t in range(n_new):
        sel = selp_ref[t]
        pe_h = jnp.dot(p_h, sel, preferred_element_type=F32)
        pe_n = jnp.dot(p_n, sel, preferred_element_type=F32)
        o_ref[0, t:t + 1, :] = (jnp.sum(pe_h * vc, axis=0, keepdims=True)
                                + jnp.sum(pe_n * vn, axis=0, keepdims=True))


def _sample_tables(w_buf, n_new):
    rows = np.arange(w_buf + n_new)
    mult = np.zeros((w_buf + n_new, n_new), np.float32)
    for t in range(n_new):
        dist = w_buf + t - rows
        for window, dil in DILATION_PATTERNS:
            mult[:, t] += (dist >= 0) & (dist <= window) & (dist % dil == 0)
    mult = np.repeat(mult, N_HEADS, axis=1)
    head_of_lane = np.arange(ATT_WIDTH) // HEAD_DIM
    col_t = np.repeat(np.arange(n_new), N_HEADS)
    col_h = np.tile(np.arange(N_HEADS), n_new)
    selq = (col_h[:, None] == head_of_lane[None, :]).astype(np.float32)
    selp = np.stack([selq * (col_t[:, None] == t) for t in range(n_new)])
    return (jnp.asarray(mult[:w_buf]), jnp.asarray(mult[w_buf:]), jnp.asarray(selq),
            jnp.asarray(selp, dtype=BF16))


def _sample_attn_call(q, kn, vn, cache_k, cache_v):
    nb, n_new, _ = q.shape
    w_buf = cache_k.shape[1]
    mh, mn, selq, selp = _sample_tables(w_buf, n_new)
    cols = n_new * N_HEADS
    new_spec = pl.BlockSpec((1, n_new, ATT_WIDTH), lambda i: (i, 0, 0))
    cache_spec = pl.BlockSpec((1, w_buf, ATT_WIDTH), lambda i: (i, 0, 0))
    return pl.pallas_call(
        functools.partial(_sample_attn_kernel, n_new=n_new),
        grid=(nb,),
        in_specs=[new_spec, new_spec, new_spec, cache_spec, cache_spec,
                  pl.BlockSpec((w_buf, cols), lambda i: (0, 0)),
                  pl.BlockSpec((n_new, cols), lambda i: (0, 0)),
                  pl.BlockSpec((cols, ATT_WIDTH), lambda i: (0, 0)),
                  pl.BlockSpec((n_new, cols, ATT_WIDTH), lambda i: (0, 0, 0))],
        out_specs=new_spec,
        out_shape=jax.ShapeDtypeStruct((nb, n_new, ATT_WIDTH), F32),
        compiler_params=_params(1), name="dilated_attn_sample",
    )(q, kn, vn, cache_k, cache_v, mh, mn, selq, selp)


def _sample_conv_kernel(uf_ref, cw_ref, cb_ref, lg_ref, lb_ref, o_ref, *, n_new, nb):
    rows = 32

    def body(n, carry):
        t = n // (nb // rows)
        base = pl.multiple_of((n % (nb // rows)) * rows, rows)
        acc = jnp.broadcast_to(cb_ref[...], (rows, 512))
        for j in range(CONV_WIDTH):
            acc = acc + cw_ref[j:j + 1, :] * uf_ref[t + j, pl.ds(base, rows), :]
        mu = jnp.mean(acc, axis=-1, keepdims=True)
        xc = acc - mu
        var = jnp.mean(xc * xc, axis=-1, keepdims=True)
        y = xc * lax.rsqrt(var + EPS) * lg_ref[...] + lb_ref[...]
        o_ref[t, pl.ds(base, rows), :] = _silu(y).astype(BF16)
        return carry

    lax.fori_loop(0, n_new * (nb // rows), body, 0)


def _sample_conv_call(ufull, conv_p):
    n_full, nb, ch = ufull.shape
    n_new = n_full - (CONV_WIDTH - 1)
    return pl.pallas_call(
        functools.partial(_sample_conv_kernel, n_new=n_new, nb=nb),
        out_shape=jax.ShapeDtypeStruct((n_new, nb, ch), BF16),
        compiler_params=pltpu.CompilerParams(vmem_limit_bytes=VMEM_LIMIT),
        name="conv_sample",
    )(ufull, *conv_p)


def _mix_residual(att_ref, cv_ref, x_ref, gt1_ref, sc2_ref, sh2_ref, gpm_ref, gpf_ref, wo_ref):
    a = jnp.concatenate([att_ref[0, hp] for hp in range(HEAD_PAIRS)] + [cv_ref[0]], axis=-1)
    mix = jnp.dot(a, wo_ref[...], preferred_element_type=F32)
    x1 = x_ref[0] + gt1_ref[0] * _rms(mix, gpm_ref[...])
    h2 = _rms(x1, gpf_ref[...]) * (1.0 + sc2_ref[0]) + sh2_ref[0]
    return x1, h2


def _ffn_kernel(att_ref, cv_ref, x_ref, gt1_ref, sc2_ref, sh2_ref, gt2_ref, gpm_ref, gpf_ref, gqf_ref,
                wo_ref, wg_ref, wu_ref, wd_ref, o_ref, x1_s, h2_s, acc_s):
    f = pl.program_id(1)

    @pl.when(f == 0)
    def _():
        x1, h2 = _mix_residual(att_ref, cv_ref, x_ref, gt1_ref, sc2_ref, sh2_ref, gpm_ref, gpf_ref, wo_ref)
        x1_s[...] = x1
        h2_s[...] = h2.astype(BF16)
        acc_s[...] = jnp.zeros_like(acc_s)

    h2 = h2_s[...]
    g = jnp.dot(h2, wg_ref[...].astype(BF16), preferred_element_type=F32)
    u = jnp.dot(h2, wu_ref[...].astype(BF16), preferred_element_type=F32)
    act = (_silu(g) * u).astype(BF16)
    acc_s[...] += jnp.dot(act, wd_ref[...].astype(BF16), preferred_element_type=F32)

    @pl.when(f == pl.num_programs(1) - 1)
    def _():
        o_ref[0] = x1_s[...] + gt2_ref[0] * _rms(acc_s[...], gqf_ref[...])


def _mix_in_specs(b, t, tm, mod_rows, mod_idx, d):
    ntb = t // tm
    row = lambda i, *_: (i // ntb, i % ntb, 0)
    specs = [pl.BlockSpec((1, HEAD_PAIRS, tm, 128), lambda i, *_: (i // ntb, 0, i % ntb, 0)),
             pl.BlockSpec((1, tm, 512), row),
             pl.BlockSpec((1, tm, d), row)]
    return specs, row


def _ffn_call(att, cv, x, mod, mod_rows, mod_idx, gpm, gpf, gqf, wo_bf, wg, wu, wd, *, tm, tf, name):
    b, t, d = x.shape
    ff = wg.shape[1]
    ntb = t // tm
    specs, row = _mix_in_specs(b, t, tm, mod_rows, mod_idx, d)
    vec = pl.BlockSpec((1, d), lambda i, f: (0, 0))
    in_specs = specs + [_mod_spec(mod_rows, 2, mod_idx), _mod_spec(mod_rows, 4, mod_idx),
                        _mod_spec(mod_rows, 3, mod_idx), _mod_spec(mod_rows, 5, mod_idx),
                        vec, vec, vec,
                        pl.BlockSpec((d, d), lambda i, f: (0, 0)),
                        pl.BlockSpec((d, tf), lambda i, f: (0, f)),
                        pl.BlockSpec((d, tf), lambda i, f: (0, f)),
                        pl.BlockSpec((tf, d), lambda i, f: (f, 0))]
    return pl.pallas_call(
        _ffn_kernel,
        grid=(b * ntb, ff // tf),
        in_specs=in_specs,
        out_specs=pl.BlockSpec((1, tm, d), row),
        out_shape=jax.ShapeDtypeStruct((b, t, d), F32),
        scratch_shapes=[pltpu.VMEM((tm, d), F32), pltpu.VMEM((tm, d), BF16), pltpu.VMEM((tm, d), F32)],
        compiler_params=_params(2), name=name,
    )(att, cv, x, mod, mod, mod, mod, gpm, gpf, gqf, wo_bf, wg, wu, wd)


def _router_kernel(att_ref, cv_ref, x_ref, gt1_ref, sc2_ref, sh2_ref, gpm_ref, gpf_ref, wo_ref, wr_ref,
                   x1_ref, h2_ref, gate_ref):
    x1, h2 = _mix_residual(att_ref, cv_ref, x_ref, gt1_ref, sc2_ref, sh2_ref, gpm_ref, gpf_ref, wo_ref)
    x1_ref[0] = x1
    h2_ref[0] = h2.astype(BF16)
    logits = jnp.dot(h2, wr_ref[...], preferred_element_type=F32, precision=lax.Precision.HIGHEST)
    lane = lax.broadcasted_iota(jnp.int32, logits.shape, 1).astype(F32)
    v1 = jnp.max(logits, axis=-1, keepdims=True)
    i1 = jnp.min(jnp.where(logits == v1, lane, float(N_EXPERTS)), axis=-1, keepdims=True)
    oh1 = lane == i1
    rest = jnp.where(oh1, -jnp.inf, logits)
    v2 = jnp.max(rest, axis=-1, keepdims=True)
    i2 = jnp.min(jnp.where(rest == v2, lane, float(N_EXPERTS)), axis=-1, keepdims=True)
    oh2 = lane == i2
    e2 = jnp.exp(v2 - v1)
    den = 1.0 + e2
    gate_ref[0] = jnp.where(oh1, 1.0 / den, 0.0) + jnp.where(oh2, e2 / den, 0.0)


def _router_call(att, cv, x, mod, mod_rows, mod_idx, gpm, gpf, wo_bf, wr, *, tm, name):
    b, t, d = x.shape
    specs, row = _mix_in_specs(b, t, tm, mod_rows, mod_idx, d)
    vec = pl.BlockSpec((1, d), lambda i: (0, 0))
    in_specs = specs + [_mod_spec(mod_rows, 2, mod_idx), _mod_spec(mod_rows, 4, mod_idx),
                        _mod_spec(mod_rows, 3, mod_idx), vec, vec,
                        pl.BlockSpec((d, d), lambda i: (0, 0)),
                        pl.BlockSpec((d, N_EXPERTS), lambda i: (0, 0))]
    return pl.pallas_call(
        _router_kernel,
        grid=(b * (t // tm),),
        in_specs=in_specs,
        out_specs=[pl.BlockSpec((1, tm, d), row), pl.BlockSpec((1, tm, d), row),
                   pl.BlockSpec((1, tm, N_EXPERTS), row)],
        out_shape=[jax.ShapeDtypeStruct((b, t, d), F32), jax.ShapeDtypeStruct((b, t, d), BF16),
                   jax.ShapeDtypeStruct((b, t, N_EXPERTS), F32)],
        compiler_params=_params(1), name=name,
    )(att, cv, x, mod, mod, mod, gpm, gpf, wo_bf, wr)


def _moe_kernel(h2_ref, gate_ref, x1_ref, gt2_ref, gqf_ref, wg_ref, wu_ref, wd_ref, o_ref, acc_s):
    e, f = pl.program_id(1), pl.program_id(2)

    @pl.when((e == 0) & (f == 0))
    def _():
        acc_s[...] = jnp.zeros_like(acc_s)

    h2 = h2_ref[0]
    g = jnp.dot(h2, wg_ref[0].astype(BF16), preferred_element_type=F32)
    u = jnp.dot(h2, wu_ref[0].astype(BF16), preferred_element_type=F32)
    act = (_silu(g) * u).astype(BF16)
    acc_s[...] += gate_ref[0, 0] * jnp.dot(act, wd_ref[0].astype(BF16), preferred_element_type=F32)

    @pl.when((e == pl.num_programs(1) - 1) & (f == pl.num_programs(2) - 1))
    def _():
        o_ref[0] = x1_ref[0] + gt2_ref[0] * _rms(acc_s[...], gqf_ref[...])


def _moe_call(h2, gates_t, x1, mod, mod_rows, mod_idx, gqf, wg, wu, wd, *, tm, tf, name):
    b, t, d = x1.shape
    n_e, _, ff = wg.shape
    ntb = t // tm
    row = lambda i, e, f: (i // ntb, i % ntb, 0)
    in_specs = [pl.BlockSpec((1, tm, d), row),
                pl.BlockSpec((1, 1, tm, 1), lambda i, e, f: (i // ntb, e, i % ntb, 0)),
                pl.BlockSpec((1, tm, d), row),
                _mod_spec(mod_rows, 5, mod_idx),
                pl.BlockSpec((1, d), lambda i, e, f: (0, 0)),
                pl.BlockSpec((1, d, tf), lambda i, e, f: (e, 0, f)),
                pl.BlockSpec((1, d, tf), lambda i, e, f: (e, 0, f)),
                pl.BlockSpec((1, tf, d), lambda i, e, f: (e, f, 0))]
    return pl.pallas_call(
        _moe_kernel,
        grid=(b * ntb, n_e, ff // tf),
        in_specs=in_specs,
        out_specs=pl.BlockSpec((1, tm, d), row),
        out_shape=jax.ShapeDtypeStruct((b, t, d), F32),
        scratch_shapes=[pltpu.VMEM((tm, d), F32)],
        compiler_params=_params(3), name=name,
    )(h2, gates_t, x1, mod, gqf, wg, wu, wd)


def _rope_tables(pos):
    half = ROT_DIM // 2
    inv_freq = ROPE_THETA ** (-jnp.arange(half, dtype=F32) * 2.0 / ROT_DIM)
    ang = pos.astype(F32)[:, None] * inv_freq[None, :]
    cos, sin = jnp.cos(ang), jnp.sin(ang)
    l64 = np.arange(128) % HEAD_DIM
    idx = l64 % half
    first = (l64 < half)[None, :]
    second = ((l64 >= half) & (l64 < ROT_DIM))[None, :]
    cos_t = jnp.where(first | second, cos[:, idx], 1.0)
    s1_t = jnp.where(first, -sin[:, idx], 0.0)
    s2_t = jnp.where(second, sin[:, idx], 0.0)
    return cos_t, s1_t, s2_t


def kernel(x_prompt, x_sample, cache_k, cache_v, state_conv, c_prompt, c_sample, w_mod, b_mod, g_pre_mix, g_post_mix, g_pre_ffn, g_post_ffn, w_in, conv_w, conv_b, conv_ln_g, conv_ln_b, w_out, ffn_w_gate, ffn_w_up, ffn_w_down, moe_w_router, moe_w_gate, moe_w_up, moe_w_down):
    bp, seq, d = x_prompt.shape
    nb, n_new, _ = x_sample.shape
    depth = w_mod.shape[0]
    w_buf = cache_k.shape[2]
    past_len = PAST_LEN
    assert w_buf == min(DILATION_PATTERNS[-1][0], past_len)
    keep = min(DILATION_PATTERNS[-1][0], seq)
    ns = nb * n_new

    c_all = jnp.concatenate([c_sample, c_prompt, jnp.zeros((MOD_ROWS - nb - bp, d), F32)], axis=0)
    mod_all = _mod_call(c_all, w_mod, b_mod)

    tm_p = 512
    tabs_p = tuple(tb.reshape(seq // tm_p, tm_p, 128) for tb in _rope_tables(jnp.arange(seq, dtype=jnp.int32)))
    tabs_s = tuple(tb.reshape(n_new, 1, 128)
                   for tb in _rope_tables(past_len + jnp.arange(n_new, dtype=jnp.int32)))

    yp = x_prompt
    ys = x_sample.transpose(1, 0, 2).reshape(1, ns, d)
    outs = [[] for _ in range(6)]
    for l in range(depth):
        mod_p = mod_all[l].reshape(MOD_ROWS, 1, 6 * d)
        mod_s = jnp.tile(mod_all[l, :nb], (n_new, 1))[None]
        idx_p2 = lambda bi, i: (nb + bi, 0)
        idx_s2 = lambda bi, i: (0, i)
        w_in_bf = w_in[l].astype(BF16)
        w_out_bf = w_out[l].astype(BF16)
        conv_p = (jnp.pad(conv_w[l], ((0, 1), (0, 0))), conv_b[l][None], conv_ln_g[l][None], conv_ln_b[l][None])
        gpre, gpm, gpf, gqf = g_pre_mix[l][None], g_post_mix[l][None], g_pre_ffn[l][None], g_post_ffn[l][None]

        q, k, v, kt, vt, cv, ul = _inproj_call(yp, mod_p, 1, idx_p2, gpre, w_in_bf, tabs_p, tm_p, conv_p,
                                               tm=tm_p, keep=keep, name=f"inproj_prompt_{l}")
        att = _attn_call(q, k, v)
        outs[0].append(kt.reshape(bp, keep, N_HEADS, HEAD_DIM))
        outs[1].append(vt.reshape(bp, keep, N_HEADS, HEAD_DIM))
        outs[2].append(ul[:, CONV_HALO - (CONV_WIDTH - 1):])

        qs, _, _, kts, vts, us = _inproj_call(ys, mod_s, nb, idx_s2, gpre, w_in_bf, tabs_s, 1, None,
                                              tm=nb, keep=ns, name=f"inproj_sample_{l}")
        to_batch_major = lambda z: z.reshape(n_new, nb, ATT_WIDTH).transpose(1, 0, 2)
        q_bm = qs.reshape(HEAD_PAIRS, n_new, nb, 128).transpose(2, 1, 0, 3).reshape(nb, n_new, ATT_WIDTH)
        k_bm, v_bm = to_batch_major(kts), to_batch_major(vts)
        att_s = _sample_attn_call(q_bm, k_bm, v_bm, cache_k[l].reshape(nb, w_buf, ATT_WIDTH),
                                  cache_v[l].reshape(nb, w_buf, ATT_WIDTH))
        att_s = (att_s.reshape(nb, n_new, HEAD_PAIRS, 128).transpose(2, 1, 0, 3)
                 .reshape(1, HEAD_PAIRS, ns, 128).astype(BF16))
        ufull = jnp.concatenate([state_conv[l].transpose(1, 0, 2), us.reshape(n_new, nb, 512)], axis=0)
        cv_s = _sample_conv_call(ufull, conv_p).reshape(1, ns, 512)
        outs[3].append(k_bm.reshape(nb, n_new, N_HEADS, HEAD_DIM))
        outs[4].append(v_bm.reshape(nb, n_new, N_HEADS, HEAD_DIM))
        outs[5].append(ufull[n_new:].transpose(1, 0, 2))

        idx_p1 = lambda i, *_: (nb + i // (seq // tm_f), 0)
        idx_s1 = lambda i, *_: (0, i)
        if l % 2 == 0:
            tm_f = 1024
            wg, wu, wd = ffn_w_gate[l // 2], ffn_w_up[l // 2], ffn_w_down[l // 2]
            yp = _ffn_call(att, cv, yp, mod_p, 1, idx_p1, gpm, gpf, gqf, w_out_bf, wg, wu, wd,
                           tm=tm_f, tf=256, name=f"ffn_prompt_{l}")
            ys = _ffn_call(att_s, cv_s, ys, mod_s, ns, idx_s1, gpm, gpf, gqf, w_out_bf, wg, wu, wd,
                           tm=ns, tf=512, name=f"ffn_sample_{l}")
        else:
            tm_f = 1024
            wr = moe_w_router[l // 2]
            wg, wu, wd = moe_w_gate[l // 2], moe_w_up[l // 2], moe_w_down[l // 2]
            x1, h2, gates = _router_call(att, cv, yp, mod_p, 1, idx_p1, gpm, gpf, w_out_bf, wr,
                                         tm=tm_f, name=f"router_prompt_{l}")
            yp = _moe_call(h2, gates.transpose(0, 2, 1)[..., None], x1, mod_p, 1, idx_p1, gqf, wg, wu, wd,
                           tm=tm_f, tf=512, name=f"moe_prompt_{l}")
            x1s, h2s, gates_s = _router_call(att_s, cv_s, ys, mod_s, ns, idx_s1, gpm, gpf, w_out_bf, wr,
                                             tm=ns, name=f"router_sample_{l}")
            ys = _moe_call(h2s, gates_s.transpose(0, 2, 1)[..., None], x1s, mod_s, ns, idx_s1, gqf, wg, wu, wd,
                           tm=ns, tf=512, name=f"moe_sample_{l}")

    y_sample = ys.reshape(n_new, nb, d).transpose(1, 0, 2)
    return (yp, y_sample) + tuple(jnp.stack(o) for o in outs)
```

```python
import functools

import numpy as np
import jax
import jax.numpy as jnp
from jax import lax
from jax.experimental import pallas as pl
from jax.experimental.pallas import tpu as pltpu

F32 = jnp.float32
BF16 = jnp.bfloat16

HEAD_DIM = 64
N_HEADS = 8
ATT_WIDTH = N_HEADS * HEAD_DIM
HEAD_PAIRS = ATT_WIDTH // 128
CONV_WIDTH = 31
CONV_HALO = 32
DILATION_PATTERNS = ((128, 1), (512, 4), (2048, 16))
BAND = 128
ROT_DIM = HEAD_DIM // 4
ROPE_THETA = 500000.0
ATTN_SCALE = HEAD_DIM ** -0.5
N_EXPERTS = 8
PAST_LEN = 2048
EPS = 1e-6
NEG = -1e30
MOD_ROWS = 136
VMEM_LIMIT = 56 * 1024 * 1024


def _rms(x, g):
    return x * lax.rsqrt(jnp.mean(x * x, axis=-1, keepdims=True) + EPS) * g


def _silu(x):
    return x * jax.nn.sigmoid(x)


def _params(n_axes, vmem=VMEM_LIMIT):
    return pltpu.CompilerParams(dimension_semantics=("arbitrary",) * n_axes, vmem_limit_bytes=vmem)


def _mod_kernel(c_ref, w_ref, b_ref, o_ref):
    a = _silu(c_ref[...]).astype(BF16)
    o_ref[0] = jnp.dot(a, w_ref[0].astype(BF16), preferred_element_type=F32) + b_ref[0]


def _mod_call(c_all, w_mod, b_mod):
    depth, d, d6 = w_mod.shape
    return pl.pallas_call(
        _mod_kernel,
        grid=(depth, d6 // d),
        in_specs=[pl.BlockSpec((MOD_ROWS, d), lambda l, j: (0, 0)),
                  pl.BlockSpec((1, d, d), lambda l, j: (l, 0, j)),
                  pl.BlockSpec((1, 1, d), lambda l, j: (l, 0, j))],
        out_specs=pl.BlockSpec((1, MOD_ROWS, d), lambda l, j: (l, 0, j)),
        out_shape=jax.ShapeDtypeStruct((depth, MOD_ROWS, d6), F32),
        compiler_params=_params(2),
        name="adaln_mod",
    )(c_all, w_mod, b_mod.reshape(depth, 1, d6))


def _inproj_kernel(*refs, conv, tm):
    if conv:
        (x_ref, sh_ref, sc_ref, g_ref, w_ref, cos_ref, s1_ref, s2_ref,
         cw_ref, cb_ref, lg_ref, lb_ref,
         q_ref, k_ref, v_ref, kt_ref, vt_ref, cv_ref, ul_ref, ubuf) = refs
    else:
        (x_ref, sh_ref, sc_ref, g_ref, w_ref, cos_ref, s1_ref, s2_ref,
         q_ref, k_ref, v_ref, kt_ref, vt_ref, u_ref) = refs

    h = _rms(x_ref[0], g_ref[...]) * (1.0 + sc_ref[0]) + sh_ref[0]
    proj = jnp.dot(h.astype(BF16), w_ref[...], preferred_element_type=F32)
    cos, s1, s2 = cos_ref[0], s1_ref[0], s2_ref[0]

    def rope(z):
        return z * cos + pltpu.roll(z, 128 - ROT_DIM // 2, 1) * s1 + pltpu.roll(z, ROT_DIM // 2, 1) * s2

    for hp in range(HEAD_PAIRS):
        lo, hi = hp * 128, (hp + 1) * 128
        q_ref[0, hp] = rope(proj[:, lo:hi]) * ATTN_SCALE
        kz = rope(proj[:, ATT_WIDTH + lo:ATT_WIDTH + hi])
        k_ref[0, hp] = kz
        kt_ref[0, :, lo:hi] = kz
        vz = proj[:, 2 * ATT_WIDTH + lo:2 * ATT_WIDTH + hi]
        v_ref[0, hp] = vz
        vt_ref[0, :, lo:hi] = vz

    a = proj[:, 3 * ATT_WIDTH:3 * ATT_WIDTH + 512]
    gate = proj[:, 3 * ATT_WIDTH + 512:]
    u = a * jax.nn.sigmoid(gate)
    if not conv:
        u_ref[0] = u
        return

    @pl.when(pl.program_id(1) == 0)
    def _():
        ubuf[0, 0:CONV_HALO, :] = jnp.zeros((CONV_HALO, 512), F32)

    ubuf[0, CONV_HALO:CONV_HALO + tm, :] = u
    span = tm + CONV_HALO - 8
    for s in range(1, 8):
        ubuf[s, 0:span, :] = ubuf[0, s:s + span, :]
    off0 = CONV_HALO - (CONV_WIDTH - 1)
    rows = 32

    def chunk(r, carry):
        base = pl.multiple_of(r * rows, rows)
        acc = jnp.broadcast_to(cb_ref[...], (rows, 512))
        for j in range(CONV_WIDTH):
            a, s = divmod(off0 + j, 8)
            acc = acc + cw_ref[j:j + 1, :] * ubuf[s, pl.ds(base + 8 * a, rows), :]
        mu = jnp.mean(acc, axis=-1, keepdims=True)
        xc = acc - mu
        var = jnp.mean(xc * xc, axis=-1, keepdims=True)
        y = xc * lax.rsqrt(var + EPS) * lg_ref[...] + lb_ref[...]
        cv_ref[0, pl.ds(base, rows), :] = _silu(y).astype(BF16)
        return carry

    lax.fori_loop(0, tm // rows, chunk, 0)
    tail = ubuf[0, tm:tm + CONV_HALO, :]
    ul_ref[0] = tail
    ubuf[0, 0:CONV_HALO, :] = tail


def _mod_spec(rows, chunk, index_fn):
    return pl.BlockSpec((1, rows, 1024), lambda *g: index_fn(*g) + (chunk,))


def _inproj_call(x, mod, mod_rows, mod_idx, g, w_bf, tabs, tab_rows, conv_p, *, tm, keep, name):
    b, t, d = x.shape
    nt = t // tm
    off = (t - keep) // tm
    conv = conv_p is not None
    tab_spec = pl.BlockSpec((1, tab_rows, 128), lambda bi, i: (i, 0, 0))
    in_specs = [pl.BlockSpec((1, tm, d), lambda bi, i: (bi, i, 0)),
                _mod_spec(mod_rows, 0, mod_idx), _mod_spec(mod_rows, 1, mod_idx),
                pl.BlockSpec((1, d), lambda bi, i: (0, 0)),
                pl.BlockSpec(w_bf.shape, lambda bi, i: (0, 0)),
                tab_spec, tab_spec, tab_spec]
    args = [x, mod, mod, g, w_bf, *tabs]
    hp_spec = pl.BlockSpec((1, HEAD_PAIRS, tm, 128), lambda bi, i: (bi, 0, i, 0))
    tail_spec = pl.BlockSpec((1, tm, 512), lambda bi, i: (bi, jnp.maximum(i - off, 0), 0))
    hp_shape = jax.ShapeDtypeStruct((b, HEAD_PAIRS, t, 128), F32)
    tail_shape = jax.ShapeDtypeStruct((b, keep, 512), F32)
    out_specs = [hp_spec, hp_spec, hp_spec, tail_spec, tail_spec]
    out_shape = [hp_shape, hp_shape, hp_shape, tail_shape, tail_shape]
    scratch = []
    if conv:
        small = pl.BlockSpec((1, 512), lambda bi, i: (0, 0))
        in_specs += [pl.BlockSpec((32, 512), lambda bi, i: (0, 0)), small, small, small]
        args += list(conv_p)
        out_specs += [pl.BlockSpec((1, tm, 512), lambda bi, i: (bi, i, 0)),
                      pl.BlockSpec((1, CONV_HALO, 512), lambda bi, i: (bi, 0, 0))]
        out_shape += [jax.ShapeDtypeStruct((b, t, 512), BF16),
                      jax.ShapeDtypeStruct((b, CONV_HALO, 512), F32)]
        scratch = [pltpu.VMEM((8, tm + CONV_HALO, 512), F32)]
    else:
        out_specs += [pl.BlockSpec((1, tm, 512), lambda bi, i: (bi, i, 0))]
        out_shape += [jax.ShapeDtypeStruct((b, t, 512), F32)]
    return pl.pallas_call(
        functools.partial(_inproj_kernel, conv=conv, tm=tm),
        grid=(b, nt), in_specs=in_specs, out_specs=out_specs, out_shape=out_shape,
        scratch_shapes=scratch, compiler_params=_params(2), name=name,
    )(*args)


def _attn_kernel(q_ref, kc_ref, kp_ref, vc_ref, vp_ref, o_ref, acc_s, m_s, l_s, *, tq):
    first_kj = jnp.where(pl.program_id(2) == 0, 128, 0)
    lane_lo = lax.broadcasted_iota(jnp.int32, (128, 128), 1) < HEAD_DIM
    qi = lax.broadcasted_iota(jnp.int32, (256, 256), 0)
    qi = jnp.where(qi >= 128, qi - 128, qi)
    kj = lax.broadcasted_iota(jnp.int32, (256, 256), 1)
    band = (kj >= qi) & (kj <= qi + BAND)
    band_first = band & (kj >= first_kj)

    def block(q2, k2, v2, mask):
        qa = jnp.where(lane_lo, q2, 0.0)
        qb = jnp.where(lane_lo, 0.0, q2)
        qq = jnp.concatenate([qa, qb], axis=0).astype(BF16)
        s = lax.dot_general(qq, k2.astype(BF16), (((1,), (1,)), ((), ())),
                            preferred_element_type=F32)
        s = jnp.where(mask, s, NEG)
        mb = jnp.max(s, axis=1, keepdims=True)
        p = jnp.exp(s - mb)
        lb = jnp.sum(p, axis=1, keepdims=True)
        pv = jnp.dot(p.astype(BF16), v2.astype(BF16), preferred_element_type=F32)
        mt = jnp.where(lane_lo, mb[:128], mb[128:])
        lt = jnp.where(lane_lo, lb[:128], lb[128:])
        pvt = jnp.where(lane_lo, pv[:128], pv[128:])
        return mt, lt, pvt

    def keys_first(kref_p, kref_c, c, d):
        return jnp.concatenate([kref_p[0, 0, pl.ds(tq - BAND * d + c, 128, stride=d), :],
                                kref_c[0, 0, pl.ds(c, 128, stride=d), :]], axis=0)

    def merge(rows, mt, lt, pvt):
        mp, lp, ap = m_s[rows, :], l_s[rows, :], acc_s[rows, :]
        mn = jnp.maximum(mp, mt)
        a = jnp.exp(mp - mn)
        b = jnp.exp(mt - mn)
        return mn, a * lp + b * lt, a * ap + b * pvt

    def merge_store(rows, mt, lt, pvt):
        mn, ln, an = merge(rows, mt, lt, pvt)
        m_s[rows, :] = mn
        l_s[rows, :] = ln
        acc_s[rows, :] = an

    d = 16

    def body16(c, carry):
        rows = pl.ds(c, 128, stride=16)
        mt, lt, pvt = block(q_ref[0, 0, rows, :], keys_first(kp_ref, kc_ref, c, 16),
                            keys_first(vp_ref, vc_ref, c, 16), band_first)
        m_s[rows, :] = mt
        l_s[rows, :] = lt
        acc_s[rows, :] = pvt
        return carry

    lax.fori_loop(0, 16, body16, 0, unroll=2)

    def body4_first(c, carry):
        rows = pl.ds(c, 128, stride=4)
        mt, lt, pvt = block(q_ref[0, 0, rows, :], keys_first(kp_ref, kc_ref, c, 4),
                            keys_first(vp_ref, vc_ref, c, 4), band_first)
        merge_store(rows, mt, lt, pvt)
        return carry

    lax.fori_loop(0, 4, body4_first, 0, unroll=2)

    def body4(n, carry):
        c = n % 4
        s = 1 + n // 4
        rows = pl.ds(c + 512 * s, 128, stride=4)
        krows = pl.ds(c + 512 * (s - 1), 256, stride=4)
        mt, lt, pvt = block(q_ref[0, 0, rows, :], kc_ref[0, 0, krows, :], vc_ref[0, 0, krows, :], band)
        merge_store(rows, mt, lt, pvt)
        return carry

    lax.fori_loop(0, 4 * (tq // 512 - 1), body4, 0, unroll=2)

    def finish(rows, mt, lt, pvt):
        _, ln, an = merge(rows, mt, lt, pvt)
        o_ref[0, 0, rows, :] = (an / ln).astype(o_ref.dtype)

    rows0 = pl.ds(0, 128)
    mt, lt, pvt = block(q_ref[0, 0, rows0, :], keys_first(kp_ref, kc_ref, 0, 1),
                        keys_first(vp_ref, vc_ref, 0, 1), band_first)
    finish(rows0, mt, lt, pvt)

    def body1(s, carry):
        rows = pl.ds(pl.multiple_of(128 * s, 128), 128)
        krows = pl.ds(pl.multiple_of(128 * (s - 1), 128), 256)
        mt, lt, pvt = block(q_ref[0, 0, rows, :], kc_ref[0, 0, krows, :], vc_ref[0, 0, krows, :], band)
        finish(rows, mt, lt, pvt)
        return carry

    lax.fori_loop(1, tq // 128, body1, 0, unroll=3)


def _attn_call(q, k, v, *, tq=2048):
    b, hp, t, _ = q.shape
    assert tq == BAND * 16 and t % tq == 0
    cur = pl.BlockSpec((1, 1, tq, 128), lambda bi, h, i: (bi, h, i, 0))
    prev = pl.BlockSpec((1, 1, tq, 128), lambda bi, h, i: (bi, h, jnp.maximum(i - 1, 0), 0))
    return pl.pallas_call(
        functools.partial(_attn_kernel, tq=tq),
        grid=(b, hp, t // tq),
        in_specs=[cur, cur, prev, cur, prev],
        out_specs=cur,
        out_shape=jax.ShapeDtypeStruct((b, hp, t, 128), BF16),
        scratch_shapes=[pltpu.VMEM((tq, 128), F32)] * 3,
        compiler_params=_params(3), name="dilated_attn_prompt",
    )(q, k, k, v, v)


SAMPLE_DENSE_ROWS = DILATION_PATTERNS[1][0]
SAMPLE_GROUP = DILATION_PATTERNS[2][1]


def _sample_attn_kernel(q_ref, kn_ref, vn_ref, kd_ref, vd_ref, ks_ref, vs_ref,
                        md_ref, ms_ref, mn_ref, bsel_ref, gm_ref, o_ref, *, n_new):
    cols = n_new * N_HEADS
    q32 = q_ref[0].reshape(cols, HEAD_DIM).astype(BF16)
    dn = (((1,), (1,)), ((), ()))
    parts = ([(kd_ref[0, 0], vd_ref[0, 0], md_ref[...], None), (kn_ref[0], vn_ref[0], mn_ref[...], None)]
             + [(ks_ref[0, 0, :, j], vs_ref[0, 0, :, j], ms_ref[j], j) for j in range(n_new)])

    def masked_scores(kv, mult):
        k2 = kv.reshape(kv.shape[0] * N_HEADS, HEAD_DIM).astype(BF16)
        s = lax.dot_general(k2, q32, dn, preferred_element_type=F32)
        return jnp.where(mult > 0, s, NEG)

    scores = [masked_scores(kv, mult) for kv, _, mult, _ in parts]
    m8 = functools.reduce(jnp.maximum, [jnp.max(s.reshape(-1, N_HEADS, cols), axis=0) for s in scores])
    m = jnp.max(m8, axis=0, keepdims=True)
    probs = [part[2] * jnp.exp(s - m) for s, part in zip(scores, parts)]
    psum = functools.reduce(jnp.add, [jnp.sum(p.reshape(-1, N_HEADS, cols), axis=0) for p in probs])
    probs = [p.astype(BF16) for p in probs]
    for t in range(n_new):
        num = jnp.zeros((N_HEADS, HEAD_DIM), F32)
        for p, (_, vv, _, only_query) in zip(probs, parts):
            if only_query is not None and only_query != t:
                continue
            pe = jnp.dot(p, bsel_ref[t], preferred_element_type=F32)
            num = num + jnp.sum(pe.reshape(-1, N_HEADS, HEAD_DIM) * vv, axis=0)
        den = jnp.sum(psum * gm_ref[t], axis=1, keepdims=True)
        o_ref[0, t] = num / den


def _sample_tables(w_buf, n_new):
    grp, dense = SAMPLE_GROUP, SAMPLE_DENSE_ROWS
    start = w_buf - dense
    rows = np.arange(w_buf + n_new)
    mult = np.zeros((w_buf + n_new, n_new), np.float32)
    for t in range(n_new):
        dist = w_buf + t - rows
        for window, dil in DILATION_PATTERNS:
            mult[:, t] += (dist >= 0) & (dist <= window) & (dist % dil == 0)
    early = mult[:start].reshape(start // grp, grp, n_new)
    assert start % dense == 0 and start % grp == 0 and n_new <= grp
    assert not early[:, n_new:].any() and not (early[:, :n_new] * (1 - np.eye(n_new))).any()
    cols = n_new * N_HEADS
    diag = np.tile(np.eye(N_HEADS, dtype=np.float32), (1, n_new))

    def expand(mm):
        return (np.repeat(mm, N_HEADS, axis=1)[:, None, :] * diag[None]).reshape(-1, cols)

    md = expand(mult[start:w_buf])
    ms = np.stack([expand(early[:, j]) for j in range(n_new)])
    mn = expand(mult[w_buf:])
    col_t = np.repeat(np.arange(n_new), N_HEADS)
    bsel = np.stack([np.broadcast_to((col_t == t)[:, None], (cols, HEAD_DIM)) for t in range(n_new)])
    gm = np.stack([np.broadcast_to((col_t == t)[None, :], (N_HEADS, cols)) for t in range(n_new)])
    return (jnp.asarray(md), jnp.asarray(ms), jnp.asarray(mn), jnp.asarray(bsel, dtype=BF16),
            jnp.asarray(gm, dtype=F32))


def _sample_attn_call(q, kn, vn, cache_k, cache_v, layer):
    nb, n_new = q.shape[:2]
    depth, _, w_buf = cache_k.shape[:3]
    grp, dense = SAMPLE_GROUP, SAMPLE_DENSE_ROWS
    start = w_buf - dense
    tables = _sample_tables(w_buf, n_new)
    grouped = lambda c: c.reshape(depth, nb, w_buf // grp, grp, N_HEADS, HEAD_DIM)
    new_spec = pl.BlockSpec((1, n_new, N_HEADS, HEAD_DIM), lambda i: (i, 0, 0, 0))
    dense_spec = pl.BlockSpec((1, 1, dense, N_HEADS, HEAD_DIM), lambda i: (layer, i, start // dense, 0, 0))
    early_spec = pl.BlockSpec((1, 1, start // grp, n_new, N_HEADS, HEAD_DIM), lambda i: (layer, i, 0, 0, 0, 0))
    const = lambda a: pl.BlockSpec(a.shape, lambda i: (0,) * a.ndim)
    return pl.pallas_call(
        functools.partial(_sample_attn_kernel, n_new=n_new),
        grid=(nb,),
        in_specs=[new_spec, new_spec, new_spec, dense_spec, dense_spec, early_spec, early_spec]
                 + [const(a) for a in tables],
        out_specs=new_spec,
        out_shape=jax.ShapeDtypeStruct((nb, n_new, N_HEADS, HEAD_DIM), F32),
        compiler_params=_params(1), name="dilated_attn_sample",
    )(q, kn, vn, cache_k, cache_v, grouped(cache_k), grouped(cache_v), *tables)


def _sample_conv_kernel(uf_ref, cw_ref, cb_ref, lg_ref, lb_ref, o_ref, *, n_new, nb):
    rows = 32

    def body(n, carry):
        t = n // (nb // rows)
        base = pl.multiple_of((n % (nb // rows)) * rows, rows)
        acc = jnp.broadcast_to(cb_ref[...], (rows, 512))
        for j in range(CONV_WIDTH):
            acc = acc + cw_ref[j:j + 1, :] * uf_ref[t + j, pl.ds(base, rows), :]
        mu = jnp.mean(acc, axis=-1, keepdims=True)
        xc = acc - mu
        var = jnp.mean(xc * xc, axis=-1, keepdims=True)
        y = xc * lax.rsqrt(var + EPS) * lg_ref[...] + lb_ref[...]
        o_ref[t, pl.ds(base, rows), :] = _silu(y).astype(BF16)
        return carry

    lax.fori_loop(0, n_new * (nb // rows), body, 0)


def _sample_conv_call(ufull, conv_p):
    n_full, nb, ch = ufull.shape
    n_new = n_full - (CONV_WIDTH - 1)
    return pl.pallas_call(
        functools.partial(_sample_conv_kernel, n_new=n_new, nb=nb),
        out_shape=jax.ShapeDtypeStruct((n_new, nb, ch), BF16),
        compiler_params=pltpu.CompilerParams(vmem_limit_bytes=VMEM_LIMIT),
        name="conv_sample",
    )(ufull, *conv_p)


def _mix_residual(att_ref, cv_ref, x_ref, gt1_ref, sc2_ref, sh2_ref, gpm_ref, gpf_ref, wo_ref):
    a = jnp.concatenate([att_ref[0, hp] for hp in range(HEAD_PAIRS)] + [cv_ref[0]], axis=-1)
    mix = jnp.dot(a, wo_ref[...], preferred_element_type=F32)
    x1 = x_ref[0] + gt1_ref[0] * _rms(mix, gpm_ref[...])
    h2 = _rms(x1, gpf_ref[...]) * (1.0 + sc2_ref[0]) + sh2_ref[0]
    return x1, h2


def _ffn_kernel(att_ref, cv_ref, x_ref, gt1_ref, sc2_ref, sh2_ref, gt2_ref, gpm_ref, gpf_ref, gqf_ref,
                wo_ref, wg_ref, wu_ref, wd_ref, o_ref, x1_s, h2_s, acc_s):
    f = pl.program_id(1)

    @pl.when(f == 0)
    def _():
        x1, h2 = _mix_residual(att_ref, cv_ref, x_ref, gt1_ref, sc2_ref, sh2_ref, gpm_ref, gpf_ref, wo_ref)
        x1_s[...] = x1
        h2_s[...] = h2.astype(BF16)
        acc_s[...] = jnp.zeros_like(acc_s)

    h2 = h2_s[...]
    g = jnp.dot(h2, wg_ref[...].astype(BF16), preferred_element_type=F32)
    u = jnp.dot(h2, wu_ref[...].astype(BF16), preferred_element_type=F32)
    act = (_silu(g) * u).astype(BF16)
    acc_s[...] += jnp.dot(act, wd_ref[...].astype(BF16), preferred_element_type=F32)

    @pl.when(f == pl.num_programs(1) - 1)
    def _():
        o_ref[0] = x1_s[...] + gt2_ref[0] * _rms(acc_s[...], gqf_ref[...])


def _mix_in_specs(b, t, tm, mod_rows, mod_idx, d):
    ntb = t // tm
    row = lambda i, *_: (i // ntb, i % ntb, 0)
    specs = [pl.BlockSpec((1, HEAD_PAIRS, tm, 128), lambda i, *_: (i // ntb, 0, i % ntb, 0)),
             pl.BlockSpec((1, tm, 512), row),
             pl.BlockSpec((1, tm, d), row)]
    return specs, row


def _ffn_call(att, cv, x, mod, mod_rows, mod_idx, gpm, gpf, gqf, wo_bf, wg, wu, wd, *, tm, tf, name):
    b, t, d = x.shape
    ff = wg.shape[1]
    ntb = t // tm
    specs, row = _mix_in_specs(b, t, tm, mod_rows, mod_idx, d)
    vec = pl.BlockSpec((1, d), lambda i, f: (0, 0))
    in_specs = specs + [_mod_spec(mod_rows, 2, mod_idx), _mod_spec(mod_rows, 4, mod_idx),
                        _mod_spec(mod_rows, 3, mod_idx), _mod_spec(mod_rows, 5, mod_idx),
                        vec, vec, vec,
                        pl.BlockSpec((d, d), lambda i, f: (0, 0)),
                        pl.BlockSpec((d, tf), lambda i, f: (0, f)),
                        pl.BlockSpec((d, tf), lambda i, f: (0, f)),
                        pl.BlockSpec((tf, d), lambda i, f: (f, 0))]
    return pl.pallas_call(
        _ffn_kernel,
        grid=(b * ntb, ff // tf),
        in_specs=in_specs,
        out_specs=pl.BlockSpec((1, tm, d), row),
        out_shape=jax.ShapeDtypeStruct((b, t, d), F32),
        scratch_shapes=[pltpu.VMEM((tm, d), F32), pltpu.VMEM((tm, d), BF16), pltpu.VMEM((tm, d), F32)],
        compiler_params=_params(2), name=name,
    )(att, cv, x, mod, mod, mod, mod, gpm, gpf, gqf, wo_bf, wg, wu, wd)


def _router_kernel(att_ref, cv_ref, x_ref, gt1_ref, sc2_ref, sh2_ref, gpm_ref, gpf_ref, wo_ref, wr_ref,
                   x1_ref, h2_ref, gate_ref):
    x1, h2 = _mix_residual(att_ref, cv_ref, x_ref, gt1_ref, sc2_ref, sh2_ref, gpm_ref, gpf_ref, wo_ref)
    x1_ref[0] = x1
    h2_ref[0] = h2.astype(BF16)
    logits = jnp.dot(h2, wr_ref[...], preferred_element_type=F32, precision=lax.Precision.HIGHEST)
    lane = lax.broadcasted_iota(jnp.int32, logits.shape, 1).astype(F32)
    v1 = jnp.max(logits, axis=-1, keepdims=True)
    i1 = jnp.min(jnp.where(logits == v1, lane, float(N_EXPERTS)), axis=-1, keepdims=True)
    oh1 = lane == i1
    rest = jnp.where(oh1, -jnp.inf, logits)
    v2 = jnp.max(rest, axis=-1, keepdims=True)
    i2 = jnp.min(jnp.where(rest == v2, lane, float(N_EXPERTS)), axis=-1, keepdims=True)
    oh2 = lane == i2
    e2 = jnp.exp(v2 - v1)
    den = 1.0 + e2
    gate_ref[0] = jnp.where(oh1, 1.0 / den, 0.0) + jnp.where(oh2, e2 / den, 0.0)


def _router_call(att, cv, x, mod, mod_rows, mod_idx, gpm, gpf, wo_bf, wr, *, tm, name):
    b, t, d = x.shape
    specs, row = _mix_in_specs(b, t, tm, mod_rows, mod_idx, d)
    vec = pl.BlockSpec((1, d), lambda i: (0, 0))
    in_specs = specs + [_mod_spec(mod_rows, 2, mod_idx), _mod_spec(mod_rows, 4, mod_idx),
                        _mod_spec(mod_rows, 3, mod_idx), vec, vec,
                        pl.BlockSpec((d, d), lambda i: (0, 0)),
                        pl.BlockSpec((d, N_EXPERTS), lambda i: (0, 0))]
    return pl.pallas_call(
        _router_kernel,
        grid=(b * (t // tm),),
        in_specs=in_specs,
        out_specs=[pl.BlockSpec((1, tm, d), row), pl.BlockSpec((1, tm, d), row),
                   pl.BlockSpec((1, tm, N_EXPERTS), row)],
        out_shape=[jax.ShapeDtypeStruct((b, t, d), F32), jax.ShapeDtypeStruct((b, t, d), BF16),
                   jax.ShapeDtypeStruct((b, t, N_EXPERTS), F32)],
        compiler_params=_params(1), name=name,
    )(att, cv, x, mod, mod, mod, gpm, gpf, wo_bf, wr)


def _moe_kernel(h2_ref, gate_ref, x1_ref, gt2_ref, gqf_ref, wg_ref, wu_ref, wd_ref, o_ref, acc_s):
    e, f = pl.program_id(1), pl.program_id(2)

    @pl.when((e == 0) & (f == 0))
    def _():
        acc_s[...] = jnp.zeros_like(acc_s)

    h2 = h2_ref[0]
    g = jnp.dot(h2, wg_ref[0].astype(BF16), preferred_element_type=F32)
    u = jnp.dot(h2, wu_ref[0].astype(BF16), preferred_element_type=F32)
    act = (_silu(g) * u).astype(BF16)
    acc_s[...] += gate_ref[0, 0] * jnp.dot(act, wd_ref[0].astype(BF16), preferred_element_type=F32)

    @pl.when((e == pl.num_programs(1) - 1) & (f == pl.num_programs(2) - 1))
    def _():
        o_ref[0] = x1_ref[0] + gt2_ref[0] * _rms(acc_s[...], gqf_ref[...])


def _moe_call(h2, gates_t, x1, mod, mod_rows, mod_idx, gqf, wg, wu, wd, *, tm, tf, name):
    b, t, d = x1.shape
    n_e, _, ff = wg.shape
    ntb = t // tm
    row = lambda i, e, f: (i // ntb, i % ntb, 0)
    in_specs = [pl.BlockSpec((1, tm, d), row),
                pl.BlockSpec((1, 1, tm, 1), lambda i, e, f: (i // ntb, e, i % ntb, 0)),
                pl.BlockSpec((1, tm, d), row),
                _mod_spec(mod_rows, 5, mod_idx),
                pl.BlockSpec((1, d), lambda i, e, f: (0, 0)),
                pl.BlockSpec((1, d, tf), lambda i, e, f: (e, 0, f)),
                pl.BlockSpec((1, d, tf), lambda i, e, f: (e, 0, f)),
                pl.BlockSpec((1, tf, d), lambda i, e, f: (e, f, 0))]
    return pl.pallas_call(
        _moe_kernel,
        grid=(b * ntb, n_e, ff // tf),
        in_specs=in_specs,
        out_specs=pl.BlockSpec((1, tm, d), row),
        out_shape=jax.ShapeDtypeStruct((b, t, d), F32),
        scratch_shapes=[pltpu.VMEM((tm, d), F32)],
        compiler_params=_params(3), name=name,
    )(h2, gates_t, x1, mod, gqf, wg, wu, wd)


def _rope_tables(pos):
    half = ROT_DIM // 2
    inv_freq = ROPE_THETA ** (-jnp.arange(half, dtype=F32) * 2.0 / ROT_DIM)
    ang = pos.astype(F32)[:, None] * inv_freq[None, :]
    cos, sin = jnp.cos(ang), jnp.sin(ang)
    l64 = np.arange(128) % HEAD_DIM
    idx = l64 % half
    first = (l64 < half)[None, :]
    second = ((l64 >= half) & (l64 < ROT_DIM))[None, :]
    cos_t = jnp.where(first | second, cos[:, idx], 1.0)
    s1_t = jnp.where(first, -sin[:, idx], 0.0)
    s2_t = jnp.where(second, sin[:, idx], 0.0)
    return cos_t, s1_t, s2_t


def kernel(x_prompt, x_sample, cache_k, cache_v, state_conv, c_prompt, c_sample, w_mod, b_mod, g_pre_mix, g_post_mix, g_pre_ffn, g_post_ffn, w_in, conv_w, conv_b, conv_ln_g, conv_ln_b, w_out, ffn_w_gate, ffn_w_up, ffn_w_down, moe_w_router, moe_w_gate, moe_w_up, moe_w_down):
    bp, seq, d = x_prompt.shape
    nb, n_new, _ = x_sample.shape
    depth = w_mod.shape[0]
    w_buf = cache_k.shape[2]
    past_len = PAST_LEN
    assert w_buf == min(DILATION_PATTERNS[-1][0], past_len)
    keep = min(DILATION_PATTERNS[-1][0], seq)
    ns = nb * n_new

    c_all = jnp.concatenate([c_sample, c_prompt, jnp.zeros((MOD_ROWS - nb - bp, d), F32)], axis=0)
    mod_all = _mod_call(c_all, w_mod, b_mod)

    tm_p = 512
    tabs_p = tuple(tb.reshape(seq // tm_p, tm_p, 128) for tb in _rope_tables(jnp.arange(seq, dtype=jnp.int32)))
    tabs_s = tuple(tb.reshape(n_new, 1, 128)
                   for tb in _rope_tables(past_len + jnp.arange(n_new, dtype=jnp.int32)))

    yp = x_prompt
    ys = x_sample.transpose(1, 0, 2).reshape(1, ns, d)
    outs = [[] for _ in range(6)]
    for l in range(depth):
        mod_p = mod_all[l].reshape(MOD_ROWS, 1, 6 * d)
        mod_s = jnp.tile(mod_all[l, :nb], (n_new, 1))[None]
        idx_p2 = lambda bi, i: (nb + bi, 0)
        idx_s2 = lambda bi, i: (0, i)
        w_in_bf = w_in[l].astype(BF16)
        w_out_bf = w_out[l].astype(BF16)
        conv_p = (jnp.pad(conv_w[l], ((0, 1), (0, 0))), conv_b[l][None], conv_ln_g[l][None], conv_ln_b[l][None])
        gpre, gpm, gpf, gqf = g_pre_mix[l][None], g_post_mix[l][None], g_pre_ffn[l][None], g_post_ffn[l][None]

        q, k, v, kt, vt, cv, ul = _inproj_call(yp, mod_p, 1, idx_p2, gpre, w_in_bf, tabs_p, tm_p, conv_p,
                                               tm=tm_p, keep=keep, name=f"inproj_prompt_{l}")
        att = _attn_call(q, k, v)
        outs[0].append(kt.reshape(bp, keep, N_HEADS, HEAD_DIM))
        outs[1].append(vt.reshape(bp, keep, N_HEADS, HEAD_DIM))
        outs[2].append(ul[:, CONV_HALO - (CONV_WIDTH - 1):])

        qs, _, _, kts, vts, us = _inproj_call(ys, mod_s, nb, idx_s2, gpre, w_in_bf, tabs_s, 1, None,
                                              tm=nb, keep=ns, name=f"inproj_sample_{l}")
        to_batch_major = lambda z: z.reshape(n_new, nb, N_HEADS, HEAD_DIM).transpose(1, 0, 2, 3)
        q_bm = (qs.reshape(HEAD_PAIRS, n_new, nb, 2, HEAD_DIM).transpose(2, 1, 0, 3, 4)
                .reshape(nb, n_new, N_HEADS, HEAD_DIM))
        k_bm, v_bm = to_batch_major(kts), to_batch_major(vts)
        att_s = _sample_attn_call(q_bm, k_bm, v_bm, cache_k, cache_v, l)
        att_s = (att_s.reshape(nb, n_new, HEAD_PAIRS, 128).transpose(2, 1, 0, 3)
                 .reshape(1, HEAD_PAIRS, ns, 128).astype(BF16))
        ufull = jnp.concatenate([state_conv[l].transpose(1, 0, 2), us.reshape(n_new, nb, 512)], axis=0)
        cv_s = _sample_conv_call(ufull, conv_p).reshape(1, ns, 512)
        outs[3].append(k_bm)
        outs[4].append(v_bm)
        outs[5].append(ufull[n_new:].transpose(1, 0, 2))

        idx_p1 = lambda i, *_: (nb + i // (seq // tm_f), 0)
        idx_s1 = lambda i, *_: (0, i)
        if l % 2 == 0:
            tm_f = 1024
            wg, wu, wd = ffn_w_gate[l // 2], ffn_w_up[l // 2], ffn_w_down[l // 2]
            yp = _ffn_call(att, cv, yp, mod_p, 1, idx_p1, gpm, gpf, gqf, w_out_bf, wg, wu, wd,
                           tm=tm_f, tf=256, name=f"ffn_prompt_{l}")
            ys = _ffn_call(att_s, cv_s, ys, mod_s, ns, idx_s1, gpm, gpf, gqf, w_out_bf, wg, wu, wd,
                           tm=ns, tf=512, name=f"ffn_sample_{l}")
        else:
            tm_f = 1024
            wr = moe_w_router[l // 2]
            wg, wu, wd = moe_w_gate[l // 2], moe_w_up[l // 2], moe_w_down[l // 2]
            x1, h2, gates = _router_call(att, cv, yp, mod_p, 1, idx_p1, gpm, gpf, w_out_bf, wr,
                                         tm=tm_f, name=f"router_prompt_{l}")
            yp = _moe_call(h2, gates.transpose(0, 2, 1)[..., None], x1, mod_p, 1, idx_p1, gqf, wg, wu, wd,
                           tm=tm_f, tf=512, name=f"moe_prompt_{l}")
            x1s, h2s, gates_s = _router_call(att_s, cv_s, ys, mod_s, ns, idx_s1, gpm, gpf, w_out_bf, wr,
                                             tm=ns, name=f"router_sample_{l}")
            ys = _moe_call(h2s, gates_s.transpose(0, 2, 1)[..., None], x1s, mod_s, ns, idx_s1, gqf, wg, wu, wd,
                           tm=ns, tf=512, name=f"moe_sample_{l}")

    y_sample = ys.reshape(n_new, nb, d).transpose(1, 0, 2)
    return (yp, y_sample) + tuple(jnp.stack(o) for o in outs)
```

```python
import functools

import numpy as np
import jax
import jax.numpy as jnp
from jax import lax
from jax.experimental import pallas as pl
from jax.experimental.pallas import tpu as pltpu

F32 = jnp.float32
BF16 = jnp.bfloat16

HEAD_DIM = 64
N_HEADS = 8
ATT_WIDTH = N_HEADS * HEAD_DIM
HEAD_PAIRS = ATT_WIDTH // 128
CONV_WIDTH = 31
CONV_HALO = 32
DILATION_PATTERNS = ((128, 1), (512, 4), (2048, 16))
BAND = 128
ROT_DIM = HEAD_DIM // 4
ROPE_THETA = 500000.0
ATTN_SCALE = HEAD_DIM ** -0.5
N_EXPERTS = 8
PAST_LEN = 2048
EPS = 1e-6
NEG = -1e30
MOD_ROWS = 136
VMEM_LIMIT = 56 * 1024 * 1024


def _rms(x, g):
    return x * lax.rsqrt(jnp.mean(x * x, axis=-1, keepdims=True) + EPS) * g


def _silu(x):
    return x * jax.nn.sigmoid(x)


def _params(n_axes, vmem=VMEM_LIMIT):
    return pltpu.CompilerParams(dimension_semantics=("arbitrary",) * n_axes, vmem_limit_bytes=vmem)


def _mod_kernel(c_ref, w_ref, b_ref, o_ref):
    a = _silu(c_ref[...]).astype(BF16)
    o_ref[0] = jnp.dot(a, w_ref[0].astype(BF16), preferred_element_type=F32) + b_ref[0]


def _mod_call(c_all, w_mod, b_mod):
    depth, d, d6 = w_mod.shape
    return pl.pallas_call(
        _mod_kernel,
        grid=(depth, d6 // d),
        in_specs=[pl.BlockSpec((MOD_ROWS, d), lambda l, j: (0, 0)),
                  pl.BlockSpec((1, d, d), lambda l, j: (l, 0, j)),
                  pl.BlockSpec((1, 1, d), lambda l, j: (l, 0, j))],
        out_specs=pl.BlockSpec((1, MOD_ROWS, d), lambda l, j: (l, 0, j)),
        out_shape=jax.ShapeDtypeStruct((depth, MOD_ROWS, d6), F32),
        compiler_params=_params(2),
        name="adaln_mod",
    )(c_all, w_mod, b_mod.reshape(depth, 1, d6))


def _inproj_kernel(*refs, conv, tm):
    if conv:
        (x_ref, sh_ref, sc_ref, g_ref, w_ref, cos_ref, s1_ref, s2_ref,
         cw_ref, cb_ref, lg_ref, lb_ref,
         q_ref, k_ref, v_ref, kt_ref, vt_ref, cv_ref, ul_ref, ubuf) = refs
    else:
        (x_ref, sh_ref, sc_ref, g_ref, w_ref, cos_ref, s1_ref, s2_ref,
         q_ref, k_ref, v_ref, kt_ref, vt_ref, u_ref) = refs

    h = _rms(x_ref[0], g_ref[...]) * (1.0 + sc_ref[0]) + sh_ref[0]
    proj = jnp.dot(h.astype(BF16), w_ref[...], preferred_element_type=F32)
    cos, s1, s2 = cos_ref[0], s1_ref[0], s2_ref[0]

    def rope(z):
        return z * cos + pltpu.roll(z, 128 - ROT_DIM // 2, 1) * s1 + pltpu.roll(z, ROT_DIM // 2, 1) * s2

    for hp in range(HEAD_PAIRS):
        lo, hi = hp * 128, (hp + 1) * 128
        q_ref[0, hp] = rope(proj[:, lo:hi]) * ATTN_SCALE
        kz = rope(proj[:, ATT_WIDTH + lo:ATT_WIDTH + hi])
        k_ref[0, hp] = kz
        kt_ref[0, :, lo:hi] = kz
        vz = proj[:, 2 * ATT_WIDTH + lo:2 * ATT_WIDTH + hi]
        v_ref[0, hp] = vz
        vt_ref[0, :, lo:hi] = vz

    a = proj[:, 3 * ATT_WIDTH:3 * ATT_WIDTH + 512]
    gate = proj[:, 3 * ATT_WIDTH + 512:]
    u = a * jax.nn.sigmoid(gate)
    if not conv:
        u_ref[0] = u
        return

    @pl.when(pl.program_id(1) == 0)
    def _():
        ubuf[0, 0:CONV_HALO, :] = jnp.zeros((CONV_HALO, 512), F32)

    ubuf[0, CONV_HALO:CONV_HALO + tm, :] = u
    span = tm + CONV_HALO - 8
    for s in range(1, 8):
        ubuf[s, 0:span, :] = ubuf[0, s:s + span, :]
    off0 = CONV_HALO - (CONV_WIDTH - 1)
    rows = 32

    def chunk(r, carry):
        base = pl.multiple_of(r * rows, rows)
        acc = jnp.broadcast_to(cb_ref[...], (rows, 512))
        for j in range(CONV_WIDTH):
            a, s = divmod(off0 + j, 8)
            acc = acc + cw_ref[j:j + 1, :] * ubuf[s, pl.ds(base + 8 * a, rows), :]
        mu = jnp.mean(acc, axis=-1, keepdims=True)
        xc = acc - mu
        var = jnp.mean(xc * xc, axis=-1, keepdims=True)
        y = xc * lax.rsqrt(var + EPS) * lg_ref[...] + lb_ref[...]
        cv_ref[0, pl.ds(base, rows), :] = _silu(y).astype(BF16)
        return carry

    lax.fori_loop(0, tm // rows, chunk, 0)
    tail = ubuf[0, tm:tm + CONV_HALO, :]
    ul_ref[0] = tail
    ubuf[0, 0:CONV_HALO, :] = tail


def _mod_spec(rows, chunk, index_fn):
    return pl.BlockSpec((1, rows, 1024), lambda *g: index_fn(*g) + (chunk,))


def _inproj_call(x, mod, mod_rows, mod_idx, g, w_bf, tabs, tab_rows, conv_p, *, tm, keep, name):
    b, t, d = x.shape
    nt = t // tm
    off = (t - keep) // tm
    conv = conv_p is not None
    tab_spec = pl.BlockSpec((1, tab_rows, 128), lambda bi, i: (i, 0, 0))
    in_specs = [pl.BlockSpec((1, tm, d), lambda bi, i: (bi, i, 0)),
                _mod_spec(mod_rows, 0, mod_idx), _mod_spec(mod_rows, 1, mod_idx),
                pl.BlockSpec((1, d), lambda bi, i: (0, 0)),
                pl.BlockSpec(w_bf.shape, lambda bi, i: (0, 0)),
                tab_spec, tab_spec, tab_spec]
    args = [x, mod, mod, g, w_bf, *tabs]
    hp_spec = pl.BlockSpec((1, HEAD_PAIRS, tm, 128), lambda bi, i: (bi, 0, i, 0))
    tail_spec = pl.BlockSpec((1, tm, 512), lambda bi, i: (bi, jnp.maximum(i - off, 0), 0))
    hp_shape = jax.ShapeDtypeStruct((b, HEAD_PAIRS, t, 128), F32)
    tail_shape = jax.ShapeDtypeStruct((b, keep, 512), F32)
    out_specs = [hp_spec, hp_spec, hp_spec, tail_spec, tail_spec]
    out_shape = [hp_shape, hp_shape, hp_shape, tail_shape, tail_shape]
    scratch = []
    if conv:
        small = pl.BlockSpec((1, 512), lambda bi, i: (0, 0))
        in_specs += [pl.BlockSpec((32, 512), lambda bi, i: (0, 0)), small, small, small]
        args += list(conv_p)
        out_specs += [pl.BlockSpec((1, tm, 512), lambda bi, i: (bi, i, 0)),
                      pl.BlockSpec((1, CONV_HALO, 512), lambda bi, i: (bi, 0, 0))]
        out_shape += [jax.ShapeDtypeStruct((b, t, 512), BF16),
                      jax.ShapeDtypeStruct((b, CONV_HALO, 512), F32)]
        scratch = [pltpu.VMEM((8, tm + CONV_HALO, 512), F32)]
    else:
        out_specs += [pl.BlockSpec((1, tm, 512), lambda bi, i: (bi, i, 0))]
        out_shape += [jax.ShapeDtypeStruct((b, t, 512), F32)]
    return pl.pallas_call(
        functools.partial(_inproj_kernel, conv=conv, tm=tm),
        grid=(b, nt), in_specs=in_specs, out_specs=out_specs, out_shape=out_shape,
        scratch_shapes=scratch, compiler_params=_params(2), name=name,
    )(*args)


def _attn_kernel(q_ref, kc_ref, kp_ref, vc_ref, vp_ref, o_ref, acc_s, m_s, l_s, *, tq):
    first_kj = jnp.where(pl.program_id(2) == 0, 128, 0)
    lane_lo = lax.broadcasted_iota(jnp.int32, (128, 128), 1) < HEAD_DIM
    qi = lax.broadcasted_iota(jnp.int32, (256, 256), 0)
    qi = jnp.where(qi >= 128, qi - 128, qi)
    kj = lax.broadcasted_iota(jnp.int32, (256, 256), 1)
    band = (kj >= qi) & (kj <= qi + BAND)
    band_first = band & (kj >= first_kj)

    ones_cols = jnp.ones((256, 128), BF16)
    nt = (((1,), (1,)), ((), ()))

    def strided(start, size, d):
        if d > 1:
            return pl.ds(start, size, stride=d)
        return pl.ds(start if isinstance(start, int) else pl.multiple_of(start, 128), size)

    def blocks(items):
        loaded = []
        for rows, first, c, d, s in items:
            if first:
                take = lambda rp, rc: jnp.concatenate(
                    [rp[0, 0, strided(tq - BAND * d + c, 128, d), :],
                     rc[0, 0, strided(c, 128, d), :]], axis=0)
            else:
                krows = strided(c + d * 128 * (s - 1), 256, d)
                take = lambda rp, rc: rc[0, 0, krows, :]
            loaded.append((q_ref[0, 0, rows, :], take(kp_ref, kc_ref), take(vp_ref, vc_ref),
                           band_first if first else band))
        scores = []
        for q2, k2, _, _ in loaded:
            qq = jnp.concatenate([jnp.where(lane_lo, q2, 0.0), jnp.where(lane_lo, 0.0, q2)], axis=0)
            scores.append(lax.dot_general(qq.astype(BF16), k2.astype(BF16), nt,
                                          preferred_element_type=F32))
        probs = []
        for sc, (_, _, _, mask) in zip(scores, loaded):
            sc = jnp.where(mask, sc, NEG)
            mb = jnp.max(sc, axis=1, keepdims=True)
            probs.append((mb, jnp.exp(sc - mb).astype(BF16)))
        out = []
        for (mb, p), (_, _, v2, _), item in zip(probs, loaded, items):
            pv = jnp.dot(p, jnp.concatenate([v2.astype(BF16), ones_cols], axis=1), preferred_element_type=F32)
            out.append((item[0], jnp.where(lane_lo, mb[:128], mb[128:]),
                        jnp.where(lane_lo, pv[:128, 128:], pv[128:, 128:]),
                        jnp.where(lane_lo, pv[:128, :128], pv[128:, :128])))
        return out

    def merge(rows, mt, lt, pvt):
        mp, lp, ap = m_s[rows, :], l_s[rows, :], acc_s[rows, :]
        mn = jnp.maximum(mp, mt)
        a = jnp.exp(mp - mn)
        b = jnp.exp(mt - mn)
        return mn, a * lp + b * lt, a * ap + b * pvt

    def store_init(results):
        for rows, mt, lt, pvt in results:
            m_s[rows, :] = mt
            l_s[rows, :] = lt
            acc_s[rows, :] = pvt

    def store_merged(results):
        for rows, mt, lt, pvt in results:
            mn, ln, an = merge(rows, mt, lt, pvt)
            m_s[rows, :] = mn
            l_s[rows, :] = ln
            acc_s[rows, :] = an

    def store_output(results):
        for rows, mt, lt, pvt in results:
            _, ln, an = merge(rows, mt, lt, pvt)
            o_ref[0, 0, rows, :] = (an / ln).astype(o_ref.dtype)

    def loop(n, fn):
        def body(i, carry):
            fn(i)
            return carry
        lax.fori_loop(0, n, body, 0)

    group = 4
    loop(16 // group, lambda g: store_init(blocks(
        [(pl.ds(g * group + j, 128, stride=16), True, g * group + j, 16, 0) for j in range(group)])))

    store_merged(blocks([(pl.ds(c, 128, stride=4), True, c, 4, 0) for c in range(4)]))
    loop(tq // 512 - 1, lambda n: store_merged(blocks(
        [(pl.ds(c + 512 * (n + 1), 128, stride=4), False, c, 4, n + 1) for c in range(4)])))

    store_output(blocks([(strided(0, 128, 1), True, 0, 1, 0)]
                        + [(strided(128 * s, 128, 1), False, 0, 1, s) for s in range(1, group)]))
    rest = 3
    assert (tq // 128 - group) % rest == 0
    loop((tq // 128 - group) // rest, lambda g: store_output(blocks(
        [(strided(128 * (group + g * rest + j), 128, 1), False, 0, 1, group + g * rest + j) for j in range(rest)])))


def _attn_call(q, k, v, *, tq=2048):
    b, hp, t, _ = q.shape
    assert tq == BAND * 16 and t % tq == 0
    cur = pl.BlockSpec((1, 1, tq, 128), lambda bi, h, i: (bi, h, i, 0))
    prev = pl.BlockSpec((1, 1, tq, 128), lambda bi, h, i: (bi, h, jnp.maximum(i - 1, 0), 0))
    return pl.pallas_call(
        functools.partial(_attn_kernel, tq=tq),
        grid=(b, hp, t // tq),
        in_specs=[cur, cur, prev, cur, prev],
        out_specs=cur,
        out_shape=jax.ShapeDtypeStruct((b, hp, t, 128), BF16),
        scratch_shapes=[pltpu.VMEM((tq, 128), F32)] * 3,
        compiler_params=_params(3), name="dilated_attn_prompt",
    )(q, k, k, v, v)


SAMPLE_ROWS = 8


def _sample_attn_kernel(q_ref, kn_ref, vn_ref, kt_ref, vt_ref, mh_ref, mn_ref, o_ref):
    mh, mn = mh_ref[...], mn_ref[...]
    nt = (((1,), (1,)), ((), ()))
    for h in range(N_HEADS):
        qh = q_ref[0, h].astype(BF16)
        s = jnp.dot(qh, kt_ref[0, 0, h].astype(BF16), preferred_element_type=F32)
        sn = lax.dot_general(qh, kn_ref[0, h].astype(BF16), nt, preferred_element_type=F32)
        s = jnp.where(mh > 0, s, NEG)
        sn = jnp.where(mn > 0, sn, NEG)
        m = jnp.maximum(jnp.max(s, axis=1, keepdims=True), jnp.max(sn, axis=1, keepdims=True))
        p = mh * jnp.exp(s - m)
        pn = mn * jnp.exp(sn - m)
        den = jnp.sum(p, axis=1, keepdims=True) + jnp.sum(pn, axis=1, keepdims=True)
        num = (lax.dot_general(p.astype(BF16), vt_ref[0, 0, h].astype(BF16), nt, preferred_element_type=F32)
               + jnp.dot(pn.astype(BF16), vn_ref[0, h].astype(BF16), preferred_element_type=F32))
        o_ref[0, h] = num / jnp.where(den > 0, den, 1.0)


def _sample_tables(w_buf, n_new):
    rows = np.arange(w_buf + n_new)
    mult = np.zeros((SAMPLE_ROWS, w_buf + SAMPLE_ROWS), np.float32)
    for t in range(n_new):
        dist = w_buf + t - rows
        for window, dil in DILATION_PATTERNS:
            mult[t, :w_buf + n_new] += (dist >= 0) & (dist <= window) & (dist % dil == 0)
    return jnp.asarray(mult[:, :w_buf]), jnp.asarray(mult[:, w_buf:])


def _sample_attn_call(q, kn, vn, cache_kt, cache_vt, layer, n_new):
    nb = q.shape[0]
    w_buf = cache_kt.shape[-1]
    mh, mn = _sample_tables(w_buf, n_new)
    new_spec = pl.BlockSpec((1, N_HEADS, SAMPLE_ROWS, HEAD_DIM), lambda i: (i, 0, 0, 0))
    cache_spec = pl.BlockSpec((1, 1, N_HEADS, HEAD_DIM, w_buf), lambda i: (layer, i, 0, 0, 0))
    return pl.pallas_call(
        _sample_attn_kernel,
        grid=(nb,),
        in_specs=[new_spec, new_spec, new_spec, cache_spec, cache_spec,
                  pl.BlockSpec(mh.shape, lambda i: (0, 0)), pl.BlockSpec(mn.shape, lambda i: (0, 0))],
        out_specs=new_spec,
        out_shape=jax.ShapeDtypeStruct((nb, N_HEADS, SAMPLE_ROWS, HEAD_DIM), F32),
        compiler_params=_params(1), name="dilated_attn_sample",
    )(q, kn, vn, cache_kt, cache_vt, mh, mn)


def _sample_conv_kernel(uf_ref, cw_ref, cb_ref, lg_ref, lb_ref, o_ref, *, n_new, nb):
    rows = 32

    def body(n, carry):
        t = n // (nb // rows)
        base = pl.multiple_of((n % (nb // rows)) * rows, rows)
        acc = jnp.broadcast_to(cb_ref[...], (rows, 512))
        for j in range(CONV_WIDTH):
            acc = acc + cw_ref[j:j + 1, :] * uf_ref[t + j, pl.ds(base, rows), :]
        mu = jnp.mean(acc, axis=-1, keepdims=True)
        xc = acc - mu
        var = jnp.mean(xc * xc, axis=-1, keepdims=True)
        y = xc * lax.rsqrt(var + EPS) * lg_ref[...] + lb_ref[...]
        o_ref[t, pl.ds(base, rows), :] = _silu(y).astype(BF16)
        return carry

    lax.fori_loop(0, n_new * (nb // rows), body, 0)


def _sample_conv_call(ufull, conv_p):
    n_full, nb, ch = ufull.shape
    n_new = n_full - (CONV_WIDTH - 1)
    return pl.pallas_call(
        functools.partial(_sample_conv_kernel, n_new=n_new, nb=nb),
        out_shape=jax.ShapeDtypeStruct((n_new, nb, ch), BF16),
        compiler_params=pltpu.CompilerParams(vmem_limit_bytes=VMEM_LIMIT),
        name="conv_sample",
    )(ufull, *conv_p)


def _mix_residual(att_ref, cv_ref, x_ref, gt1_ref, sc2_ref, sh2_ref, gpm_ref, gpf_ref, wo_ref):
    a = jnp.concatenate([att_ref[0, hp] for hp in range(HEAD_PAIRS)] + [cv_ref[0]], axis=-1)
    mix = jnp.dot(a, wo_ref[...], preferred_element_type=F32)
    x1 = x_ref[0] + gt1_ref[0] * _rms(mix, gpm_ref[...])
    h2 = _rms(x1, gpf_ref[...]) * (1.0 + sc2_ref[0]) + sh2_ref[0]
    return x1, h2


def _ffn_kernel(att_ref, cv_ref, x_ref, gt1_ref, sc2_ref, sh2_ref, gt2_ref, gpm_ref, gpf_ref, gqf_ref,
                wo_ref, wg_ref, wu_ref, wd_ref, o_ref, x1_s, h2_s, acc_s):
    f = pl.program_id(1)

    @pl.when(f == 0)
    def _():
        x1, h2 = _mix_residual(att_ref, cv_ref, x_ref, gt1_ref, sc2_ref, sh2_ref, gpm_ref, gpf_ref, wo_ref)
        x1_s[...] = x1
        h2_s[...] = h2.astype(BF16)
        acc_s[...] = jnp.zeros_like(acc_s)

    h2 = h2_s[...]
    g = jnp.dot(h2, wg_ref[...].astype(BF16), preferred_element_type=F32)
    u = jnp.dot(h2, wu_ref[...].astype(BF16), preferred_element_type=F32)
    act = (_silu(g) * u).astype(BF16)
    acc_s[...] += jnp.dot(act, wd_ref[...].astype(BF16), preferred_element_type=F32)

    @pl.when(f == pl.num_programs(1) - 1)
    def _():
        o_ref[0] = x1_s[...] + gt2_ref[0] * _rms(acc_s[...], gqf_ref[...])


def _mix_in_specs(b, t, tm, mod_rows, mod_idx, d):
    ntb = t // tm
    row = lambda i, *_: (i // ntb, i % ntb, 0)
    specs = [pl.BlockSpec((1, HEAD_PAIRS, tm, 128), lambda i, *_: (i // ntb, 0, i % ntb, 0)),
             pl.BlockSpec((1, tm, 512), row),
             pl.BlockSpec((1, tm, d), row)]
    return specs, row


def _ffn_call(att, cv, x, mod, mod_rows, mod_idx, gpm, gpf, gqf, wo_bf, wg, wu, wd, *, tm, tf, name):
    b, t, d = x.shape
    ff = wg.shape[1]
    ntb = t // tm
    specs, row = _mix_in_specs(b, t, tm, mod_rows, mod_idx, d)
    vec = pl.BlockSpec((1, d), lambda i, f: (0, 0))
    in_specs = specs + [_mod_spec(mod_rows, 2, mod_idx), _mod_spec(mod_rows, 4, mod_idx),
                        _mod_spec(mod_rows, 3, mod_idx), _mod_spec(mod_rows, 5, mod_idx),
                        vec, vec, vec,
                        pl.BlockSpec((d, d), lambda i, f: (0, 0)),
                        pl.BlockSpec((d, tf), lambda i, f: (0, f)),
                        pl.BlockSpec((d, tf), lambda i, f: (0, f)),
                        pl.BlockSpec((tf, d), lambda i, f: (f, 0))]
    return pl.pallas_call(
        _ffn_kernel,
        grid=(b * ntb, ff // tf),
        in_specs=in_specs,
        out_specs=pl.BlockSpec((1, tm, d), row),
        out_shape=jax.ShapeDtypeStruct((b, t, d), F32),
        scratch_shapes=[pltpu.VMEM((tm, d), F32), pltpu.VMEM((tm, d), BF16), pltpu.VMEM((tm, d), F32)],
        compiler_params=_params(2), name=name,
    )(att, cv, x, mod, mod, mod, mod, gpm, gpf, gqf, wo_bf, wg, wu, wd)


def _router_kernel(att_ref, cv_ref, x_ref, gt1_ref, sc2_ref, sh2_ref, gpm_ref, gpf_ref, wo_ref, wr_ref,
                   x1_ref, h2_ref, gate_ref):
    x1, h2 = _mix_residual(att_ref, cv_ref, x_ref, gt1_ref, sc2_ref, sh2_ref, gpm_ref, gpf_ref, wo_ref)
    x1_ref[0] = x1
    h2_ref[0] = h2.astype(BF16)
    logits = jnp.dot(h2, wr_ref[...], preferred_element_type=F32, precision=lax.Precision.HIGHEST)
    lane = lax.broadcasted_iota(jnp.int32, logits.shape, 1).astype(F32)
    v1 = jnp.max(logits, axis=-1, keepdims=True)
    i1 = jnp.min(jnp.where(logits == v1, lane, float(N_EXPERTS)), axis=-1, keepdims=True)
    oh1 = lane == i1
    rest = jnp.where(oh1, -jnp.inf, logits)
    v2 = jnp.max(rest, axis=-1, keepdims=True)
    i2 = jnp.min(jnp.where(rest == v2, lane, float(N_EXPERTS)), axis=-1, keepdims=True)
    oh2 = lane == i2
    e2 = jnp.exp(v2 - v1)
    den = 1.0 + e2
    gate_ref[0] = jnp.where(oh1, 1.0 / den, 0.0) + jnp.where(oh2, e2 / den, 0.0)


def _router_call(att, cv, x, mod, mod_rows, mod_idx, gpm, gpf, wo_bf, wr, *, tm, name):
    b, t, d = x.shape
    specs, row = _mix_in_specs(b, t, tm, mod_rows, mod_idx, d)
    vec = pl.BlockSpec((1, d), lambda i: (0, 0))
    in_specs = specs + [_mod_spec(mod_rows, 2, mod_idx), _mod_spec(mod_rows, 4, mod_idx),
                        _mod_spec(mod_rows, 3, mod_idx), vec, vec,
                        pl.BlockSpec((d, d), lambda i: (0, 0)),
                        pl.BlockSpec((d, N_EXPERTS), lambda i: (0, 0))]
    return pl.pallas_call(
        _router_kernel,
        grid=(b * (t // tm),),
        in_specs=in_specs,
        out_specs=[pl.BlockSpec((1, tm, d), row), pl.BlockSpec((1, tm, d), row),
                   pl.BlockSpec((1, tm, N_EXPERTS), row)],
        out_shape=[jax.ShapeDtypeStruct((b, t, d), F32), jax.ShapeDtypeStruct((b, t, d), BF16),
                   jax.ShapeDtypeStruct((b, t, N_EXPERTS), F32)],
        compiler_params=_params(1), name=name,
    )(att, cv, x, mod, mod, mod, gpm, gpf, wo_bf, wr)


def _moe_kernel(h2_ref, gate_ref, x1_ref, gt2_ref, gqf_ref, wg_ref, wu_ref, wd_ref, o_ref, acc_s):
    e, f = pl.program_id(1), pl.program_id(2)

    @pl.when((e == 0) & (f == 0))
    def _():
        acc_s[...] = jnp.zeros_like(acc_s)

    h2 = h2_ref[0]
    g = jnp.dot(h2, wg_ref[0].astype(BF16), preferred_element_type=F32)
    u = jnp.dot(h2, wu_ref[0].astype(BF16), preferred_element_type=F32)
    act = (_silu(g) * u).astype(BF16)
    acc_s[...] += gate_ref[0, 0] * jnp.dot(act, wd_ref[0].astype(BF16), preferred_element_type=F32)

    @pl.when((e == pl.num_programs(1) - 1) & (f == pl.num_programs(2) - 1))
    def _():
        o_ref[0] = x1_ref[0] + gt2_ref[0] * _rms(acc_s[...], gqf_ref[...])


def _moe_call(h2, gates_t, x1, mod, mod_rows, mod_idx, gqf, wg, wu, wd, *, tm, tf, name):
    b, t, d = x1.shape
    n_e, _, ff = wg.shape
    ntb = t // tm
    row = lambda i, e, f: (i // ntb, i % ntb, 0)
    in_specs = [pl.BlockSpec((1, tm, d), row),
                pl.BlockSpec((1, 1, tm, 1), lambda i, e, f: (i // ntb, e, i % ntb, 0)),
                pl.BlockSpec((1, tm, d), row),
                _mod_spec(mod_rows, 5, mod_idx),
                pl.BlockSpec((1, d), lambda i, e, f: (0, 0)),
                pl.BlockSpec((1, d, tf), lambda i, e, f: (e, 0, f)),
                pl.BlockSpec((1, d, tf), lambda i, e, f: (e, 0, f)),
                pl.BlockSpec((1, tf, d), lambda i, e, f: (e, f, 0))]
    return pl.pallas_call(
        _moe_kernel,
        grid=(b * ntb, n_e, ff // tf),
        in_specs=in_specs,
        out_specs=pl.BlockSpec((1, tm, d), row),
        out_shape=jax.ShapeDtypeStruct((b, t, d), F32),
        scratch_shapes=[pltpu.VMEM((tm, d), F32)],
        compiler_params=_params(3), name=name,
    )(h2, gates_t, x1, mod, gqf, wg, wu, wd)


def _rope_tables(pos):
    half = ROT_DIM // 2
    inv_freq = ROPE_THETA ** (-jnp.arange(half, dtype=F32) * 2.0 / ROT_DIM)
    ang = pos.astype(F32)[:, None] * inv_freq[None, :]
    cos, sin = jnp.cos(ang), jnp.sin(ang)
    l64 = np.arange(128) % HEAD_DIM
    idx = l64 % half
    first = (l64 < half)[None, :]
    second = ((l64 >= half) & (l64 < ROT_DIM))[None, :]
    cos_t = jnp.where(first | second, cos[:, idx], 1.0)
    s1_t = jnp.where(first, -sin[:, idx], 0.0)
    s2_t = jnp.where(second, sin[:, idx], 0.0)
    return cos_t, s1_t, s2_t


def kernel(x_prompt, x_sample, cache_k, cache_v, state_conv, c_prompt, c_sample, w_mod, b_mod, g_pre_mix, g_post_mix, g_pre_ffn, g_post_ffn, w_in, conv_w, conv_b, conv_ln_g, conv_ln_b, w_out, ffn_w_gate, ffn_w_up, ffn_w_down, moe_w_router, moe_w_gate, moe_w_up, moe_w_down):
    bp, seq, d = x_prompt.shape
    nb, n_new, _ = x_sample.shape
    depth = w_mod.shape[0]
    w_buf = cache_k.shape[2]
    past_len = PAST_LEN
    assert w_buf == min(DILATION_PATTERNS[-1][0], past_len)
    keep = min(DILATION_PATTERNS[-1][0], seq)
    ns = nb * n_new

    c_all = jnp.concatenate([c_sample, c_prompt, jnp.zeros((MOD_ROWS - nb - bp, d), F32)], axis=0)
    mod_all = _mod_call(c_all, w_mod, b_mod)

    tm_p = 512
    tabs_p = tuple(tb.reshape(seq // tm_p, tm_p, 128) for tb in _rope_tables(jnp.arange(seq, dtype=jnp.int32)))
    tabs_s = tuple(tb.reshape(n_new, 1, 128)
                   for tb in _rope_tables(past_len + jnp.arange(n_new, dtype=jnp.int32)))

    cache_kt = cache_k.transpose(0, 1, 3, 4, 2)
    cache_vt = cache_v.transpose(0, 1, 3, 4, 2)

    yp = x_prompt
    ys = x_sample.transpose(1, 0, 2).reshape(1, ns, d)
    outs = [[] for _ in range(6)]
    for l in range(depth):
        mod_p = mod_all[l].reshape(MOD_ROWS, 1, 6 * d)
        mod_s = jnp.tile(mod_all[l, :nb], (n_new, 1))[None]
        idx_p2 = lambda bi, i: (nb + bi, 0)
        idx_s2 = lambda bi, i: (0, i)
        w_in_bf = w_in[l].astype(BF16)
        w_out_bf = w_out[l].astype(BF16)
        conv_p = (jnp.pad(conv_w[l], ((0, 1), (0, 0))), conv_b[l][None], conv_ln_g[l][None], conv_ln_b[l][None])
        gpre, gpm, gpf, gqf = g_pre_mix[l][None], g_post_mix[l][None], g_pre_ffn[l][None], g_post_ffn[l][None]

        q, k, v, kt, vt, cv, ul = _inproj_call(yp, mod_p, 1, idx_p2, gpre, w_in_bf, tabs_p, tm_p, conv_p,
                                               tm=tm_p, keep=keep, name=f"inproj_prompt_{l}")
        att = _attn_call(q, k, v)
        outs[0].append(kt.reshape(bp, keep, N_HEADS, HEAD_DIM))
        outs[1].append(vt.reshape(bp, keep, N_HEADS, HEAD_DIM))
        outs[2].append(ul[:, CONV_HALO - (CONV_WIDTH - 1):])

        qs, _, _, kts, vts, us = _inproj_call(ys, mod_s, nb, idx_s2, gpre, w_in_bf, tabs_s, 1, None,
                                              tm=nb, keep=ns, name=f"inproj_sample_{l}")
        to_batch_major = lambda z: z.reshape(n_new, nb, N_HEADS, HEAD_DIM).transpose(1, 0, 2, 3)
        q_bm = (qs.reshape(HEAD_PAIRS, n_new, nb, 2, HEAD_DIM).transpose(2, 1, 0, 3, 4)
                .reshape(nb, n_new, N_HEADS, HEAD_DIM))
        k_bm, v_bm = to_batch_major(kts), to_batch_major(vts)
        head_major = lambda z: jnp.pad(z.transpose(0, 2, 1, 3), ((0, 0), (0, 0), (0, SAMPLE_ROWS - n_new), (0, 0)))
        att_s = _sample_attn_call(head_major(q_bm), head_major(k_bm), head_major(v_bm),
                                  cache_kt, cache_vt, l, n_new)
        att_s = (att_s[:, :, :n_new].reshape(nb, HEAD_PAIRS, 2, n_new, HEAD_DIM).transpose(1, 3, 0, 2, 4)
                 .reshape(1, HEAD_PAIRS, ns, 128).astype(BF16))
        ufull = jnp.concatenate([state_conv[l].transpose(1, 0, 2), us.reshape(n_new, nb, 512)], axis=0)
        cv_s = _sample_conv_call(ufull, conv_p).reshape(1, ns, 512)
        outs[3].append(k_bm)
        outs[4].append(v_bm)
        outs[5].append(ufull[n_new:].transpose(1, 0, 2))

        idx_p1 = lambda i, *_: (nb + i // (seq // tm_f), 0)
        idx_s1 = lambda i, *_: (0, i)
        if l % 2 == 0:
            tm_f = 1024
            wg, wu, wd = ffn_w_gate[l // 2], ffn_w_up[l // 2], ffn_w_down[l // 2]
            yp = _ffn_call(att, cv, yp, mod_p, 1, idx_p1, gpm, gpf, gqf, w_out_bf, wg, wu, wd,
                           tm=tm_f, tf=256, name=f"ffn_prompt_{l}")
            ys = _ffn_call(att_s, cv_s, ys, mod_s, ns, idx_s1, gpm, gpf, gqf, w_out_bf, wg, wu, wd,
                           tm=ns, tf=512, name=f"ffn_sample_{l}")
        else:
            tm_f = 1024
            wr = moe_w_router[l // 2]
            wg, wu, wd = moe_w_gate[l // 2], moe_w_up[l // 2], moe_w_down[l // 2]
            x1, h2, gates = _router_call(att, cv, yp, mod_p, 1, idx_p1, gpm, gpf, w_out_bf, wr,
                                         tm=tm_f, name=f"router_prompt_{l}")
            yp = _moe_call(h2, gates.transpose(0, 2, 1)[..., None], x1, mod_p, 1, idx_p1, gqf, wg, wu, wd,
                           tm=tm_f, tf=512, name=f"moe_prompt_{l}")
            x1s, h2s, gates_s = _router_call(att_s, cv_s, ys, mod_s, ns, idx_s1, gpm, gpf, w_out_bf, wr,
                                             tm=ns, name=f"router_sample_{l}")
            ys = _moe_call(h2s, gates_s.transpose(0, 2, 1)[..., None], x1s, mod_s, ns, idx_s1, gqf, wg, wu, wd,
                           tm=ns, tf=512, name=f"moe_sample_{l}")

    y_sample = ys.reshape(n_new, nb, d).transpose(1, 0, 2)
    return (yp, y_sample) + tuple(jnp.stack(o) for o in outs)
```

```python
import functools

import numpy as np
import jax
import jax.numpy as jnp
from jax import lax
from jax.experimental import pallas as pl
from jax.experimental.pallas import tpu as pltpu
from jax.experimental.pallas import tpu_sc as plsc

F32 = jnp.float32
BF16 = jnp.bfloat16

HEAD_DIM = 64
N_HEADS = 8
ATT_WIDTH = N_HEADS * HEAD_DIM
HEAD_PAIRS = ATT_WIDTH // 128
CONV_WIDTH = 31
CONV_HALO = 32
DILATION_PATTERNS = ((128, 1), (512, 4), (2048, 16))
BAND = 128
ROT_DIM = HEAD_DIM // 4
ROPE_THETA = 500000.0
ATTN_SCALE = HEAD_DIM ** -0.5
N_EXPERTS = 8
PAST_LEN = 2048
EPS = 1e-6
NEG = -1e30
MOD_ROWS = 136
VMEM_LIMIT = 56 * 1024 * 1024


def _rms(x, g):
    return x * lax.rsqrt(jnp.mean(x * x, axis=-1, keepdims=True) + EPS) * g


def _silu(x):
    return x * jax.nn.sigmoid(x)


def _params(n_axes, vmem=VMEM_LIMIT):
    return pltpu.CompilerParams(dimension_semantics=("arbitrary",) * n_axes, vmem_limit_bytes=vmem)


def _mod_kernel(c_ref, w_ref, b_ref, o_ref):
    a = _silu(c_ref[...]).astype(BF16)
    o_ref[0] = jnp.dot(a, w_ref[0].astype(BF16), preferred_element_type=F32) + b_ref[0]


def _mod_call(c_all, w_mod, b_mod):
    depth, d, d6 = w_mod.shape
    return pl.pallas_call(
        _mod_kernel,
        grid=(depth, d6 // d),
        in_specs=[pl.BlockSpec((MOD_ROWS, d), lambda l, j: (0, 0)),
                  pl.BlockSpec((1, d, d), lambda l, j: (l, 0, j)),
                  pl.BlockSpec((1, 1, d), lambda l, j: (l, 0, j))],
        out_specs=pl.BlockSpec((1, MOD_ROWS, d), lambda l, j: (l, 0, j)),
        out_shape=jax.ShapeDtypeStruct((depth, MOD_ROWS, d6), F32),
        compiler_params=_params(2),
        name="adaln_mod",
    )(c_all, w_mod, b_mod.reshape(depth, 1, d6))


def _inproj_kernel(*refs, conv, tm):
    if conv:
        (x_ref, sh_ref, sc_ref, g_ref, w_ref, cos_ref, s1_ref, s2_ref,
         cw_ref, cb_ref, lg_ref, lb_ref,
         q_ref, k_ref, v_ref, kt_ref, vt_ref, cv_ref, ul_ref, ubuf) = refs
    else:
        (x_ref, sh_ref, sc_ref, g_ref, w_ref, cos_ref, s1_ref, s2_ref,
         q_ref, k_ref, v_ref, kt_ref, vt_ref, u_ref) = refs

    h = _rms(x_ref[0], g_ref[...]) * (1.0 + sc_ref[0]) + sh_ref[0]
    proj = jnp.dot(h.astype(BF16), w_ref[...], preferred_element_type=F32)
    cos, s1, s2 = cos_ref[0], s1_ref[0], s2_ref[0]

    def rope(z):
        return z * cos + pltpu.roll(z, 128 - ROT_DIM // 2, 1) * s1 + pltpu.roll(z, ROT_DIM // 2, 1) * s2

    for hp in range(HEAD_PAIRS):
        lo, hi = hp * 128, (hp + 1) * 128
        q_ref[0, hp] = rope(proj[:, lo:hi]) * ATTN_SCALE
        kz = rope(proj[:, ATT_WIDTH + lo:ATT_WIDTH + hi])
        k_ref[0, hp] = kz
        kt_ref[0, :, lo:hi] = kz
        vz = proj[:, 2 * ATT_WIDTH + lo:2 * ATT_WIDTH + hi]
        v_ref[0, hp] = vz
        vt_ref[0, :, lo:hi] = vz

    a = proj[:, 3 * ATT_WIDTH:3 * ATT_WIDTH + 512]
    gate = proj[:, 3 * ATT_WIDTH + 512:]
    u = a * jax.nn.sigmoid(gate)
    if not conv:
        u_ref[0] = u
        return

    @pl.when(pl.program_id(1) == 0)
    def _():
        ubuf[0, 0:CONV_HALO, :] = jnp.zeros((CONV_HALO, 512), F32)

    ubuf[0, CONV_HALO:CONV_HALO + tm, :] = u
    span = tm + CONV_HALO - 8
    for s in range(1, 8):
        ubuf[s, 0:span, :] = ubuf[0, s:s + span, :]
    off0 = CONV_HALO - (CONV_WIDTH - 1)
    rows = 32

    def chunk(r, carry):
        base = pl.multiple_of(r * rows, rows)
        acc = jnp.broadcast_to(cb_ref[...], (rows, 512))
        for j in range(CONV_WIDTH):
            a, s = divmod(off0 + j, 8)
            acc = acc + cw_ref[j:j + 1, :] * ubuf[s, pl.ds(base + 8 * a, rows), :]
        mu = jnp.mean(acc, axis=-1, keepdims=True)
        xc = acc - mu
        var = jnp.mean(xc * xc, axis=-1, keepdims=True)
        y = xc * lax.rsqrt(var + EPS) * lg_ref[...] + lb_ref[...]
        cv_ref[0, pl.ds(base, rows), :] = _silu(y).astype(BF16)
        return carry

    lax.fori_loop(0, tm // rows, chunk, 0)
    tail = ubuf[0, tm:tm + CONV_HALO, :]
    ul_ref[0] = tail
    ubuf[0, 0:CONV_HALO, :] = tail


def _mod_spec(rows, chunk, index_fn):
    return pl.BlockSpec((1, rows, 1024), lambda *g: index_fn(*g) + (chunk,))


def _inproj_call(x, mod, mod_rows, mod_idx, g, w_bf, tabs, tab_rows, conv_p, *, tm, keep, name):
    b, t, d = x.shape
    nt = t // tm
    off = (t - keep) // tm
    conv = conv_p is not None
    tab_spec = pl.BlockSpec((1, tab_rows, 128), lambda bi, i: (i, 0, 0))
    in_specs = [pl.BlockSpec((1, tm, d), lambda bi, i: (bi, i, 0)),
                _mod_spec(mod_rows, 0, mod_idx), _mod_spec(mod_rows, 1, mod_idx),
                pl.BlockSpec((1, d), lambda bi, i: (0, 0)),
                pl.BlockSpec(w_bf.shape, lambda bi, i: (0, 0)),
                tab_spec, tab_spec, tab_spec]
    args = [x, mod, mod, g, w_bf, *tabs]
    hp_spec = pl.BlockSpec((1, HEAD_PAIRS, tm, 128), lambda bi, i: (bi, 0, i, 0))
    tail_spec = pl.BlockSpec((1, tm, 512), lambda bi, i: (bi, jnp.maximum(i - off, 0), 0))
    hp_shape = jax.ShapeDtypeStruct((b, HEAD_PAIRS, t, 128), F32)
    tail_shape = jax.ShapeDtypeStruct((b, keep, 512), F32)
    out_specs = [hp_spec, hp_spec, hp_spec, tail_spec, tail_spec]
    out_shape = [hp_shape, hp_shape, hp_shape, tail_shape, tail_shape]
    scratch = []
    if conv:
        small = pl.BlockSpec((1, 512), lambda bi, i: (0, 0))
        in_specs += [pl.BlockSpec((32, 512), lambda bi, i: (0, 0)), small, small, small]
        args += list(conv_p)
        out_specs += [pl.BlockSpec((1, tm, 512), lambda bi, i: (bi, i, 0)),
                      pl.BlockSpec((1, CONV_HALO, 512), lambda bi, i: (bi, 0, 0))]
        out_shape += [jax.ShapeDtypeStruct((b, t, 512), BF16),
                      jax.ShapeDtypeStruct((b, CONV_HALO, 512), F32)]
        scratch = [pltpu.VMEM((8, tm + CONV_HALO, 512), F32)]
    else:
        out_specs += [pl.BlockSpec((1, tm, 512), lambda bi, i: (bi, i, 0))]
        out_shape += [jax.ShapeDtypeStruct((b, t, 512), F32)]
    return pl.pallas_call(
        functools.partial(_inproj_kernel, conv=conv, tm=tm),
        grid=(b, nt), in_specs=in_specs, out_specs=out_specs, out_shape=out_shape,
        scratch_shapes=scratch, compiler_params=_params(2), name=name,
    )(*args)


def _attn_kernel(q_ref, kc_ref, kp_ref, vc_ref, vp_ref, o_ref, acc_s, m_s, l_s, *, tq):
    first_kj = jnp.where(pl.program_id(2) == 0, 128, 0)
    lane_lo = lax.broadcasted_iota(jnp.int32, (128, 128), 1) < HEAD_DIM
    qi = lax.broadcasted_iota(jnp.int32, (256, 256), 0)
    qi = jnp.where(qi >= 128, qi - 128, qi)
    kj = lax.broadcasted_iota(jnp.int32, (256, 256), 1)
    band = (kj >= qi) & (kj <= qi + BAND)
    band_first = band & (kj >= first_kj)

    ones_cols = jnp.ones((256, 128), BF16)
    nt = (((1,), (1,)), ((), ()))

    def strided(start, size, d):
        if d > 1:
            return pl.ds(start, size, stride=d)
        return pl.ds(start if isinstance(start, int) else pl.multiple_of(start, 128), size)

    def blocks(items):
        loaded = []
        for rows, first, c, d, s in items:
            if first:
                take = lambda rp, rc: jnp.concatenate(
                    [rp[0, 0, strided(tq - BAND * d + c, 128, d), :],
                     rc[0, 0, strided(c, 128, d), :]], axis=0)
            else:
                krows = strided(c + d * 128 * (s - 1), 256, d)
                take = lambda rp, rc: rc[0, 0, krows, :]
            loaded.append((q_ref[0, 0, rows, :], take(kp_ref, kc_ref), take(vp_ref, vc_ref),
                           band_first if first else band))
        scores = []
        for q2, k2, _, _ in loaded:
            qq = jnp.concatenate([jnp.where(lane_lo, q2, 0.0), jnp.where(lane_lo, 0.0, q2)], axis=0)
            scores.append(lax.dot_general(qq.astype(BF16), k2.astype(BF16), nt,
                                          preferred_element_type=F32))
        probs = []
        for sc, (_, _, _, mask) in zip(scores, loaded):
            sc = jnp.where(mask, sc, NEG)
            mb = jnp.max(sc, axis=1, keepdims=True)
            probs.append((mb, jnp.exp(sc - mb).astype(BF16)))
        out = []
        for (mb, p), (_, _, v2, _), item in zip(probs, loaded, items):
            pv = jnp.dot(p, jnp.concatenate([v2.astype(BF16), ones_cols], axis=1), preferred_element_type=F32)
            out.append((item[0], jnp.where(lane_lo, mb[:128], mb[128:]),
                        jnp.where(lane_lo, pv[:128, 128:], pv[128:, 128:]),
                        jnp.where(lane_lo, pv[:128, :128], pv[128:, :128])))
        return out

    def merge(rows, mt, lt, pvt):
        mp, lp, ap = m_s[rows, :], l_s[rows, :], acc_s[rows, :]
        mn = jnp.maximum(mp, mt)
        a = jnp.exp(mp - mn)
        b = jnp.exp(mt - mn)
        return mn, a * lp + b * lt, a * ap + b * pvt

    def store_init(results):
        for rows, mt, lt, pvt in results:
            m_s[rows, :] = mt
            l_s[rows, :] = lt
            acc_s[rows, :] = pvt

    def store_merged(results):
        for rows, mt, lt, pvt in results:
            mn, ln, an = merge(rows, mt, lt, pvt)
            m_s[rows, :] = mn
            l_s[rows, :] = ln
            acc_s[rows, :] = an

    def store_output(results):
        for rows, mt, lt, pvt in results:
            _, ln, an = merge(rows, mt, lt, pvt)
            o_ref[0, 0, rows, :] = (an / ln).astype(o_ref.dtype)

    def loop(n, fn):
        def body(i, carry):
            fn(i)
            return carry
        lax.fori_loop(0, n, body, 0)

    group = 4
    loop(16 // group, lambda g: store_init(blocks(
        [(pl.ds(g * group + j, 128, stride=16), True, g * group + j, 16, 0) for j in range(group)])))

    store_merged(blocks([(pl.ds(c, 128, stride=4), True, c, 4, 0) for c in range(4)]))
    loop(tq // 512 - 1, lambda n: store_merged(blocks(
        [(pl.ds(c + 512 * (n + 1), 128, stride=4), False, c, 4, n + 1) for c in range(4)])))

    store_output(blocks([(strided(0, 128, 1), True, 0, 1, 0)]
                        + [(strided(128 * s, 128, 1), False, 0, 1, s) for s in range(1, group)]))
    rest = 3
    assert (tq // 128 - group) % rest == 0
    loop((tq // 128 - group) // rest, lambda g: store_output(blocks(
        [(strided(128 * (group + g * rest + j), 128, 1), False, 0, 1, group + g * rest + j) for j in range(rest)])))


def _attn_call(q, k, v, *, tq=2048):
    b, hp, t, _ = q.shape
    assert tq == BAND * 16 and t % tq == 0
    cur = pl.BlockSpec((1, 1, tq, 128), lambda bi, h, i: (bi, h, i, 0))
    prev = pl.BlockSpec((1, 1, tq, 128), lambda bi, h, i: (bi, h, jnp.maximum(i - 1, 0), 0))
    return pl.pallas_call(
        functools.partial(_attn_kernel, tq=tq),
        grid=(b, hp, t // tq),
        in_specs=[cur, cur, prev, cur, prev],
        out_specs=cur,
        out_shape=jax.ShapeDtypeStruct((b, hp, t, 128), BF16),
        scratch_shapes=[pltpu.VMEM((tq, 128), F32)] * 3,
        compiler_params=_params(3), name="dilated_attn_prompt",
    )(q, k, k, v, v)


SAMPLE_ROWS = 8


def _sample_attn_kernel(q_ref, kn_ref, vn_ref, kt_ref, vt_ref, mh_ref, mn_ref, o_ref):
    mh, mn = mh_ref[...], mn_ref[...]
    nt = (((1,), (1,)), ((), ()))
    for h in range(N_HEADS):
        qh = q_ref[0, h].astype(BF16)
        s = jnp.dot(qh, kt_ref[0, 0, h].astype(BF16), preferred_element_type=F32)
        sn = lax.dot_general(qh, kn_ref[0, h].astype(BF16), nt, preferred_element_type=F32)
        s = jnp.where(mh > 0, s, NEG)
        sn = jnp.where(mn > 0, sn, NEG)
        m = jnp.maximum(jnp.max(s, axis=1, keepdims=True), jnp.max(sn, axis=1, keepdims=True))
        p = mh * jnp.exp(s - m)
        pn = mn * jnp.exp(sn - m)
        den = jnp.sum(p, axis=1, keepdims=True) + jnp.sum(pn, axis=1, keepdims=True)
        num = (lax.dot_general(p.astype(BF16), vt_ref[0, 0, h].astype(BF16), nt, preferred_element_type=F32)
               + jnp.dot(pn.astype(BF16), vn_ref[0, h].astype(BF16), preferred_element_type=F32))
        o_ref[0, h] = num / jnp.where(den > 0, den, 1.0)


def _sample_tables(w_buf, n_new):
    rows = np.arange(w_buf + n_new)
    mult = np.zeros((SAMPLE_ROWS, w_buf + SAMPLE_ROWS), np.float32)
    for t in range(n_new):
        dist = w_buf + t - rows
        for window, dil in DILATION_PATTERNS:
            mult[t, :w_buf + n_new] += (dist >= 0) & (dist <= window) & (dist % dil == 0)
    return jnp.asarray(mult[:, :w_buf]), jnp.asarray(mult[:, w_buf:])


def _sample_attn_call(q, kn, vn, cache_kt, cache_vt, layer, n_new):
    nb = q.shape[0]
    w_buf = cache_kt.shape[-1]
    mh, mn = _sample_tables(w_buf, n_new)
    new_spec = pl.BlockSpec((1, N_HEADS, SAMPLE_ROWS, HEAD_DIM), lambda i: (i, 0, 0, 0))
    cache_spec = pl.BlockSpec((1, 1, N_HEADS, HEAD_DIM, w_buf), lambda i: (layer, i, 0, 0, 0))
    return pl.pallas_call(
        _sample_attn_kernel,
        grid=(nb,),
        in_specs=[new_spec, new_spec, new_spec, cache_spec, cache_spec,
                  pl.BlockSpec(mh.shape, lambda i: (0, 0)), pl.BlockSpec(mn.shape, lambda i: (0, 0))],
        out_specs=new_spec,
        out_shape=jax.ShapeDtypeStruct((nb, N_HEADS, SAMPLE_ROWS, HEAD_DIM), F32),
        compiler_params=_params(1), name="dilated_attn_sample",
    )(q, kn, vn, cache_kt, cache_vt, mh, mn)


def _sample_conv_kernel(uf_ref, cw_ref, cb_ref, lg_ref, lb_ref, o_ref, *, n_new, nb):
    rows = 32

    def body(n, carry):
        t = n // (nb // rows)
        base = pl.multiple_of((n % (nb // rows)) * rows, rows)
        acc = jnp.broadcast_to(cb_ref[...], (rows, 512))
        for j in range(CONV_WIDTH):
            acc = acc + cw_ref[j:j + 1, :] * uf_ref[t + j, pl.ds(base, rows), :]
        mu = jnp.mean(acc, axis=-1, keepdims=True)
        xc = acc - mu
        var = jnp.mean(xc * xc, axis=-1, keepdims=True)
        y = xc * lax.rsqrt(var + EPS) * lg_ref[...] + lb_ref[...]
        o_ref[t, pl.ds(base, rows), :] = _silu(y).astype(BF16)
        return carry

    lax.fori_loop(0, n_new * (nb // rows), body, 0)


def _sample_conv_call(ufull, conv_p):
    n_full, nb, ch = ufull.shape
    n_new = n_full - (CONV_WIDTH - 1)
    return pl.pallas_call(
        functools.partial(_sample_conv_kernel, n_new=n_new, nb=nb),
        out_shape=jax.ShapeDtypeStruct((n_new, nb, ch), BF16),
        compiler_params=pltpu.CompilerParams(vmem_limit_bytes=VMEM_LIMIT),
        name="conv_sample",
    )(ufull, *conv_p)


def _mix_residual(att_ref, cv_ref, x_ref, gt1_ref, sc2_ref, sh2_ref, gpm_ref, gpf_ref, wo_ref):
    a = jnp.concatenate([att_ref[0, hp] for hp in range(HEAD_PAIRS)] + [cv_ref[0]], axis=-1)
    mix = jnp.dot(a, wo_ref[...], preferred_element_type=F32)
    x1 = x_ref[0] + gt1_ref[0] * _rms(mix, gpm_ref[...])
    h2 = _rms(x1, gpf_ref[...]) * (1.0 + sc2_ref[0]) + sh2_ref[0]
    return x1, h2


def _ffn_kernel(att_ref, cv_ref, x_ref, gt1_ref, sc2_ref, sh2_ref, gt2_ref, gpm_ref, gpf_ref, gqf_ref,
                wo_ref, wg_ref, wu_ref, wd_ref, o_ref, x1_s, h2_s, acc_s):
    f = pl.program_id(1)

    @pl.when(f == 0)
    def _():
        x1, h2 = _mix_residual(att_ref, cv_ref, x_ref, gt1_ref, sc2_ref, sh2_ref, gpm_ref, gpf_ref, wo_ref)
        x1_s[...] = x1
        h2_s[...] = h2.astype(BF16)
        acc_s[...] = jnp.zeros_like(acc_s)

    h2 = h2_s[...]
    g = jnp.dot(h2, wg_ref[...].astype(BF16), preferred_element_type=F32)
    u = jnp.dot(h2, wu_ref[...].astype(BF16), preferred_element_type=F32)
    act = (_silu(g) * u).astype(BF16)
    acc_s[...] += jnp.dot(act, wd_ref[...].astype(BF16), preferred_element_type=F32)

    @pl.when(f == pl.num_programs(1) - 1)
    def _():
        o_ref[0] = x1_s[...] + gt2_ref[0] * _rms(acc_s[...], gqf_ref[...])


def _mix_in_specs(b, t, tm, mod_rows, mod_idx, d):
    ntb = t // tm
    row = lambda i, *_: (i // ntb, i % ntb, 0)
    specs = [pl.BlockSpec((1, HEAD_PAIRS, tm, 128), lambda i, *_: (i // ntb, 0, i % ntb, 0)),
             pl.BlockSpec((1, tm, 512), row),
             pl.BlockSpec((1, tm, d), row)]
    return specs, row


def _ffn_call(att, cv, x, mod, mod_rows, mod_idx, gpm, gpf, gqf, wo_bf, wg, wu, wd, *, tm, tf, name):
    b, t, d = x.shape
    ff = wg.shape[1]
    ntb = t // tm
    specs, row = _mix_in_specs(b, t, tm, mod_rows, mod_idx, d)
    vec = pl.BlockSpec((1, d), lambda i, f: (0, 0))
    in_specs = specs + [_mod_spec(mod_rows, 2, mod_idx), _mod_spec(mod_rows, 4, mod_idx),
                        _mod_spec(mod_rows, 3, mod_idx), _mod_spec(mod_rows, 5, mod_idx),
                        vec, vec, vec,
                        pl.BlockSpec((d, d), lambda i, f: (0, 0)),
                        pl.BlockSpec((d, tf), lambda i, f: (0, f)),
                        pl.BlockSpec((d, tf), lambda i, f: (0, f)),
                        pl.BlockSpec((tf, d), lambda i, f: (f, 0))]
    return pl.pallas_call(
        _ffn_kernel,
        grid=(b * ntb, ff // tf),
        in_specs=in_specs,
        out_specs=pl.BlockSpec((1, tm, d), row),
        out_shape=jax.ShapeDtypeStruct((b, t, d), F32),
        scratch_shapes=[pltpu.VMEM((tm, d), F32), pltpu.VMEM((tm, d), BF16), pltpu.VMEM((tm, d), F32)],
        compiler_params=_params(2), name=name,
    )(att, cv, x, mod, mod, mod, mod, gpm, gpf, gqf, wo_bf, wg, wu, wd)


def _router_kernel(att_ref, cv_ref, x_ref, gt1_ref, sc2_ref, sh2_ref, gpm_ref, gpf_ref, wo_ref, wr_ref,
                   x1_ref, h2_ref, gate_ref, sel_ref):
    x1, h2 = _mix_residual(att_ref, cv_ref, x_ref, gt1_ref, sc2_ref, sh2_ref, gpm_ref, gpf_ref, wo_ref)
    x1_ref[0] = x1
    _store_split(h2_ref.at[:, 0], h2)
    logits = jnp.dot(h2, wr_ref[...], preferred_element_type=F32, precision=lax.Precision.HIGHEST)
    lane = lax.broadcasted_iota(jnp.int32, logits.shape, 1).astype(F32)
    v1 = jnp.max(logits, axis=-1, keepdims=True)
    i1 = jnp.min(jnp.where(logits == v1, lane, float(N_EXPERTS)), axis=-1, keepdims=True)
    oh1 = lane == i1
    rest = jnp.where(oh1, -jnp.inf, logits)
    v2 = jnp.max(rest, axis=-1, keepdims=True)
    i2 = jnp.min(jnp.where(rest == v2, lane, float(N_EXPERTS)), axis=-1, keepdims=True)
    oh2 = lane == i2
    e2 = jnp.exp(v2 - v1)
    den = 1.0 + e2
    gate_ref[0] = jnp.where(oh1, 1.0 / den, 0.0) + jnp.where(oh2, e2 / den, 0.0)
    sel_ref[0] = jnp.where(oh1 | oh2, 1.0, 0.0)


def _router_call(att, cv, x, mod, mod_rows, mod_idx, gpm, gpf, wo_bf, wr, *, tm, name):
    b, t, d = x.shape
    specs, row = _mix_in_specs(b, t, tm, mod_rows, mod_idx, d)
    vec = pl.BlockSpec((1, d), lambda i: (0, 0))
    in_specs = specs + [_mod_spec(mod_rows, 2, mod_idx), _mod_spec(mod_rows, 4, mod_idx),
                        _mod_spec(mod_rows, 3, mod_idx), vec, vec,
                        pl.BlockSpec((d, d), lambda i: (0, 0)),
                        pl.BlockSpec((d, N_EXPERTS), lambda i: (0, 0))]
    return pl.pallas_call(
        _router_kernel,
        grid=(b * (t // tm),),
        in_specs=in_specs,
        out_specs=[pl.BlockSpec((1, tm, d), row),
                   pl.BlockSpec((2, 1, tm, d // 4), lambda i: (0,) + row(i)),
                   pl.BlockSpec((1, tm, N_EXPERTS), row), pl.BlockSpec((1, tm, N_EXPERTS), row)],
        out_shape=[jax.ShapeDtypeStruct((b, t, d), F32), jax.ShapeDtypeStruct((2, b, t, d // 4), jnp.int32),
                   jax.ShapeDtypeStruct((b, t, N_EXPERTS), F32), jax.ShapeDtypeStruct((b, t, N_EXPERTS), F32)],
        compiler_params=_params(1), name=name,
    )(att, cv, x, mod, mod, mod, gpm, gpf, wo_bf, wr)


MOE_TILE = 1024
RANK_BLOCK = 1536
GATHER_WINDOW = 128
GATHER_ROWS = 32 * GATHER_WINDOW


def _pack_pairs(x):
    w = x.shape[1] // 2
    lo = pltpu.bitcast(x[:, :w].astype(BF16).astype(F32), jnp.int32)
    hi = pltpu.bitcast(x[:, w:].astype(BF16).astype(F32), jnp.int32)
    return hi | lax.shift_right_logical(lo, 16)


def _unpack_pairs(words):
    lo = pltpu.bitcast(lax.shift_left(words, 16), F32)
    hi = pltpu.bitcast(words & -65536, F32)
    return jnp.concatenate([lo, hi], axis=1)


def _store_split(ref, x):
    words = _pack_pairs(x)
    q = words.shape[1] // 2
    ref[0] = words[:, :q]
    ref[1] = words[:, q:]


def _load_split(ref):
    return _unpack_pairs(jnp.concatenate([ref[0], ref[1]], axis=1))


def _rank_kernel(sel_ref, rank_ref, cnt_ref, carry):
    @pl.when(pl.program_id(0) == 0)
    def _():
        carry[...] = jnp.zeros_like(carry)

    sel = sel_ref[...]
    rb = sel.shape[0]
    before = (lax.broadcasted_iota(jnp.int32, (rb, rb), 1)
              < lax.broadcasted_iota(jnp.int32, (rb, rb), 0)).astype(BF16)
    rank_ref[...] = jnp.dot(before, sel.astype(BF16), preferred_element_type=F32) + carry[...]
    carry[...] += jnp.sum(sel, axis=0, keepdims=True)
    cnt_ref[...] = carry[...]


def _rank_call(sel):
    n, n_e = sel.shape
    rb = RANK_BLOCK
    assert n % rb == 0
    return pl.pallas_call(
        _rank_kernel,
        grid=(n // rb,),
        in_specs=[pl.BlockSpec((rb, n_e), lambda i: (i, 0))],
        out_specs=[pl.BlockSpec((rb, n_e), lambda i: (i, 0)), pl.BlockSpec((1, n_e), lambda i: (0, 0))],
        out_shape=[jax.ShapeDtypeStruct((n, n_e), F32), jax.ShapeDtypeStruct((1, n_e), F32)],
        scratch_shapes=[pltpu.VMEM((1, n_e), F32)],
        compiler_params=_params(1), name="moe_rank",
    )(sel)


def _sc_gather(table, idxs):
    halves, _, width = table.shape
    n = idxs[0].shape[0]
    info = plsc.get_sparse_core_info()
    window = GATHER_WINDOW
    per_core = n // (info.num_cores * window)
    assert n % GATHER_ROWS == 0 and GATHER_ROWS == info.num_cores * info.num_subcores * window
    mesh = plsc.VectorSubcoreMesh(core_axis_name="core", subcore_axis_name="subcore")
    out_type = [jax.ShapeDtypeStruct((halves, n, width), table.dtype) for _ in idxs]

    @functools.partial(pl.kernel, out_type=out_type, mesh=mesh, scratch_types=[])
    def gather(table_hbm, *refs):
        idx_refs, out_refs = refs[:len(idxs)], refs[len(idxs):]
        for idx_hbm, out_hbm in zip(idx_refs, out_refs):
            for h in range(halves):
                def body(idx_vmem, out_vmem, h=h):
                    pltpu.sync_copy(table_hbm.at[h].at[idx_vmem.at[0]], out_vmem)

                pltpu.emit_pipeline(
                    body,
                    grid=(info.num_cores, per_core),
                    in_specs=[pl.BlockSpec((1, window), index_map=lambda c, i: (0, c * per_core + i))],
                    out_specs=[pl.BlockSpec((window, width), index_map=lambda c, i: (c * per_core + i, 0))],
                    core_axis_name=("core", "subcore"),
                    dimension_semantics=(pltpu.PARALLEL, pltpu.PARALLEL),
                )(idx_hbm, out_hbm.at[h])

    return gather(table, *[idx.reshape(1, n) for idx in idxs])


def _expert_kernel(te_ref, nu_ref, xs_ref, wg_ref, wu_ref, wd_ref, ys_ref, xb, acc):
    t, f = pl.program_id(0), pl.program_id(1)
    last = pl.num_programs(1) - 1
    used = t < nu_ref[0]

    @pl.when(used & (f == 0))
    def _():
        xb[...] = _load_split(xs_ref).astype(BF16)
        acc[...] = jnp.zeros_like(acc)

    @pl.when(used)
    def _():
        x = xb[...]
        g = jnp.dot(x, wg_ref[0].astype(BF16), preferred_element_type=F32)
        u = jnp.dot(x, wu_ref[0].astype(BF16), preferred_element_type=F32)
        act = (_silu(g) * u).astype(BF16)
        acc[...] += jnp.dot(act, wd_ref[0].astype(BF16), preferred_element_type=F32)

    @pl.when(used & (f == last))
    def _():
        _store_split(ys_ref, acc[...])

    @pl.when(jnp.logical_not(used) & (f == last))
    def _():
        ys_ref[...] = jnp.zeros_like(ys_ref)


def _expert_call(tile_expert, n_used, xs, wg, wu, wd, *, tf):
    _, rows, quarter = xs.shape
    n_e, d, ff = wg.shape
    tm = MOE_TILE
    nf = ff // tf
    chunk = lambda t, f, te, nu: jnp.where(t < nu[0], f, nf - 1)
    grid_spec = pltpu.PrefetchScalarGridSpec(
        num_scalar_prefetch=2,
        grid=(rows // tm, nf),
        in_specs=[pl.BlockSpec((2, tm, quarter), lambda t, f, te, nu: (0, t, 0)),
                  pl.BlockSpec((1, d, tf), lambda t, f, te, nu: (te[t], 0, chunk(t, f, te, nu))),
                  pl.BlockSpec((1, d, tf), lambda t, f, te, nu: (te[t], 0, chunk(t, f, te, nu))),
                  pl.BlockSpec((1, tf, d), lambda t, f, te, nu: (te[t], chunk(t, f, te, nu), 0))],
        out_specs=pl.BlockSpec((2, tm, quarter), lambda t, f, te, nu: (0, t, 0)),
        scratch_shapes=[pltpu.VMEM((tm, d), BF16), pltpu.VMEM((tm, d), F32)])
    return pl.pallas_call(
        _expert_kernel, grid_spec=grid_spec,
        out_shape=jax.ShapeDtypeStruct((2, rows, quarter), jnp.int32),
        compiler_params=_params(2), name="moe_experts",
    )(tile_expert, n_used, xs, wg, wu, wd)


def _moe_route(h2p, sel, gates, wg, wu, wd, *, tf):
    n, n_e = sel.shape
    tm = MOE_TILE
    rank, cnt = _rank_call(sel)
    counts = cnt[0].astype(jnp.int32)
    padded = (counts + tm - 1) // tm * tm
    seg_end = jnp.cumsum(padded)
    seg_start = seg_end - padded
    rows_max = -(-(2 * n + n_e * tm) // GATHER_ROWS) * GATHER_ROWS
    assert rows_max % tm == 0
    e_lo = jnp.argmax(sel, axis=1)
    e_hi = n_e - 1 - jnp.argmax(sel[:, ::-1], axis=1)
    pick = lambda a, e: jnp.take_along_axis(a, e[:, None], axis=1)[:, 0]
    rank_i = rank.astype(jnp.int32)
    d_lo = seg_start[e_lo] + pick(rank_i, e_lo)
    d_hi = seg_start[e_hi] + pick(rank_i, e_hi)
    w = jnp.stack([pick(gates, e_lo), pick(gates, e_hi)], axis=1)
    tok = jnp.arange(n, dtype=jnp.int32)
    src = jnp.zeros((rows_max,), jnp.int32).at[d_lo].set(tok).at[d_hi].set(tok)
    n_tiles = rows_max // tm
    n_used = (seg_end[-1] // tm).astype(jnp.int32)
    tile_ids = jnp.minimum(jnp.arange(n_tiles, dtype=jnp.int32), n_used - 1)
    tile_expert = jnp.minimum(jnp.searchsorted(seg_end, tile_ids * tm, side="right"), n_e - 1).astype(jnp.int32)

    xs, = _sc_gather(h2p, [src])
    ys = _expert_call(tile_expert, n_used.reshape(1), xs, wg, wu, wd, tf=tf)
    n_pad = -(-n // GATHER_ROWS) * GATHER_ROWS
    pad_idx = lambda dd: jnp.pad(dd.astype(jnp.int32), (0, n_pad - n))
    ya, yb = _sc_gather(ys, [pad_idx(d_lo), pad_idx(d_hi)])
    return ya, yb, w


def _combine_kernel(ya_ref, yb_ref, w_ref, x1_ref, gt2_ref, gqf_ref, o_ref):
    w = w_ref[...]
    f = w[:, 0:1] * _load_split(ya_ref) + w[:, 1:2] * _load_split(yb_ref)
    o_ref[0] = x1_ref[0] + gt2_ref[0] * _rms(f, gqf_ref[...])


def _combine_call(ya, yb, w, x1, row0, mod, mod_rows, mod_idx, gqf, *, tm, name):
    b, t, d = x1.shape
    ntb = t // tm
    blk0 = row0 // tm
    assert row0 % tm == 0
    pool = lambda i: (blk0 + i, 0)
    pool3 = lambda i: (0, blk0 + i, 0)
    row = lambda i: (i // ntb, i % ntb, 0)
    return pl.pallas_call(
        _combine_kernel,
        grid=(b * ntb,),
        in_specs=[pl.BlockSpec((2, tm, d // 4), pool3), pl.BlockSpec((2, tm, d // 4), pool3),
                  pl.BlockSpec((tm, 2), pool), pl.BlockSpec((1, tm, d), row),
                  _mod_spec(mod_rows, 5, mod_idx), pl.BlockSpec((1, d), lambda i: (0, 0))],
        out_specs=pl.BlockSpec((1, tm, d), row),
        out_shape=jax.ShapeDtypeStruct((b, t, d), F32),
        compiler_params=_params(1), name=name,
    )(ya, yb, w, x1, mod, gqf)


def _rope_tables(pos):
    half = ROT_DIM // 2
    inv_freq = ROPE_THETA ** (-jnp.arange(half, dtype=F32) * 2.0 / ROT_DIM)
    ang = pos.astype(F32)[:, None] * inv_freq[None, :]
    cos, sin = jnp.cos(ang), jnp.sin(ang)
    l64 = np.arange(128) % HEAD_DIM
    idx = l64 % half
    first = (l64 < half)[None, :]
    second = ((l64 >= half) & (l64 < ROT_DIM))[None, :]
    cos_t = jnp.where(first | second, cos[:, idx], 1.0)
    s1_t = jnp.where(first, -sin[:, idx], 0.0)
    s2_t = jnp.where(second, sin[:, idx], 0.0)
    return cos_t, s1_t, s2_t


def kernel(x_prompt, x_sample, cache_k, cache_v, state_conv, c_prompt, c_sample, w_mod, b_mod, g_pre_mix, g_post_mix, g_pre_ffn, g_post_ffn, w_in, conv_w, conv_b, conv_ln_g, conv_ln_b, w_out, ffn_w_gate, ffn_w_up, ffn_w_down, moe_w_router, moe_w_gate, moe_w_up, moe_w_down):
    bp, seq, d = x_prompt.shape
    nb, n_new, _ = x_sample.shape
    depth = w_mod.shape[0]
    w_buf = cache_k.shape[2]
    past_len = PAST_LEN
    assert w_buf == min(DILATION_PATTERNS[-1][0], past_len)
    keep = min(DILATION_PATTERNS[-1][0], seq)
    ns = nb * n_new

    c_all = jnp.concatenate([c_sample, c_prompt, jnp.zeros((MOD_ROWS - nb - bp, d), F32)], axis=0)
    mod_all = _mod_call(c_all, w_mod, b_mod)

    tm_p = 512
    tabs_p = tuple(tb.reshape(seq // tm_p, tm_p, 128) for tb in _rope_tables(jnp.arange(seq, dtype=jnp.int32)))
    tabs_s = tuple(tb.reshape(n_new, 1, 128)
                   for tb in _rope_tables(past_len + jnp.arange(n_new, dtype=jnp.int32)))

    cache_kt = cache_k.transpose(0, 1, 3, 4, 2)
    cache_vt = cache_v.transpose(0, 1, 3, 4, 2)

    yp = x_prompt
    ys = x_sample.transpose(1, 0, 2).reshape(1, ns, d)
    outs = [[] for _ in range(6)]
    for l in range(depth):
        mod_p = mod_all[l].reshape(MOD_ROWS, 1, 6 * d)
        mod_s = jnp.tile(mod_all[l, :nb], (n_new, 1))[None]
        idx_p2 = lambda bi, i: (nb + bi, 0)
        idx_s2 = lambda bi, i: (0, i)
        w_in_bf = w_in[l].astype(BF16)
        w_out_bf = w_out[l].astype(BF16)
        conv_p = (jnp.pad(conv_w[l], ((0, 1), (0, 0))), conv_b[l][None], conv_ln_g[l][None], conv_ln_b[l][None])
        gpre, gpm, gpf, gqf = g_pre_mix[l][None], g_post_mix[l][None], g_pre_ffn[l][None], g_post_ffn[l][None]

        q, k, v, kt, vt, cv, ul = _inproj_call(yp, mod_p, 1, idx_p2, gpre, w_in_bf, tabs_p, tm_p, conv_p,
                                               tm=tm_p, keep=keep, name=f"inproj_prompt_{l}")
        att = _attn_call(q, k, v)
        outs[0].append(kt.reshape(bp, keep, N_HEADS, HEAD_DIM))
        outs[1].append(vt.reshape(bp, keep, N_HEADS, HEAD_DIM))
        outs[2].append(ul[:, CONV_HALO - (CONV_WIDTH - 1):])

        qs, _, _, kts, vts, us = _inproj_call(ys, mod_s, nb, idx_s2, gpre, w_in_bf, tabs_s, 1, None,
                                              tm=nb, keep=ns, name=f"inproj_sample_{l}")
        to_batch_major = lambda z: z.reshape(n_new, nb, N_HEADS, HEAD_DIM).transpose(1, 0, 2, 3)
        q_bm = (qs.reshape(HEAD_PAIRS, n_new, nb, 2, HEAD_DIM).transpose(2, 1, 0, 3, 4)
                .reshape(nb, n_new, N_HEADS, HEAD_DIM))
        k_bm, v_bm = to_batch_major(kts), to_batch_major(vts)
        head_major = lambda z: jnp.pad(z.transpose(0, 2, 1, 3), ((0, 0), (0, 0), (0, SAMPLE_ROWS - n_new), (0, 0)))
        att_s = _sample_attn_call(head_major(q_bm), head_major(k_bm), head_major(v_bm),
                                  cache_kt, cache_vt, l, n_new)
        att_s = (att_s[:, :, :n_new].reshape(nb, HEAD_PAIRS, 2, n_new, HEAD_DIM).transpose(1, 3, 0, 2, 4)
                 .reshape(1, HEAD_PAIRS, ns, 128).astype(BF16))
        ufull = jnp.concatenate([state_conv[l].transpose(1, 0, 2), us.reshape(n_new, nb, 512)], axis=0)
        cv_s = _sample_conv_call(ufull, conv_p).reshape(1, ns, 512)
        outs[3].append(k_bm)
        outs[4].append(v_bm)
        outs[5].append(ufull[n_new:].transpose(1, 0, 2))

        idx_p1 = lambda i, *_: (nb + i // (seq // tm_f), 0)
        idx_s1 = lambda i, *_: (0, i)
        if l % 2 == 0:
            tm_f = 1024
            wg, wu, wd = ffn_w_gate[l // 2], ffn_w_up[l // 2], ffn_w_down[l // 2]
            yp = _ffn_call(att, cv, yp, mod_p, 1, idx_p1, gpm, gpf, gqf, w_out_bf, wg, wu, wd,
                           tm=tm_f, tf=256, name=f"ffn_prompt_{l}")
            ys = _ffn_call(att_s, cv_s, ys, mod_s, ns, idx_s1, gpm, gpf, gqf, w_out_bf, wg, wu, wd,
                           tm=ns, tf=512, name=f"ffn_sample_{l}")
        else:
            tm_f = 1024
            wr = moe_w_router[l // 2]
            wg, wu, wd = moe_w_gate[l // 2], moe_w_up[l // 2], moe_w_down[l // 2]
            x1, h2, gates, sel = _router_call(att, cv, yp, mod_p, 1, idx_p1, gpm, gpf, w_out_bf, wr,
                                              tm=tm_f, name=f"router_prompt_{l}")
            x1s, h2s, gates_s, sel_s = _router_call(att_s, cv_s, ys, mod_s, ns, idx_s1, gpm, gpf, w_out_bf, wr,
                                                    tm=ns, name=f"router_sample_{l}")
            pool = lambda a, b_: jnp.concatenate([a.reshape(bp * seq, -1), b_.reshape(ns, -1)], axis=0)
            h2_pool = jnp.concatenate([h2.reshape(2, bp * seq, d // 4), h2s.reshape(2, ns, d // 4)], axis=1)
            ya, yb, w12 = _moe_route(h2_pool, pool(sel, sel_s), pool(gates, gates_s), wg, wu, wd, tf=512)
            yp = _combine_call(ya, yb, w12, x1, 0, mod_p, 1, idx_p1, gqf, tm=tm_f, name=f"moe_combine_prompt_{l}")
            ys = _combine_call(ya, yb, w12, x1s, bp * seq, mod_s, ns, idx_s1, gqf, tm=ns,
                               name=f"moe_combine_sample_{l}")

    y_sample = ys.reshape(n_new, nb, d).transpose(1, 0, 2)
    return (yp, y_sample) + tuple(jnp.stack(o) for o in outs)
```

```python
import functools

import numpy as np
import jax
import jax.numpy as jnp
from jax import lax
from jax.experimental import pallas as pl
from jax.experimental.pallas import tpu as pltpu
from jax.experimental.pallas import tpu_sc as plsc

F32 = jnp.float32
BF16 = jnp.bfloat16

HEAD_DIM = 64
N_HEADS = 8
ATT_WIDTH = N_HEADS * HEAD_DIM
HEAD_PAIRS = ATT_WIDTH // 128
CONV_WIDTH = 31
CONV_HALO = 32
DILATION_PATTERNS = ((128, 1), (512, 4), (2048, 16))
BAND = 128
ROT_DIM = HEAD_DIM // 4
ROPE_THETA = 500000.0
ATTN_SCALE = HEAD_DIM ** -0.5
N_EXPERTS = 8
PAST_LEN = 2048
EPS = 1e-6
NEG = -1e30
MOD_ROWS = 136
VMEM_LIMIT = 56 * 1024 * 1024


def _rms(x, g):
    return x * lax.rsqrt(jnp.mean(x * x, axis=-1, keepdims=True) + EPS) * g


def _silu(x):
    return x * jax.nn.sigmoid(x)


def _params(n_axes, vmem=VMEM_LIMIT):
    return pltpu.CompilerParams(dimension_semantics=("arbitrary",) * n_axes, vmem_limit_bytes=vmem)


def _mod_kernel(c_ref, w_ref, b_ref, o_ref):
    a = _silu(c_ref[...]).astype(BF16)
    o_ref[0] = jnp.dot(a, w_ref[0].astype(BF16), preferred_element_type=F32) + b_ref[0]


def _mod_call(c_all, w_mod, b_mod):
    depth, d, d6 = w_mod.shape
    return pl.pallas_call(
        _mod_kernel,
        grid=(depth, d6 // d),
        in_specs=[pl.BlockSpec((MOD_ROWS, d), lambda l, j: (0, 0)),
                  pl.BlockSpec((1, d, d), lambda l, j: (l, 0, j)),
                  pl.BlockSpec((1, 1, d), lambda l, j: (l, 0, j))],
        out_specs=pl.BlockSpec((1, MOD_ROWS, d), lambda l, j: (l, 0, j)),
        out_shape=jax.ShapeDtypeStruct((depth, MOD_ROWS, d6), F32),
        compiler_params=_params(2),
        name="adaln_mod",
    )(c_all, w_mod, b_mod.reshape(depth, 1, d6))


def _inproj_kernel(*refs, conv, tm):
    if conv:
        (x_ref, sh_ref, sc_ref, g_ref, w_ref, cos_ref, s1_ref, s2_ref,
         cw_ref, cb_ref, lg_ref, lb_ref,
         q_ref, k_ref, v_ref, kt_ref, vt_ref, cv_ref, ul_ref, ubuf) = refs
    else:
        (x_ref, sh_ref, sc_ref, g_ref, w_ref, cos_ref, s1_ref, s2_ref,
         q_ref, k_ref, v_ref, kt_ref, vt_ref, u_ref) = refs

    h = _rms(x_ref[0], g_ref[...]) * (1.0 + sc_ref[0]) + sh_ref[0]
    proj = jnp.dot(h.astype(BF16), w_ref[...], preferred_element_type=F32)
    cos, s1, s2 = cos_ref[0], s1_ref[0], s2_ref[0]

    def rope(z):
        return z * cos + pltpu.roll(z, 128 - ROT_DIM // 2, 1) * s1 + pltpu.roll(z, ROT_DIM // 2, 1) * s2

    for hp in range(HEAD_PAIRS):
        lo, hi = hp * 128, (hp + 1) * 128
        q_ref[0, hp] = rope(proj[:, lo:hi]) * ATTN_SCALE
        kz = rope(proj[:, ATT_WIDTH + lo:ATT_WIDTH + hi])
        k_ref[0, hp] = kz
        kt_ref[0, :, lo:hi] = kz
        vz = proj[:, 2 * ATT_WIDTH + lo:2 * ATT_WIDTH + hi]
        v_ref[0, hp] = vz
        vt_ref[0, :, lo:hi] = vz

    a = proj[:, 3 * ATT_WIDTH:3 * ATT_WIDTH + 512]
    gate = proj[:, 3 * ATT_WIDTH + 512:]
    u = a * jax.nn.sigmoid(gate)
    if not conv:
        u_ref[0] = u
        return

    @pl.when(pl.program_id(1) == 0)
    def _():
        ubuf[0, 0:CONV_HALO, :] = jnp.zeros((CONV_HALO, 512), F32)

    ubuf[0, CONV_HALO:CONV_HALO + tm, :] = u
    span = tm + CONV_HALO - 8
    for s in range(1, 8):
        ubuf[s, 0:span, :] = ubuf[0, s:s + span, :]
    off0 = CONV_HALO - (CONV_WIDTH - 1)
    rows = 32

    def chunk(r, carry):
        base = pl.multiple_of(r * rows, rows)
        acc = jnp.broadcast_to(cb_ref[...], (rows, 512))
        for j in range(CONV_WIDTH):
            a, s = divmod(off0 + j, 8)
            acc = acc + cw_ref[j:j + 1, :] * ubuf[s, pl.ds(base + 8 * a, rows), :]
        mu = jnp.mean(acc, axis=-1, keepdims=True)
        xc = acc - mu
        var = jnp.mean(xc * xc, axis=-1, keepdims=True)
        y = xc * lax.rsqrt(var + EPS) * lg_ref[...] + lb_ref[...]
        cv_ref[0, pl.ds(base, rows), :] = _silu(y).astype(BF16)
        return carry

    lax.fori_loop(0, tm // rows, chunk, 0)
    tail = ubuf[0, tm:tm + CONV_HALO, :]
    ul_ref[0] = tail
    ubuf[0, 0:CONV_HALO, :] = tail


def _mod_spec(rows, chunk, index_fn):
    return pl.BlockSpec((1, rows, 1024), lambda *g: index_fn(*g) + (chunk,))


def _inproj_call(x, mod, mod_rows, mod_idx, g, w_bf, tabs, tab_rows, conv_p, *, tm, keep, name):
    b, t, d = x.shape
    nt = t // tm
    off = (t - keep) // tm
    conv = conv_p is not None
    tab_spec = pl.BlockSpec((1, tab_rows, 128), lambda bi, i: (i, 0, 0))
    in_specs = [pl.BlockSpec((1, tm, d), lambda bi, i: (bi, i, 0)),
                _mod_spec(mod_rows, 0, mod_idx), _mod_spec(mod_rows, 1, mod_idx),
                pl.BlockSpec((1, d), lambda bi, i: (0, 0)),
                pl.BlockSpec(w_bf.shape, lambda bi, i: (0, 0)),
                tab_spec, tab_spec, tab_spec]
    args = [x, mod, mod, g, w_bf, *tabs]
    hp_spec = pl.BlockSpec((1, HEAD_PAIRS, tm, 128), lambda bi, i: (bi, 0, i, 0))
    tail_spec = pl.BlockSpec((1, tm, 512), lambda bi, i: (bi, jnp.maximum(i - off, 0), 0))
    hp_shape = jax.ShapeDtypeStruct((b, HEAD_PAIRS, t, 128), F32)
    tail_shape = jax.ShapeDtypeStruct((b, keep, 512), F32)
    out_specs = [hp_spec, hp_spec, hp_spec, tail_spec, tail_spec]
    out_shape = [hp_shape, hp_shape, hp_shape, tail_shape, tail_shape]
    scratch = []
    if conv:
        small = pl.BlockSpec((1, 512), lambda bi, i: (0, 0))
        in_specs += [pl.BlockSpec((32, 512), lambda bi, i: (0, 0)), small, small, small]
        args += list(conv_p)
        out_specs += [pl.BlockSpec((1, tm, 512), lambda bi, i: (bi, i, 0)),
                      pl.BlockSpec((1, CONV_HALO, 512), lambda bi, i: (bi, 0, 0))]
        out_shape += [jax.ShapeDtypeStruct((b, t, 512), BF16),
                      jax.ShapeDtypeStruct((b, CONV_HALO, 512), F32)]
        scratch = [pltpu.VMEM((8, tm + CONV_HALO, 512), F32)]
    else:
        out_specs += [pl.BlockSpec((1, tm, 512), lambda bi, i: (bi, i, 0))]
        out_shape += [jax.ShapeDtypeStruct((b, t, 512), F32)]
    return pl.pallas_call(
        functools.partial(_inproj_kernel, conv=conv, tm=tm),
        grid=(b, nt), in_specs=in_specs, out_specs=out_specs, out_shape=out_shape,
        scratch_shapes=scratch, compiler_params=_params(2), name=name,
    )(*args)


def _attn_kernel(q_ref, kc_ref, kp_ref, vc_ref, vp_ref, o_ref, acc_s, m_s, l_s, *, tq):
    first_kj = jnp.where(pl.program_id(2) == 0, 128, 0)
    lane_lo = lax.broadcasted_iota(jnp.int32, (128, 128), 1) < HEAD_DIM
    qi = lax.broadcasted_iota(jnp.int32, (256, 256), 0)
    qi = jnp.where(qi >= 128, qi - 128, qi)
    kj = lax.broadcasted_iota(jnp.int32, (256, 256), 1)
    band = (kj >= qi) & (kj <= qi + BAND)
    band_first = band & (kj >= first_kj)

    ones_cols = jnp.ones((256, 128), BF16)
    nt = (((1,), (1,)), ((), ()))

    def strided(start, size, d):
        if d > 1:
            return pl.ds(start, size, stride=d)
        return pl.ds(start if isinstance(start, int) else pl.multiple_of(start, 128), size)

    def blocks(items):
        loaded = []
        for rows, first, c, d, s in items:
            if first:
                take = lambda rp, rc: jnp.concatenate(
                    [rp[0, 0, strided(tq - BAND * d + c, 128, d), :],
                     rc[0, 0, strided(c, 128, d), :]], axis=0)
            else:
                krows = strided(c + d * 128 * (s - 1), 256, d)
                take = lambda rp, rc: rc[0, 0, krows, :]
            loaded.append((q_ref[0, 0, rows, :], take(kp_ref, kc_ref), take(vp_ref, vc_ref),
                           band_first if first else band))
        scores = []
        for q2, k2, _, _ in loaded:
            qq = jnp.concatenate([jnp.where(lane_lo, q2, 0.0), jnp.where(lane_lo, 0.0, q2)], axis=0)
            scores.append(lax.dot_general(qq.astype(BF16), k2.astype(BF16), nt,
                                          preferred_element_type=F32))
        probs = []
        for sc, (_, _, _, mask) in zip(scores, loaded):
            sc = jnp.where(mask, sc, NEG)
            mb = jnp.max(sc, axis=1, keepdims=True)
            probs.append((mb, jnp.exp(sc - mb).astype(BF16)))
        out = []
        for (mb, p), (_, _, v2, _), item in zip(probs, loaded, items):
            pv = jnp.dot(p, jnp.concatenate([v2.astype(BF16), ones_cols], axis=1), preferred_element_type=F32)
            out.append((item[0], jnp.where(lane_lo, mb[:128], mb[128:]),
                        jnp.where(lane_lo, pv[:128, 128:], pv[128:, 128:]),
                        jnp.where(lane_lo, pv[:128, :128], pv[128:, :128])))
        return out

    def merge(rows, mt, lt, pvt):
        mp, lp, ap = m_s[rows, :], l_s[rows, :], acc_s[rows, :]
        mn = jnp.maximum(mp, mt)
        a = jnp.exp(mp - mn)
        b = jnp.exp(mt - mn)
        return mn, a * lp + b * lt, a * ap + b * pvt

    def store_init(results):
        for rows, mt, lt, pvt in results:
            m_s[rows, :] = mt
            l_s[rows, :] = lt
            acc_s[rows, :] = pvt

    def store_merged(results):
        for rows, mt, lt, pvt in results:
            mn, ln, an = merge(rows, mt, lt, pvt)
            m_s[rows, :] = mn
            l_s[rows, :] = ln
            acc_s[rows, :] = an

    def store_output(results):
        for rows, mt, lt, pvt in results:
            _, ln, an = merge(rows, mt, lt, pvt)
            o_ref[0, 0, rows, :] = (an / ln).astype(o_ref.dtype)

    def loop(n, fn):
        def body(i, carry):
            fn(i)
            return carry
        lax.fori_loop(0, n, body, 0)

    group = 4
    loop(16 // group, lambda g: store_init(blocks(
        [(pl.ds(g * group + j, 128, stride=16), True, g * group + j, 16, 0) for j in range(group)])))

    store_merged(blocks([(pl.ds(c, 128, stride=4), True, c, 4, 0) for c in range(4)]))
    loop(tq // 512 - 1, lambda n: store_merged(blocks(
        [(pl.ds(c + 512 * (n + 1), 128, stride=4), False, c, 4, n + 1) for c in range(4)])))

    store_output(blocks([(strided(0, 128, 1), True, 0, 1, 0)]
                        + [(strided(128 * s, 128, 1), False, 0, 1, s) for s in range(1, group)]))
    rest = 3
    assert (tq // 128 - group) % rest == 0
    loop((tq // 128 - group) // rest, lambda g: store_output(blocks(
        [(strided(128 * (group + g * rest + j), 128, 1), False, 0, 1, group + g * rest + j) for j in range(rest)])))


def _attn_call(q, k, v, *, tq=2048):
    b, hp, t, _ = q.shape
    assert tq == BAND * 16 and t % tq == 0
    cur = pl.BlockSpec((1, 1, tq, 128), lambda bi, h, i: (bi, h, i, 0))
    prev = pl.BlockSpec((1, 1, tq, 128), lambda bi, h, i: (bi, h, jnp.maximum(i - 1, 0), 0))
    return pl.pallas_call(
        functools.partial(_attn_kernel, tq=tq),
        grid=(b, hp, t // tq),
        in_specs=[cur, cur, prev, cur, prev],
        out_specs=cur,
        out_shape=jax.ShapeDtypeStruct((b, hp, t, 128), BF16),
        scratch_shapes=[pltpu.VMEM((tq, 128), F32)] * 3,
        compiler_params=_params(3), name="dilated_attn_prompt",
    )(q, k, k, v, v)


SAMPLE_ROWS = 8


def _sample_attn_kernel(q_ref, kn_ref, vn_ref, kt_ref, vt_ref, mh_ref, mn_ref, o_ref):
    mh, mn = mh_ref[...], mn_ref[...]
    nt = (((1,), (1,)), ((), ()))
    for h in range(N_HEADS):
        qh = q_ref[0, h].astype(BF16)
        s = jnp.dot(qh, kt_ref[0, 0, h].astype(BF16), preferred_element_type=F32)
        sn = lax.dot_general(qh, kn_ref[0, h].astype(BF16), nt, preferred_element_type=F32)
        s = jnp.where(mh > 0, s, NEG)
        sn = jnp.where(mn > 0, sn, NEG)
        m = jnp.maximum(jnp.max(s, axis=1, keepdims=True), jnp.max(sn, axis=1, keepdims=True))
        p = mh * jnp.exp(s - m)
        pn = mn * jnp.exp(sn - m)
        den = jnp.sum(p, axis=1, keepdims=True) + jnp.sum(pn, axis=1, keepdims=True)
        num = (lax.dot_general(p.astype(BF16), vt_ref[0, 0, h].astype(BF16), nt, preferred_element_type=F32)
               + jnp.dot(pn.astype(BF16), vn_ref[0, h].astype(BF16), preferred_element_type=F32))
        o_ref[0, h] = num / jnp.where(den > 0, den, 1.0)


def _sample_tables(w_buf, n_new):
    rows = np.arange(w_buf + n_new)
    mult = np.zeros((SAMPLE_ROWS, w_buf + SAMPLE_ROWS), np.float32)
    for t in range(n_new):
        dist = w_buf + t - rows
        for window, dil in DILATION_PATTERNS:
            mult[t, :w_buf + n_new] += (dist >= 0) & (dist <= window) & (dist % dil == 0)
    return jnp.asarray(mult[:, :w_buf]), jnp.asarray(mult[:, w_buf:])


def _sample_attn_call(q, kn, vn, cache_kt, cache_vt, layer, n_new):
    nb = q.shape[0]
    w_buf = cache_kt.shape[-1]
    mh, mn = _sample_tables(w_buf, n_new)
    new_spec = pl.BlockSpec((1, N_HEADS, SAMPLE_ROWS, HEAD_DIM), lambda i: (i, 0, 0, 0))
    cache_spec = pl.BlockSpec((1, 1, N_HEADS, HEAD_DIM, w_buf), lambda i: (layer, i, 0, 0, 0))
    return pl.pallas_call(
        _sample_attn_kernel,
        grid=(nb,),
        in_specs=[new_spec, new_spec, new_spec, cache_spec, cache_spec,
                  pl.BlockSpec(mh.shape, lambda i: (0, 0)), pl.BlockSpec(mn.shape, lambda i: (0, 0))],
        out_specs=new_spec,
        out_shape=jax.ShapeDtypeStruct((nb, N_HEADS, SAMPLE_ROWS, HEAD_DIM), F32),
        compiler_params=_params(1), name="dilated_attn_sample",
    )(q, kn, vn, cache_kt, cache_vt, mh, mn)


def _sample_conv_kernel(uf_ref, cw_ref, cb_ref, lg_ref, lb_ref, o_ref, *, n_new, nb):
    rows = 32

    def body(n, carry):
        t = n // (nb // rows)
        base = pl.multiple_of((n % (nb // rows)) * rows, rows)
        acc = jnp.broadcast_to(cb_ref[...], (rows, 512))
        for j in range(CONV_WIDTH):
            acc = acc + cw_ref[j:j + 1, :] * uf_ref[t + j, pl.ds(base, rows), :]
        mu = jnp.mean(acc, axis=-1, keepdims=True)
        xc = acc - mu
        var = jnp.mean(xc * xc, axis=-1, keepdims=True)
        y = xc * lax.rsqrt(var + EPS) * lg_ref[...] + lb_ref[...]
        o_ref[t, pl.ds(base, rows), :] = _silu(y).astype(BF16)
        return carry

    lax.fori_loop(0, n_new * (nb // rows), body, 0)


def _sample_conv_call(ufull, conv_p):
    n_full, nb, ch = ufull.shape
    n_new = n_full - (CONV_WIDTH - 1)
    return pl.pallas_call(
        functools.partial(_sample_conv_kernel, n_new=n_new, nb=nb),
        out_shape=jax.ShapeDtypeStruct((n_new, nb, ch), BF16),
        compiler_params=pltpu.CompilerParams(vmem_limit_bytes=VMEM_LIMIT),
        name="conv_sample",
    )(ufull, *conv_p)


def _mix_residual(att_ref, cv_ref, x_ref, gt1_ref, sc2_ref, sh2_ref, gpm_ref, gpf_ref, wo_ref):
    a = jnp.concatenate([att_ref[0, hp] for hp in range(HEAD_PAIRS)] + [cv_ref[0]], axis=-1)
    mix = jnp.dot(a, wo_ref[...], preferred_element_type=F32)
    x1 = x_ref[0] + gt1_ref[0] * _rms(mix, gpm_ref[...])
    h2 = _rms(x1, gpf_ref[...]) * (1.0 + sc2_ref[0]) + sh2_ref[0]
    return x1, h2


def _ffn_kernel(att_ref, cv_ref, x_ref, gt1_ref, sc2_ref, sh2_ref, gt2_ref, gpm_ref, gpf_ref, gqf_ref,
                wo_ref, wg_ref, wu_ref, wd_ref, o_ref, x1_s, h2_s, acc_s):
    f = pl.program_id(1)

    @pl.when(f == 0)
    def _():
        x1, h2 = _mix_residual(att_ref, cv_ref, x_ref, gt1_ref, sc2_ref, sh2_ref, gpm_ref, gpf_ref, wo_ref)
        x1_s[...] = x1
        h2_s[...] = h2.astype(BF16)
        acc_s[...] = jnp.zeros_like(acc_s)

    h2 = h2_s[...]
    g = jnp.dot(h2, wg_ref[...].astype(BF16), preferred_element_type=F32)
    u = jnp.dot(h2, wu_ref[...].astype(BF16), preferred_element_type=F32)
    act = (_silu(g) * u).astype(BF16)
    acc_s[...] += jnp.dot(act, wd_ref[...].astype(BF16), preferred_element_type=F32)

    @pl.when(f == pl.num_programs(1) - 1)
    def _():
        o_ref[0] = x1_s[...] + gt2_ref[0] * _rms(acc_s[...], gqf_ref[...])


def _mix_in_specs(b, t, tm, mod_rows, mod_idx, d):
    ntb = t // tm
    row = lambda i, *_: (i // ntb, i % ntb, 0)
    specs = [pl.BlockSpec((1, HEAD_PAIRS, tm, 128), lambda i, *_: (i // ntb, 0, i % ntb, 0)),
             pl.BlockSpec((1, tm, 512), row),
             pl.BlockSpec((1, tm, d), row)]
    return specs, row


def _ffn_call(att, cv, x, mod, mod_rows, mod_idx, gpm, gpf, gqf, wo_bf, wg, wu, wd, *, tm, tf, name):
    b, t, d = x.shape
    ff = wg.shape[1]
    ntb = t // tm
    specs, row = _mix_in_specs(b, t, tm, mod_rows, mod_idx, d)
    vec = pl.BlockSpec((1, d), lambda i, f: (0, 0))
    in_specs = specs + [_mod_spec(mod_rows, 2, mod_idx), _mod_spec(mod_rows, 4, mod_idx),
                        _mod_spec(mod_rows, 3, mod_idx), _mod_spec(mod_rows, 5, mod_idx),
                        vec, vec, vec,
                        pl.BlockSpec((d, d), lambda i, f: (0, 0)),
                        pl.BlockSpec((d, tf), lambda i, f: (0, f)),
                        pl.BlockSpec((d, tf), lambda i, f: (0, f)),
                        pl.BlockSpec((tf, d), lambda i, f: (f, 0))]
    return pl.pallas_call(
        _ffn_kernel,
        grid=(b * ntb, ff // tf),
        in_specs=in_specs,
        out_specs=pl.BlockSpec((1, tm, d), row),
        out_shape=jax.ShapeDtypeStruct((b, t, d), F32),
        scratch_shapes=[pltpu.VMEM((tm, d), F32), pltpu.VMEM((tm, d), BF16), pltpu.VMEM((tm, d), F32)],
        compiler_params=_params(2), name=name,
    )(att, cv, x, mod, mod, mod, mod, gpm, gpf, gqf, wo_bf, wg, wu, wd)


def _router_kernel(att_ref, cv_ref, x_ref, gt1_ref, sc2_ref, sh2_ref, gpm_ref, gpf_ref, wo_ref, wr_ref,
                   x1_ref, h2_ref, gate_ref, sel_ref):
    x1, h2 = _mix_residual(att_ref, cv_ref, x_ref, gt1_ref, sc2_ref, sh2_ref, gpm_ref, gpf_ref, wo_ref)
    x1_ref[0] = x1
    _store_split(h2_ref.at[:, 0], h2)
    logits = jnp.dot(h2, wr_ref[...], preferred_element_type=F32, precision=lax.Precision.HIGHEST)
    lane = lax.broadcasted_iota(jnp.int32, logits.shape, 1).astype(F32)
    v1 = jnp.max(logits, axis=-1, keepdims=True)
    i1 = jnp.min(jnp.where(logits == v1, lane, float(N_EXPERTS)), axis=-1, keepdims=True)
    oh1 = lane == i1
    rest = jnp.where(oh1, -jnp.inf, logits)
    v2 = jnp.max(rest, axis=-1, keepdims=True)
    i2 = jnp.min(jnp.where(rest == v2, lane, float(N_EXPERTS)), axis=-1, keepdims=True)
    oh2 = lane == i2
    e2 = jnp.exp(v2 - v1)
    den = 1.0 + e2
    gate_ref[0] = jnp.where(oh1, 1.0 / den, 0.0) + jnp.where(oh2, e2 / den, 0.0)
    sel_ref[0] = jnp.where(oh1 | oh2, 1.0, 0.0)


def _router_call(att, cv, x, mod, mod_rows, mod_idx, gpm, gpf, wo_bf, wr, *, tm, name):
    b, t, d = x.shape
    specs, row = _mix_in_specs(b, t, tm, mod_rows, mod_idx, d)
    vec = pl.BlockSpec((1, d), lambda i: (0, 0))
    in_specs = specs + [_mod_spec(mod_rows, 2, mod_idx), _mod_spec(mod_rows, 4, mod_idx),
                        _mod_spec(mod_rows, 3, mod_idx), vec, vec,
                        pl.BlockSpec((d, d), lambda i: (0, 0)),
                        pl.BlockSpec((d, N_EXPERTS), lambda i: (0, 0))]
    return pl.pallas_call(
        _router_kernel,
        grid=(b * (t // tm),),
        in_specs=in_specs,
        out_specs=[pl.BlockSpec((1, tm, d), row),
                   pl.BlockSpec((2, 1, tm, d // 4), lambda i: (0,) + row(i)),
                   pl.BlockSpec((1, tm, N_EXPERTS), row), pl.BlockSpec((1, tm, N_EXPERTS), row)],
        out_shape=[jax.ShapeDtypeStruct((b, t, d), F32), jax.ShapeDtypeStruct((2, b, t, d // 4), jnp.int32),
                   jax.ShapeDtypeStruct((b, t, N_EXPERTS), F32), jax.ShapeDtypeStruct((b, t, N_EXPERTS), F32)],
        compiler_params=_params(1), name=name,
    )(att, cv, x, mod, mod, mod, gpm, gpf, wo_bf, wr)


MOE_TILE = 1024
RANK_BLOCK = 1536
GATHER_WINDOW = 128
GATHER_ROWS = 32 * GATHER_WINDOW
GATHER_DEPTH = 3


def _pack_pairs(x):
    w = x.shape[1] // 2
    lo = pltpu.bitcast(x[:, :w].astype(BF16).astype(F32), jnp.int32)
    hi = pltpu.bitcast(x[:, w:].astype(BF16).astype(F32), jnp.int32)
    return hi | lax.shift_right_logical(lo, 16)


def _unpack_pairs(words):
    lo = pltpu.bitcast(lax.shift_left(words, 16), F32)
    hi = pltpu.bitcast(words & -65536, F32)
    return jnp.concatenate([lo, hi], axis=1)


def _store_split(ref, x):
    words = _pack_pairs(x)
    q = words.shape[1] // 2
    ref[0] = words[:, :q]
    ref[1] = words[:, q:]


def _load_split(ref):
    return _unpack_pairs(jnp.concatenate([ref[0], ref[1]], axis=1))


def _rank_kernel(sel_ref, rank_ref, cnt_ref, carry):
    @pl.when(pl.program_id(0) == 0)
    def _():
        carry[...] = jnp.zeros_like(carry)

    sel = sel_ref[...]
    rb = sel.shape[0]
    before = (lax.broadcasted_iota(jnp.int32, (rb, rb), 1)
              < lax.broadcasted_iota(jnp.int32, (rb, rb), 0)).astype(BF16)
    rank_ref[...] = jnp.dot(before, sel.astype(BF16), preferred_element_type=F32) + carry[...]
    carry[...] += jnp.sum(sel, axis=0, keepdims=True)
    cnt_ref[...] = carry[...]


def _rank_call(sel):
    n, n_e = sel.shape
    rb = RANK_BLOCK
    assert n % rb == 0
    return pl.pallas_call(
        _rank_kernel,
        grid=(n // rb,),
        in_specs=[pl.BlockSpec((rb, n_e), lambda i: (i, 0))],
        out_specs=[pl.BlockSpec((rb, n_e), lambda i: (i, 0)), pl.BlockSpec((1, n_e), lambda i: (0, 0))],
        out_shape=[jax.ShapeDtypeStruct((n, n_e), F32), jax.ShapeDtypeStruct((1, n_e), F32)],
        scratch_shapes=[pltpu.VMEM((1, n_e), F32)],
        compiler_params=_params(1), name="moe_rank",
    )(sel)


def _sc_gather(table, idxs):
    halves, _, width = table.shape
    n_lists, n = len(idxs), idxs[0].shape[0]
    info = plsc.get_sparse_core_info()
    win, depth = GATHER_WINDOW, GATHER_DEPTH
    assert n % GATHER_ROWS == 0 and GATHER_ROWS == info.num_cores * info.num_subcores * win
    wins = n // GATHER_ROWS
    jobs = [(l, j, h) for l in range(n_lists) for j in range(wins) for h in range(halves)]
    mesh = plsc.VectorSubcoreMesh(core_axis_name="core", subcore_axis_name="subcore")
    out_type = [jax.ShapeDtypeStruct((halves, n, width), table.dtype) for _ in idxs]
    scratch = [pltpu.VMEM((n_lists, wins, win), jnp.int32), pltpu.VMEM((depth, win, width), table.dtype),
               pltpu.SemaphoreType.DMA((depth,)), pltpu.SemaphoreType.DMA((depth,))]

    @functools.partial(pl.kernel, out_type=out_type, mesh=mesh, scratch_types=scratch)
    def gather(table_hbm, idx_hbm, *refs):
        out_refs = refs[:n_lists]
        idx_v, buf, sem_in, sem_out = refs[n_lists:]
        worker = lax.axis_index("subcore") * info.num_cores + lax.axis_index("core")
        first = worker * wins
        for l in range(n_lists):
            pltpu.sync_copy(idx_hbm.at[l, worker], idx_v.at[l])
        for g in range(0, len(jobs), depth):
            group = jobs[g:g + depth]
            reads = [pltpu.async_copy(table_hbm.at[h].at[idx_v.at[l, j]], buf.at[k], sem_in.at[k])
                     for k, (l, j, h) in enumerate(group)]
            writes = []
            for k, (l, j, h) in enumerate(group):
                reads[k].wait()
                rows = pl.ds(pl.multiple_of((first + j) * win, win), win)
                writes.append(pltpu.async_copy(buf.at[k], out_refs[l].at[h, rows], sem_out.at[k]))
            for write in writes:
                write.wait()

    return gather(table, jnp.stack(idxs).reshape(n_lists, n // (wins * win), wins, win))


def _expert_kernel(te_ref, nu_ref, xs_ref, wg_ref, wu_ref, wd_ref, ys_ref, xb, acc):
    t, f = pl.program_id(0), pl.program_id(1)
    last = pl.num_programs(1) - 1
    used = t < nu_ref[0]

    @pl.when(used & (f == 0))
    def _():
        xb[...] = _load_split(xs_ref).astype(BF16)
        acc[...] = jnp.zeros_like(acc)

    @pl.when(used)
    def _():
        x = xb[...]
        g = jnp.dot(x, wg_ref[0].astype(BF16), preferred_element_type=F32)
        u = jnp.dot(x, wu_ref[0].astype(BF16), preferred_element_type=F32)
        act = (_silu(g) * u).astype(BF16)
        acc[...] += jnp.dot(act, wd_ref[0].astype(BF16), preferred_element_type=F32)

    @pl.when(used & (f == last))
    def _():
        _store_split(ys_ref, acc[...])

    @pl.when(jnp.logical_not(used) & (f == last))
    def _():
        ys_ref[...] = jnp.zeros_like(ys_ref)


def _expert_call(tile_expert, n_used, xs, wg, wu, wd, *, tf):
    _, rows, quarter = xs.shape
    n_e, d, ff = wg.shape
    tm = MOE_TILE
    nf = ff // tf
    chunk = lambda t, f, te, nu: jnp.where(t < nu[0], f, nf - 1)
    grid_spec = pltpu.PrefetchScalarGridSpec(
        num_scalar_prefetch=2,
        grid=(rows // tm, nf),
        in_specs=[pl.BlockSpec((2, tm, quarter), lambda t, f, te, nu: (0, t, 0)),
                  pl.BlockSpec((1, d, tf), lambda t, f, te, nu: (te[t], 0, chunk(t, f, te, nu))),
                  pl.BlockSpec((1, d, tf), lambda t, f, te, nu: (te[t], 0, chunk(t, f, te, nu))),
                  pl.BlockSpec((1, tf, d), lambda t, f, te, nu: (te[t], chunk(t, f, te, nu), 0))],
        out_specs=pl.BlockSpec((2, tm, quarter), lambda t, f, te, nu: (0, t, 0)),
        scratch_shapes=[pltpu.VMEM((tm, d), BF16), pltpu.VMEM((tm, d), F32)])
    return pl.pallas_call(
        _expert_kernel, grid_spec=grid_spec,
        out_shape=jax.ShapeDtypeStruct((2, rows, quarter), jnp.int32),
        compiler_params=_params(2), name="moe_experts",
    )(tile_expert, n_used, xs, wg, wu, wd)


def _moe_route(h2p, sel, gates, wg, wu, wd, *, tf):
    n, n_e = sel.shape
    tm = MOE_TILE
    rank, cnt = _rank_call(sel)
    counts = cnt[0].astype(jnp.int32)
    padded = (counts + tm - 1) // tm * tm
    seg_end = jnp.cumsum(padded)
    seg_start = seg_end - padded
    rows_max = -(-(2 * n + n_e * tm) // GATHER_ROWS) * GATHER_ROWS
    assert rows_max % tm == 0
    lanes = jnp.arange(n_e, dtype=jnp.int32)[None, :]
    e_lo = jnp.min(jnp.where(sel > 0, lanes, n_e - 1), axis=1)
    e_hi = jnp.max(jnp.where(sel > 0, lanes, 0), axis=1)
    pick = lambda a, e: jnp.take_along_axis(a, e[:, None], axis=1)[:, 0]
    rank_i = rank.astype(jnp.int32)
    d_lo = seg_start[e_lo] + pick(rank_i, e_lo)
    d_hi = seg_start[e_hi] + pick(rank_i, e_hi)
    w = jnp.stack([pick(gates, e_lo), pick(gates, e_hi)], axis=1)
    tok = jnp.arange(n, dtype=jnp.int32)
    src = jnp.zeros((rows_max,), jnp.int32).at[d_lo].set(tok).at[d_hi].set(tok)
    n_tiles = rows_max // tm
    n_used = (seg_end[-1] // tm).astype(jnp.int32)
    tile_ids = jnp.minimum(jnp.arange(n_tiles, dtype=jnp.int32), n_used - 1)
    tile_expert = jnp.minimum(jnp.searchsorted(seg_end, tile_ids * tm, side="right"), n_e - 1).astype(jnp.int32)

    xs, = _sc_gather(h2p, [src])
    ys = _expert_call(tile_expert, n_used.reshape(1), xs, wg, wu, wd, tf=tf)
    n_pad = -(-n // GATHER_ROWS) * GATHER_ROWS
    pad_idx = lambda dd: jnp.pad(dd.astype(jnp.int32), (0, n_pad - n))
    ya, yb = _sc_gather(ys, [pad_idx(d_lo), pad_idx(d_hi)])
    return ya, yb, w


def _combine_kernel(ya_ref, yb_ref, w_ref, x1_ref, gt2_ref, gqf_ref, o_ref):
    w = w_ref[...]
    f = w[:, 0:1] * _load_split(ya_ref) + w[:, 1:2] * _load_split(yb_ref)
    o_ref[0] = x1_ref[0] + gt2_ref[0] * _rms(f, gqf_ref[...])


def _combine_call(ya, yb, w, x1, row0, mod, mod_rows, mod_idx, gqf, *, tm, name):
    b, t, d = x1.shape
    ntb = t // tm
    blk0 = row0 // tm
    assert row0 % tm == 0
    pool = lambda i: (blk0 + i, 0)
    pool3 = lambda i: (0, blk0 + i, 0)
    row = lambda i: (i // ntb, i % ntb, 0)
    return pl.pallas_call(
        _combine_kernel,
        grid=(b * ntb,),
        in_specs=[pl.BlockSpec((2, tm, d // 4), pool3), pl.BlockSpec((2, tm, d // 4), pool3),
                  pl.BlockSpec((tm, 2), pool), pl.BlockSpec((1, tm, d), row),
                  _mod_spec(mod_rows, 5, mod_idx), pl.BlockSpec((1, d), lambda i: (0, 0))],
        out_specs=pl.BlockSpec((1, tm, d), row),
        out_shape=jax.ShapeDtypeStruct((b, t, d), F32),
        compiler_params=_params(1), name=name,
    )(ya, yb, w, x1, mod, gqf)


def _rope_tables(pos):
    half = ROT_DIM // 2
    inv_freq = ROPE_THETA ** (-jnp.arange(half, dtype=F32) * 2.0 / ROT_DIM)
    ang = pos.astype(F32)[:, None] * inv_freq[None, :]
    cos, sin = jnp.cos(ang), jnp.sin(ang)
    l64 = np.arange(128) % HEAD_DIM
    idx = l64 % half
    first = (l64 < half)[None, :]
    second = ((l64 >= half) & (l64 < ROT_DIM))[None, :]
    cos_t = jnp.where(first | second, cos[:, idx], 1.0)
    s1_t = jnp.where(first, -sin[:, idx], 0.0)
    s2_t = jnp.where(second, sin[:, idx], 0.0)
    return cos_t, s1_t, s2_t


def kernel(x_prompt, x_sample, cache_k, cache_v, state_conv, c_prompt, c_sample, w_mod, b_mod, g_pre_mix, g_post_mix, g_pre_ffn, g_post_ffn, w_in, conv_w, conv_b, conv_ln_g, conv_ln_b, w_out, ffn_w_gate, ffn_w_up, ffn_w_down, moe_w_router, moe_w_gate, moe_w_up, moe_w_down):
    bp, seq, d = x_prompt.shape
    nb, n_new, _ = x_sample.shape
    depth = w_mod.shape[0]
    w_buf = cache_k.shape[2]
    past_len = PAST_LEN
    assert w_buf == min(DILATION_PATTERNS[-1][0], past_len)
    keep = min(DILATION_PATTERNS[-1][0], seq)
    ns = nb * n_new

    c_all = jnp.concatenate([c_sample, c_prompt, jnp.zeros((MOD_ROWS - nb - bp, d), F32)], axis=0)
    mod_all = _mod_call(c_all, w_mod, b_mod)

    tm_p = 512
    tabs_p = tuple(tb.reshape(seq // tm_p, tm_p, 128) for tb in _rope_tables(jnp.arange(seq, dtype=jnp.int32)))
    tabs_s = tuple(tb.reshape(n_new, 1, 128)
                   for tb in _rope_tables(past_len + jnp.arange(n_new, dtype=jnp.int32)))

    cache_kt = cache_k.transpose(0, 1, 3, 4, 2)
    cache_vt = cache_v.transpose(0, 1, 3, 4, 2)

    yp = x_prompt
    ys = x_sample.transpose(1, 0, 2).reshape(1, ns, d)
    outs = [[] for _ in range(6)]
    for l in range(depth):
        mod_p = mod_all[l].reshape(MOD_ROWS, 1, 6 * d)
        mod_s = jnp.tile(mod_all[l, :nb], (n_new, 1))[None]
        idx_p2 = lambda bi, i: (nb + bi, 0)
        idx_s2 = lambda bi, i: (0, i)
        w_in_bf = w_in[l].astype(BF16)
        w_out_bf = w_out[l].astype(BF16)
        conv_p = (jnp.pad(conv_w[l], ((0, 1), (0, 0))), conv_b[l][None], conv_ln_g[l][None], conv_ln_b[l][None])
        gpre, gpm, gpf, gqf = g_pre_mix[l][None], g_post_mix[l][None], g_pre_ffn[l][None], g_post_ffn[l][None]

        q, k, v, kt, vt, cv, ul = _inproj_call(yp, mod_p, 1, idx_p2, gpre, w_in_bf, tabs_p, tm_p, conv_p,
                                               tm=tm_p, keep=keep, name=f"inproj_prompt_{l}")
        att = _attn_call(q, k, v)
        outs[0].append(kt.reshape(bp, keep, N_HEADS, HEAD_DIM))
        outs[1].append(vt.reshape(bp, keep, N_HEADS, HEAD_DIM))
        outs[2].append(ul[:, CONV_HALO - (CONV_WIDTH - 1):])

        qs, _, _, kts, vts, us = _inproj_call(ys, mod_s, nb, idx_s2, gpre, w_in_bf, tabs_s, 1, None,
                                              tm=nb, keep=ns, name=f"inproj_sample_{l}")
        to_batch_major = lambda z: z.reshape(n_new, nb, N_HEADS, HEAD_DIM).transpose(1, 0, 2, 3)
        q_bm = (qs.reshape(HEAD_PAIRS, n_new, nb, 2, HEAD_DIM).transpose(2, 1, 0, 3, 4)
                .reshape(nb, n_new, N_HEADS, HEAD_DIM))
        k_bm, v_bm = to_batch_major(kts), to_batch_major(vts)
        head_major = lambda z: jnp.pad(z.transpose(0, 2, 1, 3), ((0, 0), (0, 0), (0, SAMPLE_ROWS - n_new), (0, 0)))
        att_s = _sample_attn_call(head_major(q_bm), head_major(k_bm), head_major(v_bm),
                                  cache_kt, cache_vt, l, n_new)
        att_s = (att_s[:, :, :n_new].reshape(nb, HEAD_PAIRS, 2, n_new, HEAD_DIM).transpose(1, 3, 0, 2, 4)
                 .reshape(1, HEAD_PAIRS, ns, 128).astype(BF16))
        ufull = jnp.concatenate([state_conv[l].transpose(1, 0, 2), us.reshape(n_new, nb, 512)], axis=0)
        cv_s = _sample_conv_call(ufull, conv_p).reshape(1, ns, 512)
        outs[3].append(k_bm)
        outs[4].append(v_bm)
        outs[5].append(ufull[n_new:].transpose(1, 0, 2))

        idx_p1 = lambda i, *_: (nb + i // (seq // tm_f), 0)
        idx_s1 = lambda i, *_: (0, i)
        if l % 2 == 0:
            tm_f = 1024
            wg, wu, wd = ffn_w_gate[l // 2], ffn_w_up[l // 2], ffn_w_down[l // 2]
            yp = _ffn_call(att, cv, yp, mod_p, 1, idx_p1, gpm, gpf, gqf, w_out_bf, wg, wu, wd,
                           tm=tm_f, tf=256, name=f"ffn_prompt_{l}")
            ys = _ffn_call(att_s, cv_s, ys, mod_s, ns, idx_s1, gpm, gpf, gqf, w_out_bf, wg, wu, wd,
                           tm=ns, tf=512, name=f"ffn_sample_{l}")
        else:
            tm_f = 1024
            wr = moe_w_router[l // 2]
            wg, wu, wd = moe_w_gate[l // 2], moe_w_up[l // 2], moe_w_down[l // 2]
            x1, h2, gates, sel = _router_call(att, cv, yp, mod_p, 1, idx_p1, gpm, gpf, w_out_bf, wr,
                                              tm=tm_f, name=f"router_prompt_{l}")
            x1s, h2s, gates_s, sel_s = _router_call(att_s, cv_s, ys, mod_s, ns, idx_s1, gpm, gpf, w_out_bf, wr,
                                                    tm=ns, name=f"router_sample_{l}")
            pool = lambda a, b_: jnp.concatenate([a.reshape(bp * seq, -1), b_.reshape(ns, -1)], axis=0)
            h2_pool = jnp.concatenate([h2.reshape(2, bp * seq, d // 4), h2s.reshape(2, ns, d // 4)], axis=1)
            ya, yb, w12 = _moe_route(h2_pool, pool(sel, sel_s), pool(gates, gates_s), wg, wu, wd, tf=512)
            yp = _combine_call(ya, yb, w12, x1, 0, mod_p, 1, idx_p1, gqf, tm=tm_f, name=f"moe_combine_prompt_{l}")
            ys = _combine_call(ya, yb, w12, x1s, bp * seq, mod_s, ns, idx_s1, gqf, tm=ns,
                               name=f"moe_combine_sample_{l}")

    y_sample = ys.reshape(n_new, nb, d).transpose(1, 0, 2)
    return (yp, y_sample) + tuple(jnp.stack(o) for o in outs)
```

```python
import functools

import numpy as np
import jax
import jax.numpy as jnp
from jax import lax
from jax.experimental import pallas as pl
from jax.experimental.pallas import tpu as pltpu
from jax.experimental.pallas import tpu_sc as plsc

F32 = jnp.float32
BF16 = jnp.bfloat16

HEAD_DIM = 64
N_HEADS = 8
ATT_WIDTH = N_HEADS * HEAD_DIM
HEAD_PAIRS = ATT_WIDTH // 128
CONV_WIDTH = 31
CONV_HALO = 32
DILATION_PATTERNS = ((128, 1), (512, 4), (2048, 16))
BAND = 128
ROT_DIM = HEAD_DIM // 4
ROPE_THETA = 500000.0
ATTN_SCALE = HEAD_DIM ** -0.5
N_EXPERTS = 8
PAST_LEN = 2048
EPS = 1e-6
NEG = -1e30
MOD_ROWS = 136
VMEM_LIMIT = 56 * 1024 * 1024


def _rms(x, g):
    return x * lax.rsqrt(jnp.mean(x * x, axis=-1, keepdims=True) + EPS) * g


def _silu(x):
    return x * jax.nn.sigmoid(x)


def _params(n_axes, vmem=VMEM_LIMIT):
    return pltpu.CompilerParams(dimension_semantics=("arbitrary",) * n_axes, vmem_limit_bytes=vmem)


def _mod_kernel(c_ref, w_ref, b_ref, o_ref):
    a = _silu(c_ref[...]).astype(BF16)
    o_ref[0] = jnp.dot(a, w_ref[0].astype(BF16), preferred_element_type=F32) + b_ref[0]


def _mod_call(c_all, w_mod, b_mod):
    depth, d, d6 = w_mod.shape
    return pl.pallas_call(
        _mod_kernel,
        grid=(depth, d6 // d),
        in_specs=[pl.BlockSpec((MOD_ROWS, d), lambda l, j: (0, 0)),
                  pl.BlockSpec((1, d, d), lambda l, j: (l, 0, j)),
                  pl.BlockSpec((1, 1, d), lambda l, j: (l, 0, j))],
        out_specs=pl.BlockSpec((1, MOD_ROWS, d), lambda l, j: (l, 0, j)),
        out_shape=jax.ShapeDtypeStruct((depth, MOD_ROWS, d6), F32),
        compiler_params=_params(2),
        name="adaln_mod",
    )(c_all, w_mod, b_mod.reshape(depth, 1, d6))


def _inproj_kernel(*refs, conv, tm):
    if conv:
        (x_ref, sh_ref, sc_ref, g_ref, w_ref, cos_ref, s1_ref, s2_ref,
         cw_ref, cb_ref, lg_ref, lb_ref,
         q_ref, k_ref, v_ref, kt_ref, vt_ref, cv_ref, ul_ref, ubuf) = refs
    else:
        (x_ref, sh_ref, sc_ref, g_ref, w_ref, cos_ref, s1_ref, s2_ref,
         q_ref, k_ref, v_ref, kt_ref, vt_ref, u_ref) = refs

    h = _rms(x_ref[0], g_ref[...]) * (1.0 + sc_ref[0]) + sh_ref[0]
    proj = jnp.dot(h.astype(BF16), w_ref[...], preferred_element_type=F32)
    cos, s1, s2 = cos_ref[0], s1_ref[0], s2_ref[0]

    def rope(z):
        return z * cos + pltpu.roll(z, 128 - ROT_DIM // 2, 1) * s1 + pltpu.roll(z, ROT_DIM // 2, 1) * s2

    for hp in range(HEAD_PAIRS):
        lo, hi = hp * 128, (hp + 1) * 128
        q_ref[0, hp] = rope(proj[:, lo:hi]) * ATTN_SCALE
        kz = rope(proj[:, ATT_WIDTH + lo:ATT_WIDTH + hi])
        k_ref[0, hp] = kz
        kt_ref[0, :, lo:hi] = kz
        vz = proj[:, 2 * ATT_WIDTH + lo:2 * ATT_WIDTH + hi]
        v_ref[0, hp] = vz
        vt_ref[0, :, lo:hi] = vz

    a = proj[:, 3 * ATT_WIDTH:3 * ATT_WIDTH + 512]
    gate = proj[:, 3 * ATT_WIDTH + 512:]
    u = a * jax.nn.sigmoid(gate)
    if not conv:
        u_ref[0] = u
        return

    @pl.when(pl.program_id(1) == 0)
    def _():
        ubuf[0, 0:CONV_HALO, :] = jnp.zeros((CONV_HALO, 512), F32)

    ubuf[0, CONV_HALO:CONV_HALO + tm, :] = u
    span = tm + CONV_HALO - 8
    for s in range(1, 8):
        ubuf[s, 0:span, :] = ubuf[0, s:s + span, :]
    off0 = CONV_HALO - (CONV_WIDTH - 1)
    rows = 32

    def chunk(r, carry):
        base = pl.multiple_of(r * rows, rows)
        acc = jnp.broadcast_to(cb_ref[...], (rows, 512))
        for j in range(CONV_WIDTH):
            a, s = divmod(off0 + j, 8)
            acc = acc + cw_ref[j:j + 1, :] * ubuf[s, pl.ds(base + 8 * a, rows), :]
        mu = jnp.mean(acc, axis=-1, keepdims=True)
        xc = acc - mu
        var = jnp.mean(xc * xc, axis=-1, keepdims=True)
        y = xc * lax.rsqrt(var + EPS) * lg_ref[...] + lb_ref[...]
        cv_ref[0, pl.ds(base, rows), :] = _silu(y).astype(BF16)
        return carry

    lax.fori_loop(0, tm // rows, chunk, 0, unroll=4)
    tail = ubuf[0, tm:tm + CONV_HALO, :]
    ul_ref[0] = tail
    ubuf[0, 0:CONV_HALO, :] = tail


def _mod_spec(rows, chunk, index_fn):
    return pl.BlockSpec((1, rows, 1024), lambda *g: index_fn(*g) + (chunk,))


def _inproj_call(x, mod, mod_rows, mod_idx, g, w_bf, tabs, tab_rows, conv_p, *, tm, keep, name):
    b, t, d = x.shape
    nt = t // tm
    off = (t - keep) // tm
    conv = conv_p is not None
    tab_spec = pl.BlockSpec((1, tab_rows, 128), lambda bi, i: (i, 0, 0))
    in_specs = [pl.BlockSpec((1, tm, d), lambda bi, i: (bi, i, 0)),
                _mod_spec(mod_rows, 0, mod_idx), _mod_spec(mod_rows, 1, mod_idx),
                pl.BlockSpec((1, d), lambda bi, i: (0, 0)),
                pl.BlockSpec(w_bf.shape, lambda bi, i: (0, 0)),
                tab_spec, tab_spec, tab_spec]
    args = [x, mod, mod, g, w_bf, *tabs]
    hp_spec = pl.BlockSpec((1, HEAD_PAIRS, tm, 128), lambda bi, i: (bi, 0, i, 0))
    tail_spec = pl.BlockSpec((1, tm, 512), lambda bi, i: (bi, jnp.maximum(i - off, 0), 0))
    hp_shape = jax.ShapeDtypeStruct((b, HEAD_PAIRS, t, 128), F32)
    tail_shape = jax.ShapeDtypeStruct((b, keep, 512), F32)
    out_specs = [hp_spec, hp_spec, hp_spec, tail_spec, tail_spec]
    out_shape = [hp_shape, hp_shape, hp_shape, tail_shape, tail_shape]
    scratch = []
    if conv:
        small = pl.BlockSpec((1, 512), lambda bi, i: (0, 0))
        in_specs += [pl.BlockSpec((32, 512), lambda bi, i: (0, 0)), small, small, small]
        args += list(conv_p)
        out_specs += [pl.BlockSpec((1, tm, 512), lambda bi, i: (bi, i, 0)),
                      pl.BlockSpec((1, CONV_HALO, 512), lambda bi, i: (bi, 0, 0))]
        out_shape += [jax.ShapeDtypeStruct((b, t, 512), BF16),
                      jax.ShapeDtypeStruct((b, CONV_HALO, 512), F32)]
        scratch = [pltpu.VMEM((8, tm + CONV_HALO, 512), F32)]
    else:
        out_specs += [pl.BlockSpec((1, tm, 512), lambda bi, i: (bi, i, 0))]
        out_shape += [jax.ShapeDtypeStruct((b, t, 512), F32)]
    return pl.pallas_call(
        functools.partial(_inproj_kernel, conv=conv, tm=tm),
        grid=(b, nt), in_specs=in_specs, out_specs=out_specs, out_shape=out_shape,
        scratch_shapes=scratch, compiler_params=_params(2), name=name,
    )(*args)


def _attn_kernel(q_ref, kc_ref, kp_ref, vc_ref, vp_ref, o_ref, acc_s, m_s, l_s, *, tq):
    first_kj = jnp.where(pl.program_id(2) == 0, 128, 0)
    lane_lo = lax.broadcasted_iota(jnp.int32, (128, 128), 1) < HEAD_DIM
    qi = lax.broadcasted_iota(jnp.int32, (256, 256), 0)
    qi = jnp.where(qi >= 128, qi - 128, qi)
    kj = lax.broadcasted_iota(jnp.int32, (256, 256), 1)
    band = (kj >= qi) & (kj <= qi + BAND)
    band_first = band & (kj >= first_kj)

    ones_cols = jnp.ones((256, 128), BF16)
    nt = (((1,), (1,)), ((), ()))

    def strided(start, size, d):
        if d > 1:
            return pl.ds(start, size, stride=d)
        return pl.ds(start if isinstance(start, int) else pl.multiple_of(start, 128), size)

    def blocks(items):
        loaded = []
        for rows, first, c, d, s in items:
            if first:
                take = lambda rp, rc: jnp.concatenate(
                    [rp[0, 0, strided(tq - BAND * d + c, 128, d), :],
                     rc[0, 0, strided(c, 128, d), :]], axis=0)
            else:
                krows = strided(c + d * 128 * (s - 1), 256, d)
                take = lambda rp, rc: rc[0, 0, krows, :]
            loaded.append((q_ref[0, 0, rows, :], take(kp_ref, kc_ref), take(vp_ref, vc_ref),
                           band_first if first else band))
        scores = []
        for q2, k2, _, _ in loaded:
            qq = jnp.concatenate([jnp.where(lane_lo, q2, 0.0), jnp.where(lane_lo, 0.0, q2)], axis=0)
            scores.append(lax.dot_general(qq.astype(BF16), k2.astype(BF16), nt,
                                          preferred_element_type=F32))
        probs = []
        for sc, (_, _, _, mask) in zip(scores, loaded):
            sc = jnp.where(mask, sc, NEG)
            mb = jnp.max(sc, axis=1, keepdims=True)
            probs.append((mb, jnp.exp(sc - mb).astype(BF16)))
        out = []
        for (mb, p), (_, _, v2, _), item in zip(probs, loaded, items):
            pv = jnp.dot(p, jnp.concatenate([v2.astype(BF16), ones_cols], axis=1), preferred_element_type=F32)
            out.append((item[0], jnp.where(lane_lo, mb[:128], mb[128:]),
                        jnp.where(lane_lo, pv[:128, 128:], pv[128:, 128:]),
                        jnp.where(lane_lo, pv[:128, :128], pv[128:, :128])))
        return out

    def merge(rows, mt, lt, pvt):
        mp, lp, ap = m_s[rows, :], l_s[rows, :], acc_s[rows, :]
        mn = jnp.maximum(mp, mt)
        a = jnp.exp(mp - mn)
        b = jnp.exp(mt - mn)
        return mn, a * lp + b * lt, a * ap + b * pvt

    def store_init(results):
        for rows, mt, lt, pvt in results:
            m_s[rows, :] = mt
            l_s[rows, :] = lt
            acc_s[rows, :] = pvt

    def store_merged(results):
        for rows, mt, lt, pvt in results:
            mn, ln, an = merge(rows, mt, lt, pvt)
            m_s[rows, :] = mn
            l_s[rows, :] = ln
            acc_s[rows, :] = an

    def store_output(results):
        for rows, mt, lt, pvt in results:
            _, ln, an = merge(rows, mt, lt, pvt)
            o_ref[0, 0, rows, :] = (an / ln).astype(o_ref.dtype)

    def loop(n, fn):
        def body(i, carry):
            fn(i)
            return carry
        lax.fori_loop(0, n, body, 0)

    group = 4
    loop(16 // group, lambda g: store_init(blocks(
        [(pl.ds(g * group + j, 128, stride=16), True, g * group + j, 16, 0) for j in range(group)])))

    store_merged(blocks([(pl.ds(c, 128, stride=4), True, c, 4, 0) for c in range(4)]))
    loop(tq // 512 - 1, lambda n: store_merged(blocks(
        [(pl.ds(c + 512 * (n + 1), 128, stride=4), False, c, 4, n + 1) for c in range(4)])))

    store_output(blocks([(strided(0, 128, 1), True, 0, 1, 0)]
                        + [(strided(128 * s, 128, 1), False, 0, 1, s) for s in range(1, group)]))
    rest = 3
    assert (tq // 128 - group) % rest == 0
    loop((tq // 128 - group) // rest, lambda g: store_output(blocks(
        [(strided(128 * (group + g * rest + j), 128, 1), False, 0, 1, group + g * rest + j) for j in range(rest)])))


def _attn_call(q, k, v, *, tq=2048):
    b, hp, t, _ = q.shape
    assert tq == BAND * 16 and t % tq == 0
    cur = pl.BlockSpec((1, 1, tq, 128), lambda bi, h, i: (bi, h, i, 0))
    prev = pl.BlockSpec((1, 1, tq, 128), lambda bi, h, i: (bi, h, jnp.maximum(i - 1, 0), 0))
    return pl.pallas_call(
        functools.partial(_attn_kernel, tq=tq),
        grid=(b, hp, t // tq),
        in_specs=[cur, cur, prev, cur, prev],
        out_specs=cur,
        out_shape=jax.ShapeDtypeStruct((b, hp, t, 128), BF16),
        scratch_shapes=[pltpu.VMEM((tq, 128), F32)] * 3,
        compiler_params=_params(3), name="dilated_attn_prompt",
    )(q, k, k, v, v)


SAMPLE_ROWS = 8


def _sample_attn_kernel(q_ref, kn_ref, vn_ref, kt_ref, vt_ref, mh_ref, mn_ref, o_ref):
    mh, mn = mh_ref[...], mn_ref[...]
    nt = (((1,), (1,)), ((), ()))
    for h in range(N_HEADS):
        qh = q_ref[0, h].astype(BF16)
        s = jnp.dot(qh, kt_ref[0, 0, h].astype(BF16), preferred_element_type=F32)
        sn = lax.dot_general(qh, kn_ref[0, h].astype(BF16), nt, preferred_element_type=F32)
        s = jnp.where(mh > 0, s, NEG)
        sn = jnp.where(mn > 0, sn, NEG)
        m = jnp.maximum(jnp.max(s, axis=1, keepdims=True), jnp.max(sn, axis=1, keepdims=True))
        p = mh * jnp.exp(s - m)
        pn = mn * jnp.exp(sn - m)
        den = jnp.sum(p, axis=1, keepdims=True) + jnp.sum(pn, axis=1, keepdims=True)
        num = (lax.dot_general(p.astype(BF16), vt_ref[0, 0, h].astype(BF16), nt, preferred_element_type=F32)
               + jnp.dot(pn.astype(BF16), vn_ref[0, h].astype(BF16), preferred_element_type=F32))
        o_ref[0, h] = num / jnp.where(den > 0, den, 1.0)


def _sample_tables(w_buf, n_new):
    rows = np.arange(w_buf + n_new)
    mult = np.zeros((SAMPLE_ROWS, w_buf + SAMPLE_ROWS), np.float32)
    for t in range(n_new):
        dist = w_buf + t - rows
        for window, dil in DILATION_PATTERNS:
            mult[t, :w_buf + n_new] += (dist >= 0) & (dist <= window) & (dist % dil == 0)
    return jnp.asarray(mult[:, :w_buf]), jnp.asarray(mult[:, w_buf:])


def _sample_attn_call(q, kn, vn, cache_kt, cache_vt, layer, n_new):
    nb = q.shape[0]
    w_buf = cache_kt.shape[-1]
    mh, mn = _sample_tables(w_buf, n_new)
    new_spec = pl.BlockSpec((1, N_HEADS, SAMPLE_ROWS, HEAD_DIM), lambda i: (i, 0, 0, 0))
    cache_spec = pl.BlockSpec((1, 1, N_HEADS, HEAD_DIM, w_buf), lambda i: (layer, i, 0, 0, 0))
    return pl.pallas_call(
        _sample_attn_kernel,
        grid=(nb,),
        in_specs=[new_spec, new_spec, new_spec, cache_spec, cache_spec,
                  pl.BlockSpec(mh.shape, lambda i: (0, 0)), pl.BlockSpec(mn.shape, lambda i: (0, 0))],
        out_specs=new_spec,
        out_shape=jax.ShapeDtypeStruct((nb, N_HEADS, SAMPLE_ROWS, HEAD_DIM), F32),
        compiler_params=_params(1), name="dilated_attn_sample",
    )(q, kn, vn, cache_kt, cache_vt, mh, mn)


def _sample_conv_kernel(uf_ref, cw_ref, cb_ref, lg_ref, lb_ref, o_ref, *, n_new, nb):
    rows = 32

    def body(n, carry):
        t = n // (nb // rows)
        base = pl.multiple_of((n % (nb // rows)) * rows, rows)
        acc = jnp.broadcast_to(cb_ref[...], (rows, 512))
        for j in range(CONV_WIDTH):
            acc = acc + cw_ref[j:j + 1, :] * uf_ref[t + j, pl.ds(base, rows), :]
        mu = jnp.mean(acc, axis=-1, keepdims=True)
        xc = acc - mu
        var = jnp.mean(xc * xc, axis=-1, keepdims=True)
        y = xc * lax.rsqrt(var + EPS) * lg_ref[...] + lb_ref[...]
        o_ref[t, pl.ds(base, rows), :] = _silu(y).astype(BF16)
        return carry

    lax.fori_loop(0, n_new * (nb // rows), body, 0)


def _sample_conv_call(ufull, conv_p):
    n_full, nb, ch = ufull.shape
    n_new = n_full - (CONV_WIDTH - 1)
    return pl.pallas_call(
        functools.partial(_sample_conv_kernel, n_new=n_new, nb=nb),
        out_shape=jax.ShapeDtypeStruct((n_new, nb, ch), BF16),
        compiler_params=pltpu.CompilerParams(vmem_limit_bytes=VMEM_LIMIT),
        name="conv_sample",
    )(ufull, *conv_p)


def _mix_residual(att_ref, cv_ref, x_ref, gt1_ref, sc2_ref, sh2_ref, gpm_ref, gpf_ref, wo_ref):
    a = jnp.concatenate([att_ref[0, hp] for hp in range(HEAD_PAIRS)] + [cv_ref[0]], axis=-1)
    mix = jnp.dot(a, wo_ref[...], preferred_element_type=F32)
    x1 = x_ref[0] + gt1_ref[0] * _rms(mix, gpm_ref[...])
    h2 = _rms(x1, gpf_ref[...]) * (1.0 + sc2_ref[0]) + sh2_ref[0]
    return x1, h2


def _ffn_kernel(att_ref, cv_ref, x_ref, gt1_ref, sc2_ref, sh2_ref, gt2_ref, gpm_ref, gpf_ref, gqf_ref,
                wo_ref, wg_ref, wu_ref, wd_ref, o_ref, x1_s, h2_s, acc_s):
    f = pl.program_id(1)

    @pl.when(f == 0)
    def _():
        x1, h2 = _mix_residual(att_ref, cv_ref, x_ref, gt1_ref, sc2_ref, sh2_ref, gpm_ref, gpf_ref, wo_ref)
        x1_s[...] = x1
        h2_s[...] = h2.astype(BF16)
        acc_s[...] = jnp.zeros_like(acc_s)

    h2 = h2_s[...]
    g = jnp.dot(h2, wg_ref[...].astype(BF16), preferred_element_type=F32)
    u = jnp.dot(h2, wu_ref[...].astype(BF16), preferred_element_type=F32)
    act = (_silu(g) * u).astype(BF16)
    acc_s[...] += jnp.dot(act, wd_ref[...].astype(BF16), preferred_element_type=F32)

    @pl.when(f == pl.num_programs(1) - 1)
    def _():
        o_ref[0] = x1_s[...] + gt2_ref[0] * _rms(acc_s[...], gqf_ref[...])


def _mix_in_specs(b, t, tm, mod_rows, mod_idx, d):
    ntb = t // tm
    row = lambda i, *_: (i // ntb, i % ntb, 0)
    specs = [pl.BlockSpec((1, HEAD_PAIRS, tm, 128), lambda i, *_: (i // ntb, 0, i % ntb, 0)),
             pl.BlockSpec((1, tm, 512), row),
             pl.BlockSpec((1, tm, d), row)]
    return specs, row


def _ffn_call(att, cv, x, mod, mod_rows, mod_idx, gpm, gpf, gqf, wo_bf, wg, wu, wd, *, tm, tf, name):
    b, t, d = x.shape
    ff = wg.shape[1]
    ntb = t // tm
    specs, row = _mix_in_specs(b, t, tm, mod_rows, mod_idx, d)
    vec = pl.BlockSpec((1, d), lambda i, f: (0, 0))
    in_specs = specs + [_mod_spec(mod_rows, 2, mod_idx), _mod_spec(mod_rows, 4, mod_idx),
                        _mod_spec(mod_rows, 3, mod_idx), _mod_spec(mod_rows, 5, mod_idx),
                        vec, vec, vec,
                        pl.BlockSpec((d, d), lambda i, f: (0, 0)),
                        pl.BlockSpec((d, tf), lambda i, f: (0, f)),
                        pl.BlockSpec((d, tf), lambda i, f: (0, f)),
                        pl.BlockSpec((tf, d), lambda i, f: (f, 0))]
    return pl.pallas_call(
        _ffn_kernel,
        grid=(b * ntb, ff // tf),
        in_specs=in_specs,
        out_specs=pl.BlockSpec((1, tm, d), row),
        out_shape=jax.ShapeDtypeStruct((b, t, d), F32),
        scratch_shapes=[pltpu.VMEM((tm, d), F32), pltpu.VMEM((tm, d), BF16), pltpu.VMEM((tm, d), F32)],
        compiler_params=_params(2), name=name,
    )(att, cv, x, mod, mod, mod, mod, gpm, gpf, gqf, wo_bf, wg, wu, wd)


def _router_kernel(att_ref, cv_ref, x_ref, gt1_ref, sc2_ref, sh2_ref, gpm_ref, gpf_ref, wo_ref, wr_ref,
                   x1_ref, h2_ref, gate_ref, sel_ref):
    x1, h2 = _mix_residual(att_ref, cv_ref, x_ref, gt1_ref, sc2_ref, sh2_ref, gpm_ref, gpf_ref, wo_ref)
    x1_ref[0] = x1
    _store_split(h2_ref.at[:, 0], h2)
    logits = jnp.dot(h2, wr_ref[...], preferred_element_type=F32, precision=lax.Precision.HIGHEST)
    lane = lax.broadcasted_iota(jnp.int32, logits.shape, 1).astype(F32)
    v1 = jnp.max(logits, axis=-1, keepdims=True)
    i1 = jnp.min(jnp.where(logits == v1, lane, float(N_EXPERTS)), axis=-1, keepdims=True)
    oh1 = lane == i1
    rest = jnp.where(oh1, -jnp.inf, logits)
    v2 = jnp.max(rest, axis=-1, keepdims=True)
    i2 = jnp.min(jnp.where(rest == v2, lane, float(N_EXPERTS)), axis=-1, keepdims=True)
    oh2 = lane == i2
    e2 = jnp.exp(v2 - v1)
    den = 1.0 + e2
    gate_ref[0] = jnp.where(oh1, 1.0 / den, 0.0) + jnp.where(oh2, e2 / den, 0.0)
    sel_ref[0] = jnp.where(oh1 | oh2, 1.0, 0.0)


def _router_call(att, cv, x, mod, mod_rows, mod_idx, gpm, gpf, wo_bf, wr, *, tm, name):
    b, t, d = x.shape
    specs, row = _mix_in_specs(b, t, tm, mod_rows, mod_idx, d)
    vec = pl.BlockSpec((1, d), lambda i: (0, 0))
    in_specs = specs + [_mod_spec(mod_rows, 2, mod_idx), _mod_spec(mod_rows, 4, mod_idx),
                        _mod_spec(mod_rows, 3, mod_idx), vec, vec,
                        pl.BlockSpec((d, d), lambda i: (0, 0)),
                        pl.BlockSpec((d, N_EXPERTS), lambda i: (0, 0))]
    return pl.pallas_call(
        _router_kernel,
        grid=(b * (t // tm),),
        in_specs=in_specs,
        out_specs=[pl.BlockSpec((1, tm, d), row),
                   pl.BlockSpec((2, 1, tm, d // 4), lambda i: (0,) + row(i)),
                   pl.BlockSpec((1, tm, N_EXPERTS), row), pl.BlockSpec((1, tm, N_EXPERTS), row)],
        out_shape=[jax.ShapeDtypeStruct((b, t, d), F32), jax.ShapeDtypeStruct((2, b, t, d // 4), jnp.int32),
                   jax.ShapeDtypeStruct((b, t, N_EXPERTS), F32), jax.ShapeDtypeStruct((b, t, N_EXPERTS), F32)],
        compiler_params=_params(1), name=name,
    )(att, cv, x, mod, mod, mod, gpm, gpf, wo_bf, wr)


MOE_TILE = 1024
RANK_BLOCK = 1536
GATHER_WINDOW = 128
GATHER_ROWS = 32 * GATHER_WINDOW
MOE_CHUNK = 3
GATHER_DEPTH = 3


def _pack_pairs(x):
    w = x.shape[1] // 2
    lo = pltpu.bitcast(x[:, :w].astype(BF16).astype(F32), jnp.int32)
    hi = pltpu.bitcast(x[:, w:].astype(BF16).astype(F32), jnp.int32)
    return hi | lax.shift_right_logical(lo, 16)


def _unpack_pairs(words):
    lo = pltpu.bitcast(lax.shift_left(words, 16), F32)
    hi = pltpu.bitcast(words & -65536, F32)
    return jnp.concatenate([lo, hi], axis=1)


def _store_split(ref, x):
    words = _pack_pairs(x)
    q = words.shape[1] // 2
    ref[0] = words[:, :q]
    ref[1] = words[:, q:]


def _load_split(ref):
    return _unpack_pairs(jnp.concatenate([ref[0], ref[1]], axis=1))


def _rank_kernel(sel_ref, rank_ref, cnt_ref, carry):
    @pl.when(pl.program_id(0) == 0)
    def _():
        carry[...] = jnp.zeros_like(carry)

    sel = sel_ref[...]
    rb = sel.shape[0]
    before = (lax.broadcasted_iota(jnp.int32, (rb, rb), 1)
              < lax.broadcasted_iota(jnp.int32, (rb, rb), 0)).astype(BF16)
    rank_ref[...] = jnp.dot(before, sel.astype(BF16), preferred_element_type=F32) + carry[...]
    carry[...] += jnp.sum(sel, axis=0, keepdims=True)
    cnt_ref[...] = carry[...]


def _rank_call(sel):
    n, n_e = sel.shape
    rb = RANK_BLOCK
    assert n % rb == 0
    return pl.pallas_call(
        _rank_kernel,
        grid=(n // rb,),
        in_specs=[pl.BlockSpec((rb, n_e), lambda i: (i, 0))],
        out_specs=[pl.BlockSpec((rb, n_e), lambda i: (i, 0)), pl.BlockSpec((1, n_e), lambda i: (0, 0))],
        out_shape=[jax.ShapeDtypeStruct((n, n_e), F32), jax.ShapeDtypeStruct((1, n_e), F32)],
        scratch_shapes=[pltpu.VMEM((1, n_e), F32)],
        compiler_params=_params(1), name="moe_rank",
    )(sel)


def _sc_gather(table, idxs):
    halves, _, width = table.shape
    n_lists, n = len(idxs), idxs[0].shape[0]
    info = plsc.get_sparse_core_info()
    win, depth = GATHER_WINDOW, GATHER_DEPTH
    assert n % GATHER_ROWS == 0 and GATHER_ROWS == info.num_cores * info.num_subcores * win
    wins = n // GATHER_ROWS
    jobs = [(l, j, h) for l in range(n_lists) for j in range(wins) for h in range(halves)]
    mesh = plsc.VectorSubcoreMesh(core_axis_name="core", subcore_axis_name="subcore")
    out_type = [jax.ShapeDtypeStruct((halves, n, width), table.dtype) for _ in idxs]
    scratch = [pltpu.VMEM((n_lists, wins, win), jnp.int32), pltpu.VMEM((depth, win, width), table.dtype),
               pltpu.SemaphoreType.DMA((depth,)), pltpu.SemaphoreType.DMA((depth,))]

    @functools.partial(pl.kernel, out_type=out_type, mesh=mesh, scratch_types=scratch)
    def gather(table_hbm, idx_hbm, *refs):
        out_refs = refs[:n_lists]
        idx_v, buf, sem_in, sem_out = refs[n_lists:]
        worker = lax.axis_index("subcore") * info.num_cores + lax.axis_index("core")
        first = worker * wins
        for l in range(n_lists):
            pltpu.sync_copy(idx_hbm.at[l, worker], idx_v.at[l])
        for g in range(0, len(jobs), depth):
            group = jobs[g:g + depth]
            reads = [pltpu.async_copy(table_hbm.at[h].at[idx_v.at[l, j]], buf.at[k], sem_in.at[k])
                     for k, (l, j, h) in enumerate(group)]
            writes = []
            for k, (l, j, h) in enumerate(group):
                reads[k].wait()
                rows = pl.ds(pl.multiple_of((first + j) * win, win), win)
                writes.append(pltpu.async_copy(buf.at[k], out_refs[l].at[h, rows], sem_out.at[k]))
            for write in writes:
                write.wait()

    return gather(table, jnp.stack(idxs).reshape(n_lists, n // (wins * win), wins, win))


def _expert_kernel(te_ref, nu_ref, xs_ref, wg_ref, wu_ref, wd_ref, *refs):
    ys_ref, xb, acc = refs[-3:]
    t, f = pl.program_id(0), pl.program_id(1)
    last = pl.num_programs(1) - 1
    used = t < nu_ref[0]

    @pl.when(used & (f == 0))
    def _():
        xb[...] = _load_split(xs_ref).astype(BF16)
        acc[...] = jnp.zeros_like(acc)

    @pl.when(used)
    def _():
        x = xb[...]
        g = jnp.dot(x, wg_ref[0].astype(BF16), preferred_element_type=F32)
        u = jnp.dot(x, wu_ref[0].astype(BF16), preferred_element_type=F32)
        act = (_silu(g) * u).astype(BF16)
        acc[...] += jnp.dot(act, wd_ref[0].astype(BF16), preferred_element_type=F32)

    @pl.when(used & (f == last))
    def _():
        _store_split(ys_ref, acc[...])

    @pl.when(jnp.logical_not(used) & (f == last))
    def _():
        ys_ref[...] = jnp.zeros_like(ys_ref)


def _expert_call(tile_expert, n_used, xs, wg, wu, wd, ys, tile0, rows_total, *, tf):
    _, rows, quarter = xs.shape
    n_e, d, ff = wg.shape
    tm = MOE_TILE
    nf = ff // tf
    chunk = lambda t, f, te, nu: jnp.where(t < nu[0], f, nf - 1)
    in_specs = [pl.BlockSpec((2, tm, quarter), lambda t, f, te, nu: (0, t, 0)),
                pl.BlockSpec((1, d, tf), lambda t, f, te, nu: (te[t], 0, chunk(t, f, te, nu))),
                pl.BlockSpec((1, d, tf), lambda t, f, te, nu: (te[t], 0, chunk(t, f, te, nu))),
                pl.BlockSpec((1, tf, d), lambda t, f, te, nu: (te[t], chunk(t, f, te, nu), 0))]
    args = [tile_expert, n_used, xs, wg, wu, wd]
    aliases = {}
    if ys is not None:
        in_specs.append(pl.BlockSpec(memory_space=pl.ANY))
        args.append(ys)
        aliases = {len(args) - 1: 0}
    grid_spec = pltpu.PrefetchScalarGridSpec(
        num_scalar_prefetch=2,
        grid=(rows // tm, nf),
        in_specs=in_specs,
        out_specs=pl.BlockSpec((2, tm, quarter), lambda t, f, te, nu: (0, tile0 + t, 0)),
        scratch_shapes=[pltpu.VMEM((tm, d), BF16), pltpu.VMEM((tm, d), F32)])
    return pl.pallas_call(
        _expert_kernel, grid_spec=grid_spec,
        out_shape=jax.ShapeDtypeStruct((2, rows_total, quarter), jnp.int32),
        input_output_aliases=aliases,
        compiler_params=_params(2), name="moe_experts",
    )(*args)


def _moe_route(h2p, sel, gates, wg, wu, wd, *, tf):
    n, n_e = sel.shape
    tm = MOE_TILE
    rank, cnt = _rank_call(sel)
    counts = cnt[0].astype(jnp.int32)
    padded = (counts + tm - 1) // tm * tm
    seg_end = jnp.cumsum(padded)
    seg_start = seg_end - padded
    rows_max = -(-(2 * n + n_e * tm) // GATHER_ROWS) * GATHER_ROWS
    assert rows_max % tm == 0
    lanes = jnp.arange(n_e, dtype=jnp.int32)[None, :]
    e_lo = jnp.min(jnp.where(sel > 0, lanes, n_e - 1), axis=1)
    e_hi = jnp.max(jnp.where(sel > 0, lanes, 0), axis=1)
    pick = lambda a, e: jnp.take_along_axis(a, e[:, None], axis=1)[:, 0]
    rank_i = rank.astype(jnp.int32)
    d_lo = seg_start[e_lo] + pick(rank_i, e_lo)
    d_hi = seg_start[e_hi] + pick(rank_i, e_hi)
    w = jnp.stack([pick(gates, e_lo), pick(gates, e_hi)], axis=1)
    tok = jnp.arange(n, dtype=jnp.int32)
    src = jnp.zeros((rows_max,), jnp.int32).at[d_lo].set(tok).at[d_hi].set(tok)
    n_tiles = rows_max // tm
    n_used = (seg_end[-1] // tm).astype(jnp.int32)
    tile_ids = jnp.minimum(jnp.arange(n_tiles, dtype=jnp.int32), n_used - 1)
    tile_expert = jnp.minimum(jnp.searchsorted(seg_end, tile_ids * tm, side="right"), n_e - 1).astype(jnp.int32)

    step = GATHER_ROWS // tm * MOE_CHUNK
    ys = None
    for t0 in range(0, n_tiles, step):
        t1 = min(t0 + step, n_tiles)
        xs, = _sc_gather(h2p, [src[t0 * tm:t1 * tm]])
        ys = _expert_call(tile_expert[t0:t1], jnp.clip(n_used - t0, 0, t1 - t0).reshape(1), xs, wg, wu, wd,
                          ys, t0, rows_max, tf=tf)
    n_pad = -(-n // GATHER_ROWS) * GATHER_ROWS
    pad_idx = lambda dd: jnp.pad(dd.astype(jnp.int32), (0, n_pad - n))
    ya, yb = _sc_gather(ys, [pad_idx(d_lo), pad_idx(d_hi)])
    return ya, yb, w


def _combine_kernel(ya_ref, yb_ref, w_ref, x1_ref, gt2_ref, gqf_ref, o_ref):
    w = w_ref[...]
    f = w[:, 0:1] * _load_split(ya_ref) + w[:, 1:2] * _load_split(yb_ref)
    o_ref[0] = x1_ref[0] + gt2_ref[0] * _rms(f, gqf_ref[...])


def _combine_call(ya, yb, w, x1, row0, mod, mod_rows, mod_idx, gqf, *, tm, name):
    b, t, d = x1.shape
    ntb = t // tm
    blk0 = row0 // tm
    assert row0 % tm == 0
    pool = lambda i: (blk0 + i, 0)
    pool3 = lambda i: (0, blk0 + i, 0)
    row = lambda i: (i // ntb, i % ntb, 0)
    return pl.pallas_call(
        _combine_kernel,
        grid=(b * ntb,),
        in_specs=[pl.BlockSpec((2, tm, d // 4), pool3), pl.BlockSpec((2, tm, d // 4), pool3),
                  pl.BlockSpec((tm, 2), pool), pl.BlockSpec((1, tm, d), row),
                  _mod_spec(mod_rows, 5, mod_idx), pl.BlockSpec((1, d), lambda i: (0, 0))],
        out_specs=pl.BlockSpec((1, tm, d), row),
        out_shape=jax.ShapeDtypeStruct((b, t, d), F32),
        compiler_params=_params(1), name=name,
    )(ya, yb, w, x1, mod, gqf)


def _rope_tables(pos):
    half = ROT_DIM // 2
    inv_freq = ROPE_THETA ** (-jnp.arange(half, dtype=F32) * 2.0 / ROT_DIM)
    ang = pos.astype(F32)[:, None] * inv_freq[None, :]
    cos, sin = jnp.cos(ang), jnp.sin(ang)
    l64 = np.arange(128) % HEAD_DIM
    idx = l64 % half
    first = (l64 < half)[None, :]
    second = ((l64 >= half) & (l64 < ROT_DIM))[None, :]
    cos_t = jnp.where(first | second, cos[:, idx], 1.0)
    s1_t = jnp.where(first, -sin[:, idx], 0.0)
    s2_t = jnp.where(second, sin[:, idx], 0.0)
    return cos_t, s1_t, s2_t


def kernel(x_prompt, x_sample, cache_k, cache_v, state_conv, c_prompt, c_sample, w_mod, b_mod, g_pre_mix, g_post_mix, g_pre_ffn, g_post_ffn, w_in, conv_w, conv_b, conv_ln_g, conv_ln_b, w_out, ffn_w_gate, ffn_w_up, ffn_w_down, moe_w_router, moe_w_gate, moe_w_up, moe_w_down):
    bp, seq, d = x_prompt.shape
    nb, n_new, _ = x_sample.shape
    depth = w_mod.shape[0]
    w_buf = cache_k.shape[2]
    past_len = PAST_LEN
    assert w_buf == min(DILATION_PATTERNS[-1][0], past_len)
    keep = min(DILATION_PATTERNS[-1][0], seq)
    ns = nb * n_new

    c_all = jnp.concatenate([c_sample, c_prompt, jnp.zeros((MOD_ROWS - nb - bp, d), F32)], axis=0)
    mod_all = _mod_call(c_all, w_mod, b_mod)

    tm_p = 512
    tabs_p = tuple(tb.reshape(seq // tm_p, tm_p, 128) for tb in _rope_tables(jnp.arange(seq, dtype=jnp.int32)))
    tabs_s = tuple(tb.reshape(n_new, 1, 128)
                   for tb in _rope_tables(past_len + jnp.arange(n_new, dtype=jnp.int32)))

    cache_kt = cache_k.transpose(0, 1, 3, 4, 2)
    cache_vt = cache_v.transpose(0, 1, 3, 4, 2)

    yp = x_prompt
    ys = x_sample.transpose(1, 0, 2).reshape(1, ns, d)
    outs = [[] for _ in range(6)]
    for l in range(depth):
        mod_p = mod_all[l].reshape(MOD_ROWS, 1, 6 * d)
        mod_s = jnp.tile(mod_all[l, :nb], (n_new, 1))[None]
        idx_p2 = lambda bi, i: (nb + bi, 0)
        idx_s2 = lambda bi, i: (0, i)
        w_in_bf = w_in[l].astype(BF16)
        w_out_bf = w_out[l].astype(BF16)
        conv_p = (jnp.pad(conv_w[l], ((0, 1), (0, 0))), conv_b[l][None], conv_ln_g[l][None], conv_ln_b[l][None])
        gpre, gpm, gpf, gqf = g_pre_mix[l][None], g_post_mix[l][None], g_pre_ffn[l][None], g_post_ffn[l][None]

        q, k, v, kt, vt, cv, ul = _inproj_call(yp, mod_p, 1, idx_p2, gpre, w_in_bf, tabs_p, tm_p, conv_p,
                                               tm=tm_p, keep=keep, name=f"inproj_prompt_{l}")
        att = _attn_call(q, k, v)
        outs[0].append(kt.reshape(bp, keep, N_HEADS, HEAD_DIM))
        outs[1].append(vt.reshape(bp, keep, N_HEADS, HEAD_DIM))
        outs[2].append(ul[:, CONV_HALO - (CONV_WIDTH - 1):])

        qs, _, _, kts, vts, us = _inproj_call(ys, mod_s, nb, idx_s2, gpre, w_in_bf, tabs_s, 1, None,
                                              tm=nb, keep=ns, name=f"inproj_sample_{l}")
        to_batch_major = lambda z: z.reshape(n_new, nb, N_HEADS, HEAD_DIM).transpose(1, 0, 2, 3)
        q_bm = (qs.reshape(HEAD_PAIRS, n_new, nb, 2, HEAD_DIM).transpose(2, 1, 0, 3, 4)
                .reshape(nb, n_new, N_HEADS, HEAD_DIM))
        k_bm, v_bm = to_batch_major(kts), to_batch_major(vts)
        head_major = lambda z: jnp.pad(z.transpose(0, 2, 1, 3), ((0, 0), (0, 0), (0, SAMPLE_ROWS - n_new), (0, 0)))
        att_s = _sample_attn_call(head_major(q_bm), head_major(k_bm), head_major(v_bm),
                                  cache_kt, cache_vt, l, n_new)
        att_s = (att_s[:, :, :n_new].reshape(nb, HEAD_PAIRS, 2, n_new, HEAD_DIM).transpose(1, 3, 0, 2, 4)
                 .reshape(1, HEAD_PAIRS, ns, 128).astype(BF16))
        ufull = jnp.concatenate([state_conv[l].transpose(1, 0, 2), us.reshape(n_new, nb, 512)], axis=0)
        cv_s = _sample_conv_call(ufull, conv_p).reshape(1, ns, 512)
        outs[3].append(k_bm)
        outs[4].append(v_bm)
        outs[5].append(ufull[n_new:].transpose(1, 0, 2))

        idx_p1 = lambda i, *_: (nb + i // (seq // tm_f), 0)
        idx_s1 = lambda i, *_: (0, i)
        if l % 2 == 0:
            tm_f = 1024
            wg, wu, wd = ffn_w_gate[l // 2], ffn_w_up[l // 2], ffn_w_down[l // 2]
            yp = _ffn_call(att, cv, yp, mod_p, 1, idx_p1, gpm, gpf, gqf, w_out_bf, wg, wu, wd,
                           tm=tm_f, tf=256, name=f"ffn_prompt_{l}")
            ys = _ffn_call(att_s, cv_s, ys, mod_s, ns, idx_s1, gpm, gpf, gqf, w_out_bf, wg, wu, wd,
                           tm=ns, tf=512, name=f"ffn_sample_{l}")
        else:
            tm_f = 1024
            wr = moe_w_router[l // 2]
            wg, wu, wd = moe_w_gate[l // 2], moe_w_up[l // 2], moe_w_down[l // 2]
            x1, h2, gates, sel = _router_call(att, cv, yp, mod_p, 1, idx_p1, gpm, gpf, w_out_bf, wr,
                                              tm=tm_f, name=f"router_prompt_{l}")
            x1s, h2s, gates_s, sel_s = _router_call(att_s, cv_s, ys, mod_s, ns, idx_s1, gpm, gpf, w_out_bf, wr,
                                                    tm=ns, name=f"router_sample_{l}")
            pool = lambda a, b_: jnp.concatenate([a.reshape(bp * seq, -1), b_.reshape(ns, -1)], axis=0)
            h2_pool = jnp.concatenate([h2.reshape(2, bp * seq, d // 4), h2s.reshape(2, ns, d // 4)], axis=1)
            ya, yb, w12 = _moe_route(h2_pool, pool(sel, sel_s), pool(gates, gates_s), wg, wu, wd, tf=512)
            yp = _combine_call(ya, yb, w12, x1, 0, mod_p, 1, idx_p1, gqf, tm=tm_f, name=f"moe_combine_prompt_{l}")
            ys = _combine_call(ya, yb, w12, x1s, bp * seq, mod_s, ns, idx_s1, gqf, tm=ns,
                               name=f"moe_combine_sample_{l}")

    y_sample = ys.reshape(n_new, nb, d).transpose(1, 0, 2)
    return (yp, y_sample) + tuple(jnp.stack(o) for o in outs)
```

```python
import functools

import numpy as np
import jax
import jax.numpy as jnp
from jax import lax
from jax.experimental import pallas as pl
from jax.experimental.pallas import tpu as pltpu
from jax.experimental.pallas import tpu_sc as plsc

F32 = jnp.float32
BF16 = jnp.bfloat16

HEAD_DIM = 64
N_HEADS = 8
ATT_WIDTH = N_HEADS * HEAD_DIM
HEAD_PAIRS = ATT_WIDTH // 128
CONV_WIDTH = 31
CONV_HALO = 32
DILATION_PATTERNS = ((128, 1), (512, 4), (2048, 16))
BAND = 128
ROT_DIM = HEAD_DIM // 4
ROPE_THETA = 500000.0
ATTN_SCALE = HEAD_DIM ** -0.5
N_EXPERTS = 8
PAST_LEN = 2048
EPS = 1e-6
NEG = -1e30
MOD_ROWS = 136
VMEM_LIMIT = 56 * 1024 * 1024


def _rms(x, g):
    return x * lax.rsqrt(jnp.mean(x * x, axis=-1, keepdims=True) + EPS) * g


def _silu(x):
    return x * jax.nn.sigmoid(x)


def _params(n_axes, vmem=VMEM_LIMIT):
    return pltpu.CompilerParams(dimension_semantics=("arbitrary",) * n_axes, vmem_limit_bytes=vmem)


def _mod_kernel(c_ref, w_ref, b_ref, o_ref):
    a = _silu(c_ref[...]).astype(BF16)
    o_ref[0] = jnp.dot(a, w_ref[0].astype(BF16), preferred_element_type=F32) + b_ref[0]


def _mod_call(c_all, w_mod, b_mod):
    depth, d, d6 = w_mod.shape
    return pl.pallas_call(
        _mod_kernel,
        grid=(depth, d6 // d),
        in_specs=[pl.BlockSpec((MOD_ROWS, d), lambda l, j: (0, 0)),
                  pl.BlockSpec((1, d, d), lambda l, j: (l, 0, j)),
                  pl.BlockSpec((1, 1, d), lambda l, j: (l, 0, j))],
        out_specs=pl.BlockSpec((1, MOD_ROWS, d), lambda l, j: (l, 0, j)),
        out_shape=jax.ShapeDtypeStruct((depth, MOD_ROWS, d6), F32),
        compiler_params=_params(2),
        name="adaln_mod",
    )(c_all, w_mod, b_mod.reshape(depth, 1, d6))


def _inproj_kernel(*refs, conv, tm):
    if conv:
        (x_ref, sh_ref, sc_ref, g_ref, w_ref, cos_ref, s1_ref, s2_ref,
         cw_ref, cb_ref, lg_ref, lb_ref,
         q_ref, k_ref, v_ref, kt_ref, vt_ref, cv_ref, ul_ref, ubuf) = refs
    else:
        (x_ref, sh_ref, sc_ref, g_ref, w_ref, cos_ref, s1_ref, s2_ref,
         q_ref, k_ref, v_ref, kt_ref, vt_ref, u_ref) = refs

    h = _rms(x_ref[0], g_ref[...]) * (1.0 + sc_ref[0]) + sh_ref[0]
    proj = jnp.dot(h.astype(BF16), w_ref[...], preferred_element_type=F32)
    cos, s1, s2 = cos_ref[0], s1_ref[0], s2_ref[0]

    def rope(z):
        return z * cos + pltpu.roll(z, 128 - ROT_DIM // 2, 1) * s1 + pltpu.roll(z, ROT_DIM // 2, 1) * s2

    for hp in range(HEAD_PAIRS):
        lo, hi = hp * 128, (hp + 1) * 128
        q_ref[0, hp] = rope(proj[:, lo:hi]) * ATTN_SCALE
        kz = rope(proj[:, ATT_WIDTH + lo:ATT_WIDTH + hi])
        k_ref[0, hp] = kz
        kt_ref[0, :, lo:hi] = kz
        vz = proj[:, 2 * ATT_WIDTH + lo:2 * ATT_WIDTH + hi]
        v_ref[0, hp] = vz
        vt_ref[0, :, lo:hi] = vz

    a = proj[:, 3 * ATT_WIDTH:3 * ATT_WIDTH + 512]
    gate = proj[:, 3 * ATT_WIDTH + 512:]
    u = a * jax.nn.sigmoid(gate)
    if not conv:
        u_ref[0] = u
        return

    @pl.when(pl.program_id(1) == 0)
    def _():
        ubuf[0, 0:CONV_HALO, :] = jnp.zeros((CONV_HALO, 512), F32)

    ubuf[0, CONV_HALO:CONV_HALO + tm, :] = u
    span = tm + CONV_HALO - 8
    for s in range(1, 8):
        ubuf[s, 0:span, :] = ubuf[0, s:s + span, :]
    off0 = CONV_HALO - (CONV_WIDTH - 1)
    rows = 32

    def chunk(r, carry):
        base = pl.multiple_of(r * rows, rows)
        acc = jnp.broadcast_to(cb_ref[...], (rows, 512))
        for j in range(CONV_WIDTH):
            a, s = divmod(off0 + j, 8)
            acc = acc + cw_ref[j:j + 1, :] * ubuf[s, pl.ds(base + 8 * a, rows), :]
        mu = jnp.mean(acc, axis=-1, keepdims=True)
        xc = acc - mu
        var = jnp.mean(xc * xc, axis=-1, keepdims=True)
        y = xc * lax.rsqrt(var + EPS) * lg_ref[...] + lb_ref[...]
        cv_ref[0, pl.ds(base, rows), :] = _silu(y).astype(BF16)
        return carry

    lax.fori_loop(0, tm // rows, chunk, 0, unroll=4)
    tail = ubuf[0, tm:tm + CONV_HALO, :]
    ul_ref[0] = tail
    ubuf[0, 0:CONV_HALO, :] = tail


def _mod_spec(rows, chunk, index_fn):
    return pl.BlockSpec((1, rows, 1024), lambda *g: index_fn(*g) + (chunk,))


def _inproj_call(x, mod, mod_rows, mod_idx, g, w_bf, tabs, tab_rows, conv_p, *, tm, keep, name):
    b, t, d = x.shape
    nt = t // tm
    off = (t - keep) // tm
    conv = conv_p is not None
    tab_spec = pl.BlockSpec((1, tab_rows, 128), lambda bi, i: (i, 0, 0))
    in_specs = [pl.BlockSpec((1, tm, d), lambda bi, i: (bi, i, 0)),
                _mod_spec(mod_rows, 0, mod_idx), _mod_spec(mod_rows, 1, mod_idx),
                pl.BlockSpec((1, d), lambda bi, i: (0, 0)),
                pl.BlockSpec(w_bf.shape, lambda bi, i: (0, 0)),
                tab_spec, tab_spec, tab_spec]
    args = [x, mod, mod, g, w_bf, *tabs]
    hp_spec = pl.BlockSpec((1, HEAD_PAIRS, tm, 128), lambda bi, i: (bi, 0, i, 0))
    tail_spec = pl.BlockSpec((1, tm, 512), lambda bi, i: (bi, jnp.maximum(i - off, 0), 0))
    hp_shape = jax.ShapeDtypeStruct((b, HEAD_PAIRS, t, 128), F32)
    tail_shape = jax.ShapeDtypeStruct((b, keep, 512), F32)
    out_specs = [hp_spec, hp_spec, hp_spec, tail_spec, tail_spec]
    out_shape = [hp_shape, hp_shape, hp_shape, tail_shape, tail_shape]
    scratch = []
    if conv:
        small = pl.BlockSpec((1, 512), lambda bi, i: (0, 0))
        in_specs += [pl.BlockSpec((32, 512), lambda bi, i: (0, 0)), small, small, small]
        args += list(conv_p)
        out_specs += [pl.BlockSpec((1, tm, 512), lambda bi, i: (bi, i, 0)),
                      pl.BlockSpec((1, CONV_HALO, 512), lambda bi, i: (bi, 0, 0))]
        out_shape += [jax.ShapeDtypeStruct((b, t, 512), BF16),
                      jax.ShapeDtypeStruct((b, CONV_HALO, 512), F32)]
        scratch = [pltpu.VMEM((8, tm + CONV_HALO, 512), F32)]
    else:
        out_specs += [pl.BlockSpec((1, tm, 512), lambda bi, i: (bi, i, 0))]
        out_shape += [jax.ShapeDtypeStruct((b, t, 512), F32)]
    return pl.pallas_call(
        functools.partial(_inproj_kernel, conv=conv, tm=tm),
        grid=(b, nt), in_specs=in_specs, out_specs=out_specs, out_shape=out_shape,
        scratch_shapes=scratch, compiler_params=_params(2), name=name,
    )(*args)


def _attn_kernel(q_ref, kc_ref, kp_ref, vc_ref, vp_ref, o_ref, acc_s, m_s, l_s, *, tq):
    first_kj = jnp.where(pl.program_id(2) == 0, 128, 0)
    lane_lo = lax.broadcasted_iota(jnp.int32, (128, 128), 1) < HEAD_DIM
    qi = lax.broadcasted_iota(jnp.int32, (256, 256), 0)
    qi = jnp.where(qi >= 128, qi - 128, qi)
    kj = lax.broadcasted_iota(jnp.int32, (256, 256), 1)
    band = (kj >= qi) & (kj <= qi + BAND)
    band_first = band & (kj >= first_kj)

    ones_cols = jnp.ones((256, 128), BF16)
    nt = (((1,), (1,)), ((), ()))

    def strided(start, size, d):
        if d > 1:
            return pl.ds(start, size, stride=d)
        return pl.ds(start if isinstance(start, int) else pl.multiple_of(start, 128), size)

    def blocks(items):
        loaded = []
        for rows, first, c, d, s in items:
            if first:
                take = lambda rp, rc: jnp.concatenate(
                    [rp[0, 0, strided(tq - BAND * d + c, 128, d), :],
                     rc[0, 0, strided(c, 128, d), :]], axis=0)
            else:
                krows = strided(c + d * 128 * (s - 1), 256, d)
                take = lambda rp, rc: rc[0, 0, krows, :]
            loaded.append((q_ref[0, 0, rows, :], take(kp_ref, kc_ref), take(vp_ref, vc_ref),
                           band_first if first else band))
        scores = []
        for q2, k2, _, _ in loaded:
            qq = jnp.concatenate([jnp.where(lane_lo, q2, 0.0), jnp.where(lane_lo, 0.0, q2)], axis=0)
            scores.append(lax.dot_general(qq.astype(BF16), k2.astype(BF16), nt,
                                          preferred_element_type=F32))
        probs = []
        for sc, (_, _, _, mask) in zip(scores, loaded):
            sc = jnp.where(mask, sc, NEG)
            mb = jnp.max(sc, axis=1, keepdims=True)
            probs.append((mb, jnp.exp(sc - mb).astype(BF16)))
        out = []
        for (mb, p), (_, _, v2, _), item in zip(probs, loaded, items):
            pv = jnp.dot(p, jnp.concatenate([v2.astype(BF16), ones_cols], axis=1), preferred_element_type=F32)
            out.append((item[0], jnp.where(lane_lo, mb[:128], mb[128:]),
                        jnp.where(lane_lo, pv[:128, 128:], pv[128:, 128:]),
                        jnp.where(lane_lo, pv[:128, :128], pv[128:, :128])))
        return out

    def merge(rows, mt, lt, pvt):
        mp, lp, ap = m_s[rows, :], l_s[rows, :], acc_s[rows, :]
        mn = jnp.maximum(mp, mt)
        a = jnp.exp(mp - mn)
        b = jnp.exp(mt - mn)
        return mn, a * lp + b * lt, a * ap + b * pvt

    def store_init(results):
        for rows, mt, lt, pvt in results:
            m_s[rows, :] = mt
            l_s[rows, :] = lt
            acc_s[rows, :] = pvt

    def store_merged(results):
        for rows, mt, lt, pvt in results:
            mn, ln, an = merge(rows, mt, lt, pvt)
            m_s[rows, :] = mn
            l_s[rows, :] = ln
            acc_s[rows, :] = an

    def store_output(results):
        for rows, mt, lt, pvt in results:
            _, ln, an = merge(rows, mt, lt, pvt)
            o_ref[0, 0, rows, :] = (an / ln).astype(o_ref.dtype)

    def loop(n, fn):
        def body(i, carry):
            fn(i)
            return carry
        lax.fori_loop(0, n, body, 0)

    group = 4
    loop(16 // group, lambda g: store_init(blocks(
        [(pl.ds(g * group + j, 128, stride=16), True, g * group + j, 16, 0) for j in range(group)])))

    store_merged(blocks([(pl.ds(c, 128, stride=4), True, c, 4, 0) for c in range(4)]))
    loop(tq // 512 - 1, lambda n: store_merged(blocks(
        [(pl.ds(c + 512 * (n + 1), 128, stride=4), False, c, 4, n + 1) for c in range(4)])))

    store_output(blocks([(strided(0, 128, 1), True, 0, 1, 0)]
                        + [(strided(128 * s, 128, 1), False, 0, 1, s) for s in range(1, group)]))
    rest = 3
    assert (tq // 128 - group) % rest == 0
    loop((tq // 128 - group) // rest, lambda g: store_output(blocks(
        [(strided(128 * (group + g * rest + j), 128, 1), False, 0, 1, group + g * rest + j) for j in range(rest)])))


def _attn_call(q, k, v, *, tq=2048):
    b, hp, t, _ = q.shape
    assert tq == BAND * 16 and t % tq == 0
    cur = pl.BlockSpec((1, 1, tq, 128), lambda bi, h, i: (bi, h, i, 0))
    prev = pl.BlockSpec((1, 1, tq, 128), lambda bi, h, i: (bi, h, jnp.maximum(i - 1, 0), 0))
    return pl.pallas_call(
        functools.partial(_attn_kernel, tq=tq),
        grid=(b, hp, t // tq),
        in_specs=[cur, cur, prev, cur, prev],
        out_specs=cur,
        out_shape=jax.ShapeDtypeStruct((b, hp, t, 128), BF16),
        scratch_shapes=[pltpu.VMEM((tq, 128), F32)] * 3,
        compiler_params=_params(3), name="dilated_attn_prompt",
    )(q, k, k, v, v)


SAMPLE_ROWS = 8


def _sample_attn_kernel(q_ref, kn_ref, vn_ref, kt_ref, vt_ref, mh_ref, mn_ref, o_ref):
    mh, mn = mh_ref[...], mn_ref[...]
    nt = (((1,), (1,)), ((), ()))
    for h in range(N_HEADS):
        qh = q_ref[0, h].astype(BF16)
        s = jnp.dot(qh, kt_ref[0, 0, h].astype(BF16), preferred_element_type=F32)
        sn = lax.dot_general(qh, kn_ref[0, h].astype(BF16), nt, preferred_element_type=F32)
        s = jnp.where(mh > 0, s, NEG)
        sn = jnp.where(mn > 0, sn, NEG)
        m = jnp.maximum(jnp.max(s, axis=1, keepdims=True), jnp.max(sn, axis=1, keepdims=True))
        p = mh * jnp.exp(s - m)
        pn = mn * jnp.exp(sn - m)
        den = jnp.sum(p, axis=1, keepdims=True) + jnp.sum(pn, axis=1, keepdims=True)
        num = (lax.dot_general(p.astype(BF16), vt_ref[0, 0, h].astype(BF16), nt, preferred_element_type=F32)
               + jnp.dot(pn.astype(BF16), vn_ref[0, h].astype(BF16), preferred_element_type=F32))
        o_ref[0, h] = num / jnp.where(den > 0, den, 1.0)


def _sample_tables(w_buf, n_new):
    rows = np.arange(w_buf + n_new)
    mult = np.zeros((SAMPLE_ROWS, w_buf + SAMPLE_ROWS), np.float32)
    for t in range(n_new):
        dist = w_buf + t - rows
        for window, dil in DILATION_PATTERNS:
            mult[t, :w_buf + n_new] += (dist >= 0) & (dist <= window) & (dist % dil == 0)
    return jnp.asarray(mult[:, :w_buf]), jnp.asarray(mult[:, w_buf:])


def _sample_attn_call(q, kn, vn, cache_kt, cache_vt, layer, n_new):
    nb = q.shape[0]
    w_buf = cache_kt.shape[-1]
    mh, mn = _sample_tables(w_buf, n_new)
    new_spec = pl.BlockSpec((1, N_HEADS, SAMPLE_ROWS, HEAD_DIM), lambda i: (i, 0, 0, 0))
    cache_spec = pl.BlockSpec((1, 1, N_HEADS, HEAD_DIM, w_buf), lambda i: (layer, i, 0, 0, 0))
    return pl.pallas_call(
        _sample_attn_kernel,
        grid=(nb,),
        in_specs=[new_spec, new_spec, new_spec, cache_spec, cache_spec,
                  pl.BlockSpec(mh.shape, lambda i: (0, 0)), pl.BlockSpec(mn.shape, lambda i: (0, 0))],
        out_specs=new_spec,
        out_shape=jax.ShapeDtypeStruct((nb, N_HEADS, SAMPLE_ROWS, HEAD_DIM), F32),
        compiler_params=_params(1), name="dilated_attn_sample",
    )(q, kn, vn, cache_kt, cache_vt, mh, mn)


def _sample_conv_kernel(uf_ref, cw_ref, cb_ref, lg_ref, lb_ref, o_ref, *, n_new, nb):
    rows = 32

    def body(n, carry):
        t = n // (nb // rows)
        base = pl.multiple_of((n % (nb // rows)) * rows, rows)
        acc = jnp.broadcast_to(cb_ref[...], (rows, 512))
        for j in range(CONV_WIDTH):
            acc = acc + cw_ref[j:j + 1, :] * uf_ref[t + j, pl.ds(base, rows), :]
        mu = jnp.mean(acc, axis=-1, keepdims=True)
        xc = acc - mu
        var = jnp.mean(xc * xc, axis=-1, keepdims=True)
        y = xc * lax.rsqrt(var + EPS) * lg_ref[...] + lb_ref[...]
        o_ref[t, pl.ds(base, rows), :] = _silu(y).astype(BF16)
        return carry

    lax.fori_loop(0, n_new * (nb // rows), body, 0)


def _sample_conv_call(ufull, conv_p):
    n_full, nb, ch = ufull.shape
    n_new = n_full - (CONV_WIDTH - 1)
    return pl.pallas_call(
        functools.partial(_sample_conv_kernel, n_new=n_new, nb=nb),
        out_shape=jax.ShapeDtypeStruct((n_new, nb, ch), BF16),
        compiler_params=pltpu.CompilerParams(vmem_limit_bytes=VMEM_LIMIT),
        name="conv_sample",
    )(ufull, *conv_p)


def _mix_residual(att_ref, cv_ref, x_ref, gt1_ref, sc2_ref, sh2_ref, gpm_ref, gpf_ref, wo_ref):
    a = jnp.concatenate([att_ref[0, hp] for hp in range(HEAD_PAIRS)] + [cv_ref[0]], axis=-1)
    mix = jnp.dot(a, wo_ref[...], preferred_element_type=F32)
    x1 = x_ref[0] + gt1_ref[0] * _rms(mix, gpm_ref[...])
    h2 = _rms(x1, gpf_ref[...]) * (1.0 + sc2_ref[0]) + sh2_ref[0]
    return x1, h2


def _ffn_kernel(att_ref, cv_ref, x_ref, gt1_ref, sc2_ref, sh2_ref, gt2_ref, gpm_ref, gpf_ref, gqf_ref,
                wo_ref, wg_ref, wu_ref, wd_ref, o_ref, x1_s, h2_s, acc_s):
    f = pl.program_id(1)

    @pl.when(f == 0)
    def _():
        x1, h2 = _mix_residual(att_ref, cv_ref, x_ref, gt1_ref, sc2_ref, sh2_ref, gpm_ref, gpf_ref, wo_ref)
        x1_s[...] = x1
        h2_s[...] = h2.astype(BF16)
        acc_s[...] = jnp.zeros_like(acc_s)

    h2 = h2_s[...]
    g = jnp.dot(h2, wg_ref[...].astype(BF16), preferred_element_type=F32)
    u = jnp.dot(h2, wu_ref[...].astype(BF16), preferred_element_type=F32)
    act = (_silu(g) * u).astype(BF16)
    acc_s[...] += jnp.dot(act, wd_ref[...].astype(BF16), preferred_element_type=F32)

    @pl.when(f == pl.num_programs(1) - 1)
    def _():
        o_ref[0] = x1_s[...] + gt2_ref[0] * _rms(acc_s[...], gqf_ref[...])


def _mix_in_specs(b, t, tm, mod_rows, mod_idx, d):
    ntb = t // tm
    row = lambda i, *_: (i // ntb, i % ntb, 0)
    specs = [pl.BlockSpec((1, HEAD_PAIRS, tm, 128), lambda i, *_: (i // ntb, 0, i % ntb, 0)),
             pl.BlockSpec((1, tm, 512), row),
             pl.BlockSpec((1, tm, d), row)]
    return specs, row


def _ffn_call(att, cv, x, mod, mod_rows, mod_idx, gpm, gpf, gqf, wo_bf, wg, wu, wd, *, tm, tf, name):
    b, t, d = x.shape
    ff = wg.shape[1]
    ntb = t // tm
    specs, row = _mix_in_specs(b, t, tm, mod_rows, mod_idx, d)
    vec = pl.BlockSpec((1, d), lambda i, f: (0, 0))
    in_specs = specs + [_mod_spec(mod_rows, 2, mod_idx), _mod_spec(mod_rows, 4, mod_idx),
                        _mod_spec(mod_rows, 3, mod_idx), _mod_spec(mod_rows, 5, mod_idx),
                        vec, vec, vec,
                        pl.BlockSpec((d, d), lambda i, f: (0, 0)),
                        pl.BlockSpec((d, tf), lambda i, f: (0, f)),
                        pl.BlockSpec((d, tf), lambda i, f: (0, f)),
                        pl.BlockSpec((tf, d), lambda i, f: (f, 0))]
    return pl.pallas_call(
        _ffn_kernel,
        grid=(b * ntb, ff // tf),
        in_specs=in_specs,
        out_specs=pl.BlockSpec((1, tm, d), row),
        out_shape=jax.ShapeDtypeStruct((b, t, d), F32),
        scratch_shapes=[pltpu.VMEM((tm, d), F32), pltpu.VMEM((tm, d), BF16), pltpu.VMEM((tm, d), F32)],
        compiler_params=_params(2), name=name,
    )(att, cv, x, mod, mod, mod, mod, gpm, gpf, gqf, wo_bf, wg, wu, wd)


def _router_kernel(att_ref, cv_ref, x_ref, gt1_ref, sc2_ref, sh2_ref, gpm_ref, gpf_ref, wo_ref, wr_ref,
                   x1_ref, h2_ref, gate_ref, sel_ref):
    x1, h2 = _mix_residual(att_ref, cv_ref, x_ref, gt1_ref, sc2_ref, sh2_ref, gpm_ref, gpf_ref, wo_ref)
    x1_ref[0] = x1
    _store_split(h2_ref.at[:, 0], h2)
    logits = jnp.concatenate([jnp.sum(h2 * wr_ref[e:e + 1, :], axis=1, keepdims=True)
                              for e in range(N_EXPERTS)], axis=1)
    lane = lax.broadcasted_iota(jnp.int32, logits.shape, 1).astype(F32)
    v1 = jnp.max(logits, axis=-1, keepdims=True)
    i1 = jnp.min(jnp.where(logits == v1, lane, float(N_EXPERTS)), axis=-1, keepdims=True)
    oh1 = lane == i1
    rest = jnp.where(oh1, -jnp.inf, logits)
    v2 = jnp.max(rest, axis=-1, keepdims=True)
    i2 = jnp.min(jnp.where(rest == v2, lane, float(N_EXPERTS)), axis=-1, keepdims=True)
    oh2 = lane == i2
    e2 = jnp.exp(v2 - v1)
    den = 1.0 + e2
    gate_ref[0] = jnp.where(oh1, 1.0 / den, 0.0) + jnp.where(oh2, e2 / den, 0.0)
    sel_ref[0] = jnp.where(oh1 | oh2, 1.0, 0.0)


def _router_call(att, cv, x, mod, mod_rows, mod_idx, gpm, gpf, wo_bf, wr, *, tm, name):
    b, t, d = x.shape
    specs, row = _mix_in_specs(b, t, tm, mod_rows, mod_idx, d)
    vec = pl.BlockSpec((1, d), lambda i: (0, 0))
    in_specs = specs + [_mod_spec(mod_rows, 2, mod_idx), _mod_spec(mod_rows, 4, mod_idx),
                        _mod_spec(mod_rows, 3, mod_idx), vec, vec,
                        pl.BlockSpec((d, d), lambda i: (0, 0)),
                        pl.BlockSpec((N_EXPERTS, d), lambda i: (0, 0))]
    return pl.pallas_call(
        _router_kernel,
        grid=(b * (t // tm),),
        in_specs=in_specs,
        out_specs=[pl.BlockSpec((1, tm, d), row),
                   pl.BlockSpec((2, 1, tm, d // 4), lambda i: (0,) + row(i)),
                   pl.BlockSpec((1, tm, N_EXPERTS), row), pl.BlockSpec((1, tm, N_EXPERTS), row)],
        out_shape=[jax.ShapeDtypeStruct((b, t, d), F32), jax.ShapeDtypeStruct((2, b, t, d // 4), jnp.int32),
                   jax.ShapeDtypeStruct((b, t, N_EXPERTS), F32), jax.ShapeDtypeStruct((b, t, N_EXPERTS), F32)],
        compiler_params=_params(1), name=name,
    )(att, cv, x, mod, mod, mod, gpm, gpf, wo_bf, wr)


MOE_TILE = 1024
RANK_BLOCK = 1536
GATHER_WINDOW = 128
GATHER_ROWS = 32 * GATHER_WINDOW
MOE_CHUNK = 3
GATHER_DEPTH = 3


def _pack_pairs(x):
    w = x.shape[1] // 2
    lo = pltpu.bitcast(x[:, :w].astype(BF16).astype(F32), jnp.int32)
    hi = pltpu.bitcast(x[:, w:].astype(BF16).astype(F32), jnp.int32)
    return hi | lax.shift_right_logical(lo, 16)


def _unpack_pairs(words):
    lo = pltpu.bitcast(lax.shift_left(words, 16), F32)
    hi = pltpu.bitcast(words & -65536, F32)
    return jnp.concatenate([lo, hi], axis=1)


def _store_split(ref, x):
    words = _pack_pairs(x)
    q = words.shape[1] // 2
    ref[0] = words[:, :q]
    ref[1] = words[:, q:]


def _load_split(ref):
    return _unpack_pairs(jnp.concatenate([ref[0], ref[1]], axis=1))


def _rank_kernel(sel_ref, rank_ref, cnt_ref, carry):
    @pl.when(pl.program_id(0) == 0)
    def _():
        carry[...] = jnp.zeros_like(carry)

    sel = sel_ref[...]
    rb = sel.shape[0]
    before = (lax.broadcasted_iota(jnp.int32, (rb, rb), 1)
              < lax.broadcasted_iota(jnp.int32, (rb, rb), 0)).astype(BF16)
    rank_ref[...] = jnp.dot(before, sel.astype(BF16), preferred_element_type=F32) + carry[...]
    carry[...] += jnp.sum(sel, axis=0, keepdims=True)
    cnt_ref[...] = carry[...]


def _rank_call(sel):
    n, n_e = sel.shape
    rb = RANK_BLOCK
    assert n % rb == 0
    return pl.pallas_call(
        _rank_kernel,
        grid=(n // rb,),
        in_specs=[pl.BlockSpec((rb, n_e), lambda i: (i, 0))],
        out_specs=[pl.BlockSpec((rb, n_e), lambda i: (i, 0)), pl.BlockSpec((1, n_e), lambda i: (0, 0))],
        out_shape=[jax.ShapeDtypeStruct((n, n_e), F32), jax.ShapeDtypeStruct((1, n_e), F32)],
        scratch_shapes=[pltpu.VMEM((1, n_e), F32)],
        compiler_params=_params(1), name="moe_rank",
    )(sel)


def _sc_gather(table, idxs):
    halves, _, width = table.shape
    n_lists, n = len(idxs), idxs[0].shape[0]
    info = plsc.get_sparse_core_info()
    win, depth = GATHER_WINDOW, GATHER_DEPTH
    assert n % GATHER_ROWS == 0 and GATHER_ROWS == info.num_cores * info.num_subcores * win
    wins = n // GATHER_ROWS
    jobs = [(l, j, h) for l in range(n_lists) for j in range(wins) for h in range(halves)]
    mesh = plsc.VectorSubcoreMesh(core_axis_name="core", subcore_axis_name="subcore")
    out_type = [jax.ShapeDtypeStruct((halves, n, width), table.dtype) for _ in idxs]
    scratch = [pltpu.VMEM((n_lists, wins, win), jnp.int32), pltpu.VMEM((depth, win, width), table.dtype),
               pltpu.SemaphoreType.DMA((depth,)), pltpu.SemaphoreType.DMA((depth,))]

    @functools.partial(pl.kernel, out_type=out_type, mesh=mesh, scratch_types=scratch)
    def gather(table_hbm, idx_hbm, *refs):
        out_refs = refs[:n_lists]
        idx_v, buf, sem_in, sem_out = refs[n_lists:]
        worker = lax.axis_index("subcore") * info.num_cores + lax.axis_index("core")
        first = worker * wins
        for l in range(n_lists):
            pltpu.sync_copy(idx_hbm.at[l, worker], idx_v.at[l])
        for g in range(0, len(jobs), depth):
            group = jobs[g:g + depth]
            reads = [pltpu.async_copy(table_hbm.at[h].at[idx_v.at[l, j]], buf.at[k], sem_in.at[k])
                     for k, (l, j, h) in enumerate(group)]
            writes = []
            for k, (l, j, h) in enumerate(group):
                reads[k].wait()
                rows = pl.ds(pl.multiple_of((first + j) * win, win), win)
                writes.append(pltpu.async_copy(buf.at[k], out_refs[l].at[h, rows], sem_out.at[k]))
            for write in writes:
                write.wait()

    return gather(table, jnp.stack(idxs).reshape(n_lists, n // (wins * win), wins, win))


def _expert_kernel(te_ref, nu_ref, xs_ref, wg_ref, wu_ref, wd_ref, *refs):
    ys_ref, xb, acc = refs[-3:]
    t, f = pl.program_id(0), pl.program_id(1)
    last = pl.num_programs(1) - 1
    used = t < nu_ref[0]

    @pl.when(used & (f == 0))
    def _():
        xb[...] = _load_split(xs_ref).astype(BF16)
        acc[...] = jnp.zeros_like(acc)

    @pl.when(used)
    def _():
        x = xb[...]
        g = jnp.dot(x, wg_ref[0].astype(BF16), preferred_element_type=F32)
        u = jnp.dot(x, wu_ref[0].astype(BF16), preferred_element_type=F32)
        act = (_silu(g) * u).astype(BF16)
        acc[...] += jnp.dot(act, wd_ref[0].astype(BF16), preferred_element_type=F32)

    @pl.when(used & (f == last))
    def _():
        _store_split(ys_ref, acc[...])

    @pl.when(jnp.logical_not(used) & (f == last))
    def _():
        ys_ref[...] = jnp.zeros_like(ys_ref)


def _expert_call(tile_expert, n_used, xs, wg, wu, wd, ys, tile0, rows_total, *, tf):
    _, rows, quarter = xs.shape
    n_e, d, ff = wg.shape
    tm = MOE_TILE
    nf = ff // tf
    chunk = lambda t, f, te, nu: jnp.where(t < nu[0], f, nf - 1)
    in_specs = [pl.BlockSpec((2, tm, quarter), lambda t, f, te, nu: (0, t, 0)),
                pl.BlockSpec((1, d, tf), lambda t, f, te, nu: (te[t], 0, chunk(t, f, te, nu))),
                pl.BlockSpec((1, d, tf), lambda t, f, te, nu: (te[t], 0, chunk(t, f, te, nu))),
                pl.BlockSpec((1, tf, d), lambda t, f, te, nu: (te[t], chunk(t, f, te, nu), 0))]
    args = [tile_expert, n_used, xs, wg, wu, wd]
    aliases = {}
    if ys is not None:
        in_specs.append(pl.BlockSpec(memory_space=pl.ANY))
        args.append(ys)
        aliases = {len(args) - 1: 0}
    grid_spec = pltpu.PrefetchScalarGridSpec(
        num_scalar_prefetch=2,
        grid=(rows // tm, nf),
        in_specs=in_specs,
        out_specs=pl.BlockSpec((2, tm, quarter), lambda t, f, te, nu: (0, tile0 + t, 0)),
        scratch_shapes=[pltpu.VMEM((tm, d), BF16), pltpu.VMEM((tm, d), F32)])
    return pl.pallas_call(
        _expert_kernel, grid_spec=grid_spec,
        out_shape=jax.ShapeDtypeStruct((2, rows_total, quarter), jnp.int32),
        input_output_aliases=aliases,
        compiler_params=_params(2), name="moe_experts",
    )(*args)


def _moe_route(h2p, sel, gates, wg, wu, wd, *, tf):
    n, n_e = sel.shape
    tm = MOE_TILE
    rank, cnt = _rank_call(sel)
    counts = cnt[0].astype(jnp.int32)
    padded = (counts + tm - 1) // tm * tm
    seg_end = jnp.cumsum(padded)
    seg_start = seg_end - padded
    rows_max = -(-(2 * n + n_e * tm) // GATHER_ROWS) * GATHER_ROWS
    assert rows_max % tm == 0
    lanes = jnp.arange(n_e, dtype=jnp.int32)[None, :]
    e_lo = jnp.min(jnp.where(sel > 0, lanes, n_e - 1), axis=1)
    e_hi = jnp.max(jnp.where(sel > 0, lanes, 0), axis=1)
    pick = lambda a, e: jnp.take_along_axis(a, e[:, None], axis=1)[:, 0]
    rank_i = rank.astype(jnp.int32)
    d_lo = seg_start[e_lo] + pick(rank_i, e_lo)
    d_hi = seg_start[e_hi] + pick(rank_i, e_hi)
    w = jnp.stack([pick(gates, e_lo), pick(gates, e_hi)], axis=1)
    tok = jnp.arange(n, dtype=jnp.int32)
    src = (jnp.arange(rows_max, dtype=jnp.int32) % n).at[d_lo].set(tok).at[d_hi].set(tok)
    n_tiles = rows_max // tm
    n_used = (seg_end[-1] // tm).astype(jnp.int32)
    tile_ids = jnp.minimum(jnp.arange(n_tiles, dtype=jnp.int32), n_used - 1)
    tile_expert = jnp.minimum(jnp.searchsorted(seg_end, tile_ids * tm, side="right"), n_e - 1).astype(jnp.int32)

    step = GATHER_ROWS // tm * MOE_CHUNK
    ys = None
    for t0 in range(0, n_tiles, step):
        t1 = min(t0 + step, n_tiles)
        xs, = _sc_gather(h2p, [src[t0 * tm:t1 * tm]])
        ys = _expert_call(tile_expert[t0:t1], jnp.clip(n_used - t0, 0, t1 - t0).reshape(1), xs, wg, wu, wd,
                          ys, t0, rows_max, tf=tf)
    n_pad = -(-n // GATHER_ROWS) * GATHER_ROWS
    pad_idx = lambda dd: jnp.concatenate([dd.astype(jnp.int32), jnp.arange(n_pad - n, dtype=jnp.int32)])
    ya, yb = _sc_gather(ys, [pad_idx(d_lo), pad_idx(d_hi)])
    return ya, yb, w


def _combine_kernel(ya_ref, yb_ref, w_ref, x1_ref, gt2_ref, gqf_ref, o_ref):
    w = w_ref[...]
    f = w[:, 0:1] * _load_split(ya_ref) + w[:, 1:2] * _load_split(yb_ref)
    o_ref[0] = x1_ref[0] + gt2_ref[0] * _rms(f, gqf_ref[...])


def _combine_call(ya, yb, w, x1, row0, mod, mod_rows, mod_idx, gqf, *, tm, name):
    b, t, d = x1.shape
    ntb = t // tm
    blk0 = row0 // tm
    assert row0 % tm == 0
    pool = lambda i: (blk0 + i, 0)
    pool3 = lambda i: (0, blk0 + i, 0)
    row = lambda i: (i // ntb, i % ntb, 0)
    return pl.pallas_call(
        _combine_kernel,
        grid=(b * ntb,),
        in_specs=[pl.BlockSpec((2, tm, d // 4), pool3), pl.BlockSpec((2, tm, d // 4), pool3),
                  pl.BlockSpec((tm, 2), pool), pl.BlockSpec((1, tm, d), row),
                  _mod_spec(mod_rows, 5, mod_idx), pl.BlockSpec((1, d), lambda i: (0, 0))],
        out_specs=pl.BlockSpec((1, tm, d), row),
        out_shape=jax.ShapeDtypeStruct((b, t, d), F32),
        compiler_params=_params(1), name=name,
    )(ya, yb, w, x1, mod, gqf)


def _rope_tables(pos):
    half = ROT_DIM // 2
    inv_freq = ROPE_THETA ** (-jnp.arange(half, dtype=F32) * 2.0 / ROT_DIM)
    ang = pos.astype(F32)[:, None] * inv_freq[None, :]
    cos, sin = jnp.cos(ang), jnp.sin(ang)
    l64 = np.arange(128) % HEAD_DIM
    idx = l64 % half
    first = (l64 < half)[None, :]
    second = ((l64 >= half) & (l64 < ROT_DIM))[None, :]
    cos_t = jnp.where(first | second, cos[:, idx], 1.0)
    s1_t = jnp.where(first, -sin[:, idx], 0.0)
    s2_t = jnp.where(second, sin[:, idx], 0.0)
    return cos_t, s1_t, s2_t


def kernel(x_prompt, x_sample, cache_k, cache_v, state_conv, c_prompt, c_sample, w_mod, b_mod, g_pre_mix, g_post_mix, g_pre_ffn, g_post_ffn, w_in, conv_w, conv_b, conv_ln_g, conv_ln_b, w_out, ffn_w_gate, ffn_w_up, ffn_w_down, moe_w_router, moe_w_gate, moe_w_up, moe_w_down):
    bp, seq, d = x_prompt.shape
    nb, n_new, _ = x_sample.shape
    depth = w_mod.shape[0]
    w_buf = cache_k.shape[2]
    past_len = PAST_LEN
    assert w_buf == min(DILATION_PATTERNS[-1][0], past_len)
    keep = min(DILATION_PATTERNS[-1][0], seq)
    ns = nb * n_new

    c_all = jnp.concatenate([c_sample, c_prompt, jnp.zeros((MOD_ROWS - nb - bp, d), F32)], axis=0)
    mod_all = _mod_call(c_all, w_mod, b_mod)

    tm_p = 512
    tabs_p = tuple(tb.reshape(seq // tm_p, tm_p, 128) for tb in _rope_tables(jnp.arange(seq, dtype=jnp.int32)))
    tabs_s = tuple(tb.reshape(n_new, 1, 128)
                   for tb in _rope_tables(past_len + jnp.arange(n_new, dtype=jnp.int32)))

    cache_kt = cache_k.transpose(0, 1, 3, 4, 2)
    cache_vt = cache_v.transpose(0, 1, 3, 4, 2)

    yp = x_prompt
    ys = x_sample.transpose(1, 0, 2).reshape(1, ns, d)
    outs = [[] for _ in range(6)]
    for l in range(depth):
        mod_p = mod_all[l].reshape(MOD_ROWS, 1, 6 * d)
        mod_s = jnp.tile(mod_all[l, :nb], (n_new, 1))[None]
        idx_p2 = lambda bi, i: (nb + bi, 0)
        idx_s2 = lambda bi, i: (0, i)
        w_in_bf = w_in[l].astype(BF16)
        w_out_bf = w_out[l].astype(BF16)
        conv_p = (jnp.pad(conv_w[l], ((0, 1), (0, 0))), conv_b[l][None], conv_ln_g[l][None], conv_ln_b[l][None])
        gpre, gpm, gpf, gqf = g_pre_mix[l][None], g_post_mix[l][None], g_pre_ffn[l][None], g_post_ffn[l][None]

        q, k, v, kt, vt, cv, ul = _inproj_call(yp, mod_p, 1, idx_p2, gpre, w_in_bf, tabs_p, tm_p, conv_p,
                                               tm=tm_p, keep=keep, name=f"inproj_prompt_{l}")
        att = _attn_call(q, k, v)
        outs[0].append(kt.reshape(bp, keep, N_HEADS, HEAD_DIM))
        outs[1].append(vt.reshape(bp, keep, N_HEADS, HEAD_DIM))
        outs[2].append(ul[:, CONV_HALO - (CONV_WIDTH - 1):])

        qs, _, _, kts, vts, us = _inproj_call(ys, mod_s, nb, idx_s2, gpre, w_in_bf, tabs_s, 1, None,
                                              tm=nb, keep=ns, name=f"inproj_sample_{l}")
        to_batch_major = lambda z: z.reshape(n_new, nb, N_HEADS, HEAD_DIM).transpose(1, 0, 2, 3)
        q_bm = (qs.reshape(HEAD_PAIRS, n_new, nb, 2, HEAD_DIM).transpose(2, 1, 0, 3, 4)
                .reshape(nb, n_new, N_HEADS, HEAD_DIM))
        k_bm, v_bm = to_batch_major(kts), to_batch_major(vts)
        head_major = lambda z: jnp.pad(z.transpose(0, 2, 1, 3), ((0, 0), (0, 0), (0, SAMPLE_ROWS - n_new), (0, 0)))
        att_s = _sample_attn_call(head_major(q_bm), head_major(k_bm), head_major(v_bm),
                                  cache_kt, cache_vt, l, n_new)
        att_s = (att_s[:, :, :n_new].reshape(nb, HEAD_PAIRS, 2, n_new, HEAD_DIM).transpose(1, 3, 0, 2, 4)
                 .reshape(1, HEAD_PAIRS, ns, 128).astype(BF16))
        ufull = jnp.concatenate([state_conv[l].transpose(1, 0, 2), us.reshape(n_new, nb, 512)], axis=0)
        cv_s = _sample_conv_call(ufull, conv_p).reshape(1, ns, 512)
        outs[3].append(k_bm)
        outs[4].append(v_bm)
        outs[5].append(ufull[n_new:].transpose(1, 0, 2))

        idx_p1 = lambda i, *_: (nb + i // (seq // tm_f), 0)
        idx_s1 = lambda i, *_: (0, i)
        if l % 2 == 0:
            tm_f = 1024
            wg, wu, wd = ffn_w_gate[l // 2], ffn_w_up[l // 2], ffn_w_down[l // 2]
            yp = _ffn_call(att, cv, yp, mod_p, 1, idx_p1, gpm, gpf, gqf, w_out_bf, wg, wu, wd,
                           tm=tm_f, tf=256, name=f"ffn_prompt_{l}")
            ys = _ffn_call(att_s, cv_s, ys, mod_s, ns, idx_s1, gpm, gpf, gqf, w_out_bf, wg, wu, wd,
                           tm=ns, tf=512, name=f"ffn_sample_{l}")
        else:
            tm_f = 1024
            wr = moe_w_router[l // 2].T
            wg, wu, wd = moe_w_gate[l // 2], moe_w_up[l // 2], moe_w_down[l // 2]
            x1, h2, gates, sel = _router_call(att, cv, yp, mod_p, 1, idx_p1, gpm, gpf, w_out_bf, wr,
                                              tm=tm_f, name=f"router_prompt_{l}")
            x1s, h2s, gates_s, sel_s = _router_call(att_s, cv_s, ys, mod_s, ns, idx_s1, gpm, gpf, w_out_bf, wr,
                                                    tm=ns, name=f"router_sample_{l}")
            pool = lambda a, b_: jnp.concatenate([a.reshape(bp * seq, -1), b_.reshape(ns, -1)], axis=0)
            h2_pool = jnp.concatenate([h2.reshape(2, bp * seq, d // 4), h2s.reshape(2, ns, d // 4)], axis=1)
            ya, yb, w12 = _moe_route(h2_pool, pool(sel, sel_s), pool(gates, gates_s), wg, wu, wd, tf=512)
            yp = _combine_call(ya, yb, w12, x1, 0, mod_p, 1, idx_p1, gqf, tm=tm_f, name=f"moe_combine_prompt_{l}")
            ys = _combine_call(ya, yb, w12, x1s, bp * seq, mod_s, ns, idx_s1, gqf, tm=ns,
                               name=f"moe_combine_sample_{l}")

    y_sample = ys.reshape(n_new, nb, d).transpose(1, 0, 2)
    return (yp, y_sample) + tuple(jnp.stack(o) for o in outs)
```

```python
import functools

import numpy as np
import jax
import jax.numpy as jnp
from jax import lax
from jax.experimental import pallas as pl
from jax.experimental.pallas import tpu as pltpu
from jax.experimental.pallas import tpu_sc as plsc

F32 = jnp.float32
BF16 = jnp.bfloat16

HEAD_DIM = 64
N_HEADS = 8
ATT_WIDTH = N_HEADS * HEAD_DIM
HEAD_PAIRS = ATT_WIDTH // 128
CONV_WIDTH = 31
CONV_HALO = 32
DILATION_PATTERNS = ((128, 1), (512, 4), (2048, 16))
BAND = 128
ROT_DIM = HEAD_DIM // 4
ROPE_THETA = 500000.0
ATTN_SCALE = HEAD_DIM ** -0.5
N_EXPERTS = 8
PAST_LEN = 2048
EPS = 1e-6
NEG = -1e30
MOD_ROWS = 136
VMEM_LIMIT = 56 * 1024 * 1024


def _rms(x, g):
    return x * lax.rsqrt(jnp.mean(x * x, axis=-1, keepdims=True) + EPS) * g


def _silu(x):
    return x * jax.nn.sigmoid(x)


def _params(n_axes, vmem=VMEM_LIMIT):
    return pltpu.CompilerParams(dimension_semantics=("arbitrary",) * n_axes, vmem_limit_bytes=vmem)


def _mod_kernel(c_ref, w_ref, b_ref, o_ref):
    a = _silu(c_ref[...]).astype(BF16)
    o_ref[0] = jnp.dot(a, w_ref[0].astype(BF16), preferred_element_type=F32) + b_ref[0]


def _mod_call(c_all, w_mod, b_mod):
    depth, d, d6 = w_mod.shape
    return pl.pallas_call(
        _mod_kernel,
        grid=(depth, d6 // d),
        in_specs=[pl.BlockSpec((MOD_ROWS, d), lambda l, j: (0, 0)),
                  pl.BlockSpec((1, d, d), lambda l, j: (l, 0, j)),
                  pl.BlockSpec((1, 1, d), lambda l, j: (l, 0, j))],
        out_specs=pl.BlockSpec((1, MOD_ROWS, d), lambda l, j: (l, 0, j)),
        out_shape=jax.ShapeDtypeStruct((depth, MOD_ROWS, d6), F32),
        compiler_params=_params(2),
        name="adaln_mod",
    )(c_all, w_mod, b_mod.reshape(depth, 1, d6))


def _inproj_kernel(*refs, conv, tm):
    if conv:
        (x_ref, sh_ref, sc_ref, g_ref, w_ref, cos_ref, s1_ref, s2_ref,
         cw_ref, cb_ref, lg_ref, lb_ref,
         q_ref, k_ref, v_ref, kt_ref, vt_ref, cv_ref, ul_ref, ubuf) = refs
    else:
        (x_ref, sh_ref, sc_ref, g_ref, w_ref, cos_ref, s1_ref, s2_ref,
         q_ref, k_ref, v_ref, kt_ref, vt_ref, u_ref) = refs

    h = _rms(x_ref[0], g_ref[...]) * (1.0 + sc_ref[0]) + sh_ref[0]
    proj = jnp.dot(h.astype(BF16), w_ref[...], preferred_element_type=F32)
    cos, s1, s2 = cos_ref[0], s1_ref[0], s2_ref[0]

    def rope(z):
        return z * cos + pltpu.roll(z, 128 - ROT_DIM // 2, 1) * s1 + pltpu.roll(z, ROT_DIM // 2, 1) * s2

    for hp in range(HEAD_PAIRS):
        lo, hi = hp * 128, (hp + 1) * 128
        q_ref[0, hp] = rope(proj[:, lo:hi]) * ATTN_SCALE
        kz = rope(proj[:, ATT_WIDTH + lo:ATT_WIDTH + hi])
        k_ref[0, hp] = kz
        kt_ref[0, :, lo:hi] = kz
        vz = proj[:, 2 * ATT_WIDTH + lo:2 * ATT_WIDTH + hi]
        v_ref[0, hp] = vz
        vt_ref[0, :, lo:hi] = vz

    a = proj[:, 3 * ATT_WIDTH:3 * ATT_WIDTH + 512]
    gate = proj[:, 3 * ATT_WIDTH + 512:]
    u = a * jax.nn.sigmoid(gate)
    if not conv:
        u_ref[0] = u
        return

    @pl.when(pl.program_id(1) == 0)
    def _():
        ubuf[0, 0:CONV_HALO, :] = jnp.zeros((CONV_HALO, 512), F32)

    ubuf[0, CONV_HALO:CONV_HALO + tm, :] = u
    span = tm + CONV_HALO - 8
    for s in range(1, 8):
        ubuf[s, 0:span, :] = ubuf[0, s:s + span, :]
    off0 = CONV_HALO - (CONV_WIDTH - 1)
    rows = 32

    def chunk(r, carry):
        base = pl.multiple_of(r * rows, rows)
        acc = jnp.broadcast_to(cb_ref[...], (rows, 512))
        for j in range(CONV_WIDTH):
            a, s = divmod(off0 + j, 8)
            acc = acc + cw_ref[j:j + 1, :] * ubuf[s, pl.ds(base + 8 * a, rows), :]
        mu = jnp.mean(acc, axis=-1, keepdims=True)
        xc = acc - mu
        var = jnp.mean(xc * xc, axis=-1, keepdims=True)
        y = xc * lax.rsqrt(var + EPS) * lg_ref[...] + lb_ref[...]
        cv_ref[0, pl.ds(base, rows), :] = _silu(y).astype(BF16)
        return carry

    lax.fori_loop(0, tm // rows, chunk, 0, unroll=4)
    tail = ubuf[0, tm:tm + CONV_HALO, :]
    ul_ref[0] = tail
    ubuf[0, 0:CONV_HALO, :] = tail


def _mod_spec(rows, chunk, index_fn):
    return pl.BlockSpec((1, rows, 1024), lambda *g: index_fn(*g) + (chunk,))


def _inproj_call(x, mod, mod_rows, mod_idx, g, w_bf, tabs, tab_rows, conv_p, *, tm, keep, name):
    b, t, d = x.shape
    nt = t // tm
    off = (t - keep) // tm
    conv = conv_p is not None
    tab_spec = pl.BlockSpec((1, tab_rows, 128), lambda bi, i: (i, 0, 0))
    in_specs = [pl.BlockSpec((1, tm, d), lambda bi, i: (bi, i, 0)),
                _mod_spec(mod_rows, 0, mod_idx), _mod_spec(mod_rows, 1, mod_idx),
                pl.BlockSpec((1, d), lambda bi, i: (0, 0)),
                pl.BlockSpec(w_bf.shape, lambda bi, i: (0, 0)),
                tab_spec, tab_spec, tab_spec]
    args = [x, mod, mod, g, w_bf, *tabs]
    hp_spec = pl.BlockSpec((1, HEAD_PAIRS, tm, 128), lambda bi, i: (bi, 0, i, 0))
    tail_spec = pl.BlockSpec((1, tm, 512), lambda bi, i: (bi, jnp.maximum(i - off, 0), 0))
    hp_shape = jax.ShapeDtypeStruct((b, HEAD_PAIRS, t, 128), F32)
    tail_shape = jax.ShapeDtypeStruct((b, keep, 512), F32)
    out_specs = [hp_spec, hp_spec, hp_spec, tail_spec, tail_spec]
    out_shape = [hp_shape, hp_shape, hp_shape, tail_shape, tail_shape]
    scratch = []
    if conv:
        small = pl.BlockSpec((1, 512), lambda bi, i: (0, 0))
        in_specs += [pl.BlockSpec((32, 512), lambda bi, i: (0, 0)), small, small, small]
        args += list(conv_p)
        out_specs += [pl.BlockSpec((1, tm, 512), lambda bi, i: (bi, i, 0)),
                      pl.BlockSpec((1, CONV_HALO, 512), lambda bi, i: (bi, 0, 0))]
        out_shape += [jax.ShapeDtypeStruct((b, t, 512), BF16),
                      jax.ShapeDtypeStruct((b, CONV_HALO, 512), F32)]
        scratch = [pltpu.VMEM((8, tm + CONV_HALO, 512), F32)]
    else:
        out_specs += [pl.BlockSpec((1, tm, 512), lambda bi, i: (bi, i, 0))]
        out_shape += [jax.ShapeDtypeStruct((b, t, 512), F32)]
    return pl.pallas_call(
        functools.partial(_inproj_kernel, conv=conv, tm=tm),
        grid=(b, nt), in_specs=in_specs, out_specs=out_specs, out_shape=out_shape,
        scratch_shapes=scratch, compiler_params=_params(2), name=name,
    )(*args)


def _attn_kernel(q_ref, kc_ref, kp_ref, vc_ref, vp_ref, o_ref, acc_s, m_s, l_s, *, tq):
    first_kj = jnp.where(pl.program_id(2) == 0, 128, 0)
    lane_lo = lax.broadcasted_iota(jnp.int32, (128, 128), 1) < HEAD_DIM
    qi = lax.broadcasted_iota(jnp.int32, (256, 256), 0)
    qi = jnp.where(qi >= 128, qi - 128, qi)
    kj = lax.broadcasted_iota(jnp.int32, (256, 256), 1)
    band = (kj >= qi) & (kj <= qi + BAND)
    band_first = band & (kj >= first_kj)

    ones_cols = jnp.ones((256, 128), BF16)
    nt = (((1,), (1,)), ((), ()))

    def strided(start, size, d):
        if d > 1:
            return pl.ds(start, size, stride=d)
        return pl.ds(start if isinstance(start, int) else pl.multiple_of(start, 128), size)

    def blocks(items):
        loaded = []
        for rows, first, c, d, s in items:
            if first:
                take = lambda rp, rc: jnp.concatenate(
                    [rp[0, 0, strided(tq - BAND * d + c, 128, d), :],
                     rc[0, 0, strided(c, 128, d), :]], axis=0)
            else:
                krows = strided(c + d * 128 * (s - 1), 256, d)
                take = lambda rp, rc: rc[0, 0, krows, :]
            loaded.append((q_ref[0, 0, rows, :], take(kp_ref, kc_ref), take(vp_ref, vc_ref),
                           band_first if first else band))
        scores = []
        for q2, k2, _, _ in loaded:
            qq = jnp.concatenate([jnp.where(lane_lo, q2, 0.0), jnp.where(lane_lo, 0.0, q2)], axis=0)
            scores.append(lax.dot_general(qq.astype(BF16), k2.astype(BF16), nt,
                                          preferred_element_type=F32))
        probs = []
        for sc, (_, _, _, mask) in zip(scores, loaded):
            sc = jnp.where(mask, sc, NEG)
            mb = jnp.max(sc, axis=1, keepdims=True)
            probs.append((mb, jnp.exp(sc - mb).astype(BF16)))
        out = []
        for (mb, p), (_, _, v2, _), item in zip(probs, loaded, items):
            pv = jnp.dot(p, jnp.concatenate([v2.astype(BF16), ones_cols], axis=1), preferred_element_type=F32)
            out.append((item[0], jnp.where(lane_lo, mb[:128], mb[128:]),
                        jnp.where(lane_lo, pv[:128, 128:], pv[128:, 128:]),
                        jnp.where(lane_lo, pv[:128, :128], pv[128:, :128])))
        return out

    def merge(rows, mt, lt, pvt):
        mp, lp, ap = m_s[rows, :], l_s[rows, :], acc_s[rows, :]
        mn = jnp.maximum(mp, mt)
        a = jnp.exp(mp - mn)
        b = jnp.exp(mt - mn)
        return mn, a * lp + b * lt, a * ap + b * pvt

    def store_init(results):
        for rows, mt, lt, pvt in results:
            m_s[rows, :] = mt
            l_s[rows, :] = lt
            acc_s[rows, :] = pvt

    def store_merged(results):
        for rows, mt, lt, pvt in results:
            mn, ln, an = merge(rows, mt, lt, pvt)
            m_s[rows, :] = mn
            l_s[rows, :] = ln
            acc_s[rows, :] = an

    def store_output(results):
        for rows, mt, lt, pvt in results:
            _, ln, an = merge(rows, mt, lt, pvt)
            o_ref[0, 0, rows, :] = (an / ln).astype(o_ref.dtype)

    def loop(n, fn):
        def body(i, carry):
            fn(i)
            return carry
        lax.fori_loop(0, n, body, 0)

    group = 4
    loop(16 // group, lambda g: store_init(blocks(
        [(pl.ds(g * group + j, 128, stride=16), True, g * group + j, 16, 0) for j in range(group)])))

    store_merged(blocks([(pl.ds(c, 128, stride=4), True, c, 4, 0) for c in range(4)]))
    loop(tq // 512 - 1, lambda n: store_merged(blocks(
        [(pl.ds(c + 512 * (n + 1), 128, stride=4), False, c, 4, n + 1) for c in range(4)])))

    store_output(blocks([(strided(0, 128, 1), True, 0, 1, 0)]
                        + [(strided(128 * s, 128, 1), False, 0, 1, s) for s in range(1, group)]))
    rest = 4
    assert (tq // 128 - group) % rest == 0
    loop((tq // 128 - group) // rest, lambda g: store_output(blocks(
        [(strided(128 * (group + g * rest + j), 128, 1), False, 0, 1, group + g * rest + j) for j in range(rest)])))


def _attn_call(q, k, v, *, tq=2048):
    b, hp, t, _ = q.shape
    assert tq == BAND * 16 and t % tq == 0
    cur = pl.BlockSpec((1, 1, tq, 128), lambda bi, h, i: (bi, h, i, 0))
    prev = pl.BlockSpec((1, 1, tq, 128), lambda bi, h, i: (bi, h, jnp.maximum(i - 1, 0), 0))
    return pl.pallas_call(
        functools.partial(_attn_kernel, tq=tq),
        grid=(b, hp, t // tq),
        in_specs=[cur, cur, prev, cur, prev],
        out_specs=cur,
        out_shape=jax.ShapeDtypeStruct((b, hp, t, 128), BF16),
        scratch_shapes=[pltpu.VMEM((tq, 128), F32)] * 3,
        compiler_params=_params(3), name="dilated_attn_prompt",
    )(q, k, k, v, v)


SAMPLE_ROWS = 8


def _sample_attn_kernel(q_ref, kn_ref, vn_ref, kt_ref, vt_ref, mh_ref, mn_ref, o_ref):
    mh, mn = mh_ref[...], mn_ref[...]
    nt = (((1,), (1,)), ((), ()))
    for h in range(N_HEADS):
        qh = q_ref[0, h].astype(BF16)
        s = jnp.dot(qh, kt_ref[0, 0, h].astype(BF16), preferred_element_type=F32)
        sn = lax.dot_general(qh, kn_ref[0, h].astype(BF16), nt, preferred_element_type=F32)
        s = jnp.where(mh > 0, s, NEG)
        sn = jnp.where(mn > 0, sn, NEG)
        m = jnp.maximum(jnp.max(s, axis=1, keepdims=True), jnp.max(sn, axis=1, keepdims=True))
        p = mh * jnp.exp(s - m)
        pn = mn * jnp.exp(sn - m)
        den = jnp.sum(p, axis=1, keepdims=True) + jnp.sum(pn, axis=1, keepdims=True)
        num = (lax.dot_general(p.astype(BF16), vt_ref[0, 0, h].astype(BF16), nt, preferred_element_type=F32)
               + jnp.dot(pn.astype(BF16), vn_ref[0, h].astype(BF16), preferred_element_type=F32))
        o_ref[0, h] = num / jnp.where(den > 0, den, 1.0)


def _sample_tables(w_buf, n_new):
    rows = np.arange(w_buf + n_new)
    mult = np.zeros((SAMPLE_ROWS, w_buf + SAMPLE_ROWS), np.float32)
    for t in range(n_new):
        dist = w_buf + t - rows
        for window, dil in DILATION_PATTERNS:
            mult[t, :w_buf + n_new] += (dist >= 0) & (dist <= window) & (dist % dil == 0)
    return jnp.asarray(mult[:, :w_buf]), jnp.asarray(mult[:, w_buf:])


def _sample_attn_call(q, kn, vn, cache_kt, cache_vt, layer, n_new):
    nb = q.shape[0]
    w_buf = cache_kt.shape[-1]
    mh, mn = _sample_tables(w_buf, n_new)
    new_spec = pl.BlockSpec((1, N_HEADS, SAMPLE_ROWS, HEAD_DIM), lambda i: (i, 0, 0, 0))
    cache_spec = pl.BlockSpec((1, 1, N_HEADS, HEAD_DIM, w_buf), lambda i: (layer, i, 0, 0, 0))
    return pl.pallas_call(
        _sample_attn_kernel,
        grid=(nb,),
        in_specs=[new_spec, new_spec, new_spec, cache_spec, cache_spec,
                  pl.BlockSpec(mh.shape, lambda i: (0, 0)), pl.BlockSpec(mn.shape, lambda i: (0, 0))],
        out_specs=new_spec,
        out_shape=jax.ShapeDtypeStruct((nb, N_HEADS, SAMPLE_ROWS, HEAD_DIM), F32),
        compiler_params=_params(1), name="dilated_attn_sample",
    )(q, kn, vn, cache_kt, cache_vt, mh, mn)


def _sample_conv_kernel(uf_ref, cw_ref, cb_ref, lg_ref, lb_ref, o_ref, *, n_new, nb):
    rows = 32

    def body(n, carry):
        t = n // (nb // rows)
        base = pl.multiple_of((n % (nb // rows)) * rows, rows)
        acc = jnp.broadcast_to(cb_ref[...], (rows, 512))
        for j in range(CONV_WIDTH):
            acc = acc + cw_ref[j:j + 1, :] * uf_ref[t + j, pl.ds(base, rows), :]
        mu = jnp.mean(acc, axis=-1, keepdims=True)
        xc = acc - mu
        var = jnp.mean(xc * xc, axis=-1, keepdims=True)
        y = xc * lax.rsqrt(var + EPS) * lg_ref[...] + lb_ref[...]
        o_ref[t, pl.ds(base, rows), :] = _silu(y).astype(BF16)
        return carry

    lax.fori_loop(0, n_new * (nb // rows), body, 0)


def _sample_conv_call(ufull, conv_p):
    n_full, nb, ch = ufull.shape
    n_new = n_full - (CONV_WIDTH - 1)
    return pl.pallas_call(
        functools.partial(_sample_conv_kernel, n_new=n_new, nb=nb),
        out_shape=jax.ShapeDtypeStruct((n_new, nb, ch), BF16),
        compiler_params=pltpu.CompilerParams(vmem_limit_bytes=VMEM_LIMIT),
        name="conv_sample",
    )(ufull, *conv_p)


def _mix_residual(att_ref, cv_ref, x_ref, gt1_ref, sc2_ref, sh2_ref, gpm_ref, gpf_ref, wo_ref):
    a = jnp.concatenate([att_ref[0, hp] for hp in range(HEAD_PAIRS)] + [cv_ref[0]], axis=-1)
    mix = jnp.dot(a, wo_ref[...], preferred_element_type=F32)
    x1 = x_ref[0] + gt1_ref[0] * _rms(mix, gpm_ref[...])
    h2 = _rms(x1, gpf_ref[...]) * (1.0 + sc2_ref[0]) + sh2_ref[0]
    return x1, h2


def _ffn_kernel(att_ref, cv_ref, x_ref, gt1_ref, sc2_ref, sh2_ref, gt2_ref, gpm_ref, gpf_ref, gqf_ref,
                wo_ref, wg_ref, wu_ref, wd_ref, o_ref, x1_s, h2_s, acc_s):
    f = pl.program_id(1)

    @pl.when(f == 0)
    def _():
        x1, h2 = _mix_residual(att_ref, cv_ref, x_ref, gt1_ref, sc2_ref, sh2_ref, gpm_ref, gpf_ref, wo_ref)
        x1_s[...] = x1
        h2_s[...] = h2.astype(BF16)
        acc_s[...] = jnp.zeros_like(acc_s)

    h2 = h2_s[...]
    g = jnp.dot(h2, wg_ref[...].astype(BF16), preferred_element_type=F32)
    u = jnp.dot(h2, wu_ref[...].astype(BF16), preferred_element_type=F32)
    act = (_silu(g) * u).astype(BF16)
    acc_s[...] += jnp.dot(act, wd_ref[...].astype(BF16), preferred_element_type=F32)

    @pl.when(f == pl.num_programs(1) - 1)
    def _():
        o_ref[0] = x1_s[...] + gt2_ref[0] * _rms(acc_s[...], gqf_ref[...])


def _mix_in_specs(b, t, tm, mod_rows, mod_idx, d):
    ntb = t // tm
    row = lambda i, *_: (i // ntb, i % ntb, 0)
    specs = [pl.BlockSpec((1, HEAD_PAIRS, tm, 128), lambda i, *_: (i // ntb, 0, i % ntb, 0)),
             pl.BlockSpec((1, tm, 512), row),
             pl.BlockSpec((1, tm, d), row)]
    return specs, row


def _ffn_call(att, cv, x, mod, mod_rows, mod_idx, gpm, gpf, gqf, wo_bf, wg, wu, wd, *, tm, tf, name):
    b, t, d = x.shape
    ff = wg.shape[1]
    ntb = t // tm
    specs, row = _mix_in_specs(b, t, tm, mod_rows, mod_idx, d)
    vec = pl.BlockSpec((1, d), lambda i, f: (0, 0))
    in_specs = specs + [_mod_spec(mod_rows, 2, mod_idx), _mod_spec(mod_rows, 4, mod_idx),
                        _mod_spec(mod_rows, 3, mod_idx), _mod_spec(mod_rows, 5, mod_idx),
                        vec, vec, vec,
                        pl.BlockSpec((d, d), lambda i, f: (0, 0)),
                        pl.BlockSpec((d, tf), lambda i, f: (0, f)),
                        pl.BlockSpec((d, tf), lambda i, f: (0, f)),
                        pl.BlockSpec((tf, d), lambda i, f: (f, 0))]
    return pl.pallas_call(
        _ffn_kernel,
        grid=(b * ntb, ff // tf),
        in_specs=in_specs,
        out_specs=pl.BlockSpec((1, tm, d), row),
        out_shape=jax.ShapeDtypeStruct((b, t, d), F32),
        scratch_shapes=[pltpu.VMEM((tm, d), F32), pltpu.VMEM((tm, d), BF16), pltpu.VMEM((tm, d), F32)],
        compiler_params=_params(2), name=name,
    )(att, cv, x, mod, mod, mod, mod, gpm, gpf, gqf, wo_bf, wg, wu, wd)


def _router_kernel(att_ref, cv_ref, x_ref, gt1_ref, sc2_ref, sh2_ref, gpm_ref, gpf_ref, wo_ref, wr_ref,
                   x1_ref, h2_ref, gate_ref, sel_ref):
    x1, h2 = _mix_residual(att_ref, cv_ref, x_ref, gt1_ref, sc2_ref, sh2_ref, gpm_ref, gpf_ref, wo_ref)
    x1_ref[0] = x1
    _store_split(h2_ref.at[:, 0], h2)
    logits = jnp.concatenate([jnp.sum(h2 * wr_ref[e:e + 1, :], axis=1, keepdims=True)
                              for e in range(N_EXPERTS)], axis=1)
    lane = lax.broadcasted_iota(jnp.int32, logits.shape, 1).astype(F32)
    v1 = jnp.max(logits, axis=-1, keepdims=True)
    i1 = jnp.min(jnp.where(logits == v1, lane, float(N_EXPERTS)), axis=-1, keepdims=True)
    oh1 = lane == i1
    rest = jnp.where(oh1, -jnp.inf, logits)
    v2 = jnp.max(rest, axis=-1, keepdims=True)
    i2 = jnp.min(jnp.where(rest == v2, lane, float(N_EXPERTS)), axis=-1, keepdims=True)
    oh2 = lane == i2
    e2 = jnp.exp(v2 - v1)
    den = 1.0 + e2
    gate_ref[0] = jnp.where(oh1, 1.0 / den, 0.0) + jnp.where(oh2, e2 / den, 0.0)
    sel_ref[0] = jnp.where(oh1 | oh2, 1.0, 0.0)


def _router_call(att, cv, x, mod, mod_rows, mod_idx, gpm, gpf, wo_bf, wr, *, tm, name):
    b, t, d = x.shape
    specs, row = _mix_in_specs(b, t, tm, mod_rows, mod_idx, d)
    vec = pl.BlockSpec((1, d), lambda i: (0, 0))
    in_specs = specs + [_mod_spec(mod_rows, 2, mod_idx), _mod_spec(mod_rows, 4, mod_idx),
                        _mod_spec(mod_rows, 3, mod_idx), vec, vec,
                        pl.BlockSpec((d, d), lambda i: (0, 0)),
                        pl.BlockSpec((N_EXPERTS, d), lambda i: (0, 0))]
    return pl.pallas_call(
        _router_kernel,
        grid=(b * (t // tm),),
        in_specs=in_specs,
        out_specs=[pl.BlockSpec((1, tm, d), row),
                   pl.BlockSpec((2, 1, tm, d // 4), lambda i: (0,) + row(i)),
                   pl.BlockSpec((1, tm, N_EXPERTS), row), pl.BlockSpec((1, tm, N_EXPERTS), row)],
        out_shape=[jax.ShapeDtypeStruct((b, t, d), F32), jax.ShapeDtypeStruct((2, b, t, d // 4), jnp.int32),
                   jax.ShapeDtypeStruct((b, t, N_EXPERTS), F32), jax.ShapeDtypeStruct((b, t, N_EXPERTS), F32)],
        compiler_params=_params(1), name=name,
    )(att, cv, x, mod, mod, mod, gpm, gpf, wo_bf, wr)


MOE_TILE = 1024
RANK_BLOCK = 1536
GATHER_WINDOW = 128
GATHER_ROWS = 32 * GATHER_WINDOW
MOE_CHUNK = 3
GATHER_DEPTH = 3


def _pack_pairs(x):
    w = x.shape[1] // 2
    lo = pltpu.bitcast(x[:, :w].astype(BF16).astype(F32), jnp.int32)
    hi = pltpu.bitcast(x[:, w:].astype(BF16).astype(F32), jnp.int32)
    return hi | lax.shift_right_logical(lo, 16)


def _unpack_pairs(words):
    lo = pltpu.bitcast(lax.shift_left(words, 16), F32)
    hi = pltpu.bitcast(words & -65536, F32)
    return jnp.concatenate([lo, hi], axis=1)


def _store_split(ref, x):
    words = _pack_pairs(x)
    q = words.shape[1] // 2
    ref[0] = words[:, :q]
    ref[1] = words[:, q:]


def _load_split(ref):
    return _unpack_pairs(jnp.concatenate([ref[0], ref[1]], axis=1))


def _rank_kernel(sel_ref, rank_ref, cnt_ref, carry):
    @pl.when(pl.program_id(0) == 0)
    def _():
        carry[...] = jnp.zeros_like(carry)

    sel = sel_ref[...]
    rb = sel.shape[0]
    before = (lax.broadcasted_iota(jnp.int32, (rb, rb), 1)
              < lax.broadcasted_iota(jnp.int32, (rb, rb), 0)).astype(BF16)
    rank_ref[...] = jnp.dot(before, sel.astype(BF16), preferred_element_type=F32) + carry[...]
    carry[...] += jnp.sum(sel, axis=0, keepdims=True)
    cnt_ref[...] = carry[...]


def _rank_call(sel):
    n, n_e = sel.shape
    rb = RANK_BLOCK
    assert n % rb == 0
    return pl.pallas_call(
        _rank_kernel,
        grid=(n // rb,),
        in_specs=[pl.BlockSpec((rb, n_e), lambda i: (i, 0))],
        out_specs=[pl.BlockSpec((rb, n_e), lambda i: (i, 0)), pl.BlockSpec((1, n_e), lambda i: (0, 0))],
        out_shape=[jax.ShapeDtypeStruct((n, n_e), F32), jax.ShapeDtypeStruct((1, n_e), F32)],
        scratch_shapes=[pltpu.VMEM((1, n_e), F32)],
        compiler_params=_params(1), name="moe_rank",
    )(sel)


def _sc_gather(table, idxs):
    halves, _, width = table.shape
    n_lists, n = len(idxs), idxs[0].shape[0]
    info = plsc.get_sparse_core_info()
    win, depth = GATHER_WINDOW, GATHER_DEPTH
    assert n % GATHER_ROWS == 0 and GATHER_ROWS == info.num_cores * info.num_subcores * win
    wins = n // GATHER_ROWS
    jobs = [(l, j, h) for l in range(n_lists) for j in range(wins) for h in range(halves)]
    mesh = plsc.VectorSubcoreMesh(core_axis_name="core", subcore_axis_name="subcore")
    out_type = [jax.ShapeDtypeStruct((halves, n, width), table.dtype) for _ in idxs]
    scratch = [pltpu.VMEM((n_lists, wins, win), jnp.int32), pltpu.VMEM((depth, win, width), table.dtype),
               pltpu.SemaphoreType.DMA((depth,)), pltpu.SemaphoreType.DMA((depth,))]

    @functools.partial(pl.kernel, out_type=out_type, mesh=mesh, scratch_types=scratch)
    def gather(table_hbm, idx_hbm, *refs):
        out_refs = refs[:n_lists]
        idx_v, buf, sem_in, sem_out = refs[n_lists:]
        worker = lax.axis_index("subcore") * info.num_cores + lax.axis_index("core")
        first = worker * wins
        for l in range(n_lists):
            pltpu.sync_copy(idx_hbm.at[l, worker], idx_v.at[l])
        for g in range(0, len(jobs), depth):
            group = jobs[g:g + depth]
            reads = [pltpu.async_copy(table_hbm.at[h].at[idx_v.at[l, j]], buf.at[k], sem_in.at[k])
                     for k, (l, j, h) in enumerate(group)]
            writes = []
            for k, (l, j, h) in enumerate(group):
                reads[k].wait()
                rows = pl.ds(pl.multiple_of((first + j) * win, win), win)
                writes.append(pltpu.async_copy(buf.at[k], out_refs[l].at[h, rows], sem_out.at[k]))
            for write in writes:
                write.wait()

    return gather(table, jnp.stack(idxs).reshape(n_lists, n // (wins * win), wins, win))


def _expert_kernel(te_ref, nu_ref, xs_ref, wg_ref, wu_ref, wd_ref, *refs):
    ys_ref, xb, acc = refs[-3:]
    t, f = pl.program_id(0), pl.program_id(1)
    last = pl.num_programs(1) - 1
    used = t < nu_ref[0]

    @pl.when(used & (f == 0))
    def _():
        xb[...] = _load_split(xs_ref).astype(BF16)
        acc[...] = jnp.zeros_like(acc)

    @pl.when(used)
    def _():
        x = xb[...]
        g = jnp.dot(x, wg_ref[0].astype(BF16), preferred_element_type=F32)
        u = jnp.dot(x, wu_ref[0].astype(BF16), preferred_element_type=F32)
        act = (_silu(g) * u).astype(BF16)
        acc[...] += jnp.dot(act, wd_ref[0].astype(BF16), preferred_element_type=F32)

    @pl.when(used & (f == last))
    def _():
        _store_split(ys_ref, acc[...])

    @pl.when(jnp.logical_not(used) & (f == last))
    def _():
        ys_ref[...] = jnp.zeros_like(ys_ref)


def _expert_call(tile_expert, n_used, xs, wg, wu, wd, ys, tile0, rows_total, *, tf):
    _, rows, quarter = xs.shape
    n_e, d, ff = wg.shape
    tm = MOE_TILE
    nf = ff // tf
    chunk = lambda t, f, te, nu: jnp.where(t < nu[0], f, nf - 1)
    in_specs = [pl.BlockSpec((2, tm, quarter), lambda t, f, te, nu: (0, t, 0)),
                pl.BlockSpec((1, d, tf), lambda t, f, te, nu: (te[t], 0, chunk(t, f, te, nu))),
                pl.BlockSpec((1, d, tf), lambda t, f, te, nu: (te[t], 0, chunk(t, f, te, nu))),
                pl.BlockSpec((1, tf, d), lambda t, f, te, nu: (te[t], chunk(t, f, te, nu), 0))]
    args = [tile_expert, n_used, xs, wg, wu, wd]
    aliases = {}
    if ys is not None:
        in_specs.append(pl.BlockSpec(memory_space=pl.ANY))
        args.append(ys)
        aliases = {len(args) - 1: 0}
    grid_spec = pltpu.PrefetchScalarGridSpec(
        num_scalar_prefetch=2,
        grid=(rows // tm, nf),
        in_specs=in_specs,
        out_specs=pl.BlockSpec((2, tm, quarter), lambda t, f, te, nu: (0, tile0 + t, 0)),
        scratch_shapes=[pltpu.VMEM((tm, d), BF16), pltpu.VMEM((tm, d), F32)])
    return pl.pallas_call(
        _expert_kernel, grid_spec=grid_spec,
        out_shape=jax.ShapeDtypeStruct((2, rows_total, quarter), jnp.int32),
        input_output_aliases=aliases,
        compiler_params=_params(2), name="moe_experts",
    )(*args)


def _moe_route(h2p, sel, gates, wg, wu, wd, *, tf):
    n, n_e = sel.shape
    tm = MOE_TILE
    rank, cnt = _rank_call(sel)
    counts = cnt[0].astype(jnp.int32)
    padded = (counts + tm - 1) // tm * tm
    seg_end = jnp.cumsum(padded)
    seg_start = seg_end - padded
    rows_max = -(-(2 * n + n_e * tm) // GATHER_ROWS) * GATHER_ROWS
    assert rows_max % tm == 0
    lanes = jnp.arange(n_e, dtype=jnp.int32)[None, :]
    e_lo = jnp.min(jnp.where(sel > 0, lanes, n_e - 1), axis=1)
    e_hi = jnp.max(jnp.where(sel > 0, lanes, 0), axis=1)
    pick = lambda a, e: jnp.take_along_axis(a, e[:, None], axis=1)[:, 0]
    rank_i = rank.astype(jnp.int32)
    d_lo = seg_start[e_lo] + pick(rank_i, e_lo)
    d_hi = seg_start[e_hi] + pick(rank_i, e_hi)
    w = jnp.stack([pick(gates, e_lo), pick(gates, e_hi)], axis=1)
    tok = jnp.arange(n, dtype=jnp.int32)
    src = (jnp.arange(rows_max, dtype=jnp.int32) % n).at[d_lo].set(tok).at[d_hi].set(tok)
    n_tiles = rows_max // tm
    n_used = (seg_end[-1] // tm).astype(jnp.int32)
    tile_ids = jnp.minimum(jnp.arange(n_tiles, dtype=jnp.int32), n_used - 1)
    tile_expert = jnp.minimum(jnp.sum(seg_end[None, :] <= (tile_ids * tm)[:, None], axis=1), n_e - 1).astype(jnp.int32)

    step = GATHER_ROWS // tm * MOE_CHUNK
    ys = None
    for t0 in range(0, n_tiles, step):
        t1 = min(t0 + step, n_tiles)
        xs, = _sc_gather(h2p, [src[t0 * tm:t1 * tm]])
        ys = _expert_call(tile_expert[t0:t1], jnp.clip(n_used - t0, 0, t1 - t0).reshape(1), xs, wg, wu, wd,
                          ys, t0, rows_max, tf=tf)
    n_pad = -(-n // GATHER_ROWS) * GATHER_ROWS
    pad_idx = lambda dd: jnp.concatenate([dd.astype(jnp.int32), jnp.arange(n_pad - n, dtype=jnp.int32)])
    ya, yb = _sc_gather(ys, [pad_idx(d_lo), pad_idx(d_hi)])
    return ya, yb, w


def _combine_kernel(ya_ref, yb_ref, w_ref, x1_ref, gt2_ref, gqf_ref, o_ref):
    w = w_ref[...]
    f = w[:, 0:1] * _load_split(ya_ref) + w[:, 1:2] * _load_split(yb_ref)
    o_ref[0] = x1_ref[0] + gt2_ref[0] * _rms(f, gqf_ref[...])


def _combine_call(ya, yb, w, x1, row0, mod, mod_rows, mod_idx, gqf, *, tm, name):
    b, t, d = x1.shape
    ntb = t // tm
    blk0 = row0 // tm
    assert row0 % tm == 0
    pool = lambda i: (blk0 + i, 0)
    pool3 = lambda i: (0, blk0 + i, 0)
    row = lambda i: (i // ntb, i % ntb, 0)
    return pl.pallas_call(
        _combine_kernel,
        grid=(b * ntb,),
        in_specs=[pl.BlockSpec((2, tm, d // 4), pool3), pl.BlockSpec((2, tm, d // 4), pool3),
                  pl.BlockSpec((tm, 2), pool), pl.BlockSpec((1, tm, d), row),
                  _mod_spec(mod_rows, 5, mod_idx), pl.BlockSpec((1, d), lambda i: (0, 0))],
        out_specs=pl.BlockSpec((1, tm, d), row),
        out_shape=jax.ShapeDtypeStruct((b, t, d), F32),
        compiler_params=_params(1), name=name,
    )(ya, yb, w, x1, mod, gqf)


def _rope_tables(pos):
    half = ROT_DIM // 2
    inv_freq = ROPE_THETA ** (-jnp.arange(half, dtype=F32) * 2.0 / ROT_DIM)
    ang = pos.astype(F32)[:, None] * inv_freq[None, :]
    cos, sin = jnp.cos(ang), jnp.sin(ang)
    l64 = np.arange(128) % HEAD_DIM
    idx = l64 % half
    first = (l64 < half)[None, :]
    second = ((l64 >= half) & (l64 < ROT_DIM))[None, :]
    cos_t = jnp.where(first | second, cos[:, idx], 1.0)
    s1_t = jnp.where(first, -sin[:, idx], 0.0)
    s2_t = jnp.where(second, sin[:, idx], 0.0)
    return cos_t, s1_t, s2_t


def kernel(x_prompt, x_sample, cache_k, cache_v, state_conv, c_prompt, c_sample, w_mod, b_mod, g_pre_mix, g_post_mix, g_pre_ffn, g_post_ffn, w_in, conv_w, conv_b, conv_ln_g, conv_ln_b, w_out, ffn_w_gate, ffn_w_up, ffn_w_down, moe_w_router, moe_w_gate, moe_w_up, moe_w_down):
    bp, seq, d = x_prompt.shape
    nb, n_new, _ = x_sample.shape
    depth = w_mod.shape[0]
    w_buf = cache_k.shape[2]
    past_len = PAST_LEN
    assert w_buf == min(DILATION_PATTERNS[-1][0], past_len)
    keep = min(DILATION_PATTERNS[-1][0], seq)
    ns = nb * n_new

    c_all = jnp.concatenate([c_sample, c_prompt, jnp.zeros((MOD_ROWS - nb - bp, d), F32)], axis=0)
    mod_all = _mod_call(c_all, w_mod, b_mod)

    tm_p = 512
    tabs_p = tuple(tb.reshape(seq // tm_p, tm_p, 128) for tb in _rope_tables(jnp.arange(seq, dtype=jnp.int32)))
    tabs_s = tuple(tb.reshape(n_new, 1, 128)
                   for tb in _rope_tables(past_len + jnp.arange(n_new, dtype=jnp.int32)))

    cache_kt = cache_k.transpose(0, 1, 3, 4, 2)
    cache_vt = cache_v.transpose(0, 1, 3, 4, 2)

    yp = x_prompt
    ys = x_sample.transpose(1, 0, 2).reshape(1, ns, d)
    outs = [[] for _ in range(6)]
    for l in range(depth):
        mod_p = mod_all[l].reshape(MOD_ROWS, 1, 6 * d)
        mod_s = jnp.tile(mod_all[l, :nb], (n_new, 1))[None]
        idx_p2 = lambda bi, i: (nb + bi, 0)
        idx_s2 = lambda bi, i: (0, i)
        w_in_bf = w_in[l].astype(BF16)
        w_out_bf = w_out[l].astype(BF16)
        conv_p = (jnp.pad(conv_w[l], ((0, 1), (0, 0))), conv_b[l][None], conv_ln_g[l][None], conv_ln_b[l][None])
        gpre, gpm, gpf, gqf = g_pre_mix[l][None], g_post_mix[l][None], g_pre_ffn[l][None], g_post_ffn[l][None]

        q, k, v, kt, vt, cv, ul = _inproj_call(yp, mod_p, 1, idx_p2, gpre, w_in_bf, tabs_p, tm_p, conv_p,
                                               tm=tm_p, keep=keep, name=f"inproj_prompt_{l}")
        att = _attn_call(q, k, v)
        outs[0].append(kt.reshape(bp, keep, N_HEADS, HEAD_DIM))
        outs[1].append(vt.reshape(bp, keep, N_HEADS, HEAD_DIM))
        outs[2].append(ul[:, CONV_HALO - (CONV_WIDTH - 1):])

        qs, _, _, kts, vts, us = _inproj_call(ys, mod_s, nb, idx_s2, gpre, w_in_bf, tabs_s, 1, None,
                                              tm=nb, keep=ns, name=f"inproj_sample_{l}")
        to_batch_major = lambda z: z.reshape(n_new, nb, N_HEADS, HEAD_DIM).transpose(1, 0, 2, 3)
        q_bm = (qs.reshape(HEAD_PAIRS, n_new, nb, 2, HEAD_DIM).transpose(2, 1, 0, 3, 4)
                .reshape(nb, n_new, N_HEADS, HEAD_DIM))
        k_bm, v_bm = to_batch_major(kts), to_batch_major(vts)
        head_major = lambda z: jnp.pad(z.transpose(0, 2, 1, 3), ((0, 0), (0, 0), (0, SAMPLE_ROWS - n_new), (0, 0)))
        att_s = _sample_attn_call(head_major(q_bm), head_major(k_bm), head_major(v_bm),
                                  cache_kt, cache_vt, l, n_new)
        att_s = (att_s[:, :, :n_new].reshape(nb, HEAD_PAIRS, 2, n_new, HEAD_DIM).transpose(1, 3, 0, 2, 4)
                 .reshape(1, HEAD_PAIRS, ns, 128).astype(BF16))
        ufull = jnp.concatenate([state_conv[l].transpose(1, 0, 2), us.reshape(n_new, nb, 512)], axis=0)
        cv_s = _sample_conv_call(ufull, conv_p).reshape(1, ns, 512)
        outs[3].append(k_bm)
        outs[4].append(v_bm)
        outs[5].append(ufull[n_new:].transpose(1, 0, 2))

        idx_p1 = lambda i, *_: (nb + i // (seq // tm_f), 0)
        idx_s1 = lambda i, *_: (0, i)
        if l % 2 == 0:
            tm_f = 1024
            wg, wu, wd = ffn_w_gate[l // 2], ffn_w_up[l // 2], ffn_w_down[l // 2]
            yp = _ffn_call(att, cv, yp, mod_p, 1, idx_p1, gpm, gpf, gqf, w_out_bf, wg, wu, wd,
                           tm=tm_f, tf=512, name=f"ffn_prompt_{l}")
            ys = _ffn_call(att_s, cv_s, ys, mod_s, ns, idx_s1, gpm, gpf, gqf, w_out_bf, wg, wu, wd,
                           tm=ns, tf=512, name=f"ffn_sample_{l}")
        else:
            tm_f = 1024
            wr = moe_w_router[l // 2].T
            wg, wu, wd = moe_w_gate[l // 2], moe_w_up[l // 2], moe_w_down[l // 2]
            x1, h2, gates, sel = _router_call(att, cv, yp, mod_p, 1, idx_p1, gpm, gpf, w_out_bf, wr,
                                              tm=tm_f, name=f"router_prompt_{l}")
            x1s, h2s, gates_s, sel_s = _router_call(att_s, cv_s, ys, mod_s, ns, idx_s1, gpm, gpf, w_out_bf, wr,
                                                    tm=ns, name=f"router_sample_{l}")
            pool = lambda a, b_: jnp.concatenate([a.reshape(bp * seq, -1), b_.reshape(ns, -1)], axis=0)
            h2_pool = jnp.concatenate([h2.reshape(2, bp * seq, d // 4), h2s.reshape(2, ns, d // 4)], axis=1)
            ya, yb, w12 = _moe_route(h2_pool, pool(sel, sel_s), pool(gates, gates_s), wg, wu, wd, tf=896)
            yp = _combine_call(ya, yb, w12, x1, 0, mod_p, 1, idx_p1, gqf, tm=tm_f, name=f"moe_combine_prompt_{l}")
            ys = _combine_call(ya, yb, w12, x1s, bp * seq, mod_s, ns, idx_s1, gqf, tm=ns,
                               name=f"moe_combine_sample_{l}")

    y_sample = ys.reshape(n_new, nb, d).transpose(1, 0, 2)
    return (yp, y_sample) + tuple(jnp.stack(o) for o in outs)
```

```python
import functools

import numpy as np
import jax
import jax.numpy as jnp
from jax import lax
from jax.experimental import pallas as pl
from jax.experimental.pallas import tpu as pltpu
from jax.experimental.pallas import tpu_sc as plsc

F32 = jnp.float32
BF16 = jnp.bfloat16

HEAD_DIM = 64
N_HEADS = 8
ATT_WIDTH = N_HEADS * HEAD_DIM
HEAD_PAIRS = ATT_WIDTH // 128
CONV_WIDTH = 31
CONV_HALO = 32
DILATION_PATTERNS = ((128, 1), (512, 4), (2048, 16))
BAND = 128
ROT_DIM = HEAD_DIM // 4
ROPE_THETA = 500000.0
ATTN_SCALE = HEAD_DIM ** -0.5
N_EXPERTS = 8
PAST_LEN = 2048
EPS = 1e-6
NEG = -1e30
MOD_ROWS = 136
VMEM_LIMIT = 56 * 1024 * 1024


def _rms(x, g):
    return x * lax.rsqrt(jnp.mean(x * x, axis=-1, keepdims=True) + EPS) * g


def _silu(x):
    return x * jax.nn.sigmoid(x)


def _params(n_axes, vmem=VMEM_LIMIT):
    return pltpu.CompilerParams(dimension_semantics=("arbitrary",) * n_axes, vmem_limit_bytes=vmem)


def _mod_kernel(c_ref, w_ref, b_ref, o_ref):
    a = _silu(c_ref[...]).astype(BF16)
    o_ref[0] = jnp.dot(a, w_ref[0].astype(BF16), preferred_element_type=F32) + b_ref[0]


def _mod_call(c_all, w_mod, b_mod):
    depth, d, d6 = w_mod.shape
    return pl.pallas_call(
        _mod_kernel,
        grid=(depth, d6 // d),
        in_specs=[pl.BlockSpec((MOD_ROWS, d), lambda l, j: (0, 0)),
                  pl.BlockSpec((1, d, d), lambda l, j: (l, 0, j)),
                  pl.BlockSpec((1, 1, d), lambda l, j: (l, 0, j))],
        out_specs=pl.BlockSpec((1, MOD_ROWS, d), lambda l, j: (l, 0, j)),
        out_shape=jax.ShapeDtypeStruct((depth, MOD_ROWS, d6), F32),
        compiler_params=_params(2),
        name="adaln_mod",
    )(c_all, w_mod, b_mod.reshape(depth, 1, d6))


def _inproj_kernel(*refs, conv, tm):
    if conv:
        (x_ref, sh_ref, sc_ref, g_ref, w_ref, cos_ref, s1_ref, s2_ref,
         cw_ref, cb_ref, lg_ref, lb_ref,
         q_ref, k_ref, v_ref, kt_ref, vt_ref, cv_ref, ul_ref, ubuf) = refs
    else:
        (x_ref, sh_ref, sc_ref, g_ref, w_ref, cos_ref, s1_ref, s2_ref,
         q_ref, k_ref, v_ref, kt_ref, vt_ref, u_ref) = refs

    h = _rms(x_ref[0], g_ref[...]) * (1.0 + sc_ref[0]) + sh_ref[0]
    proj = jnp.dot(h.astype(BF16), w_ref[...], preferred_element_type=F32)
    cos, s1, s2 = cos_ref[0], s1_ref[0], s2_ref[0]

    def rope(z):
        return z * cos + pltpu.roll(z, 128 - ROT_DIM // 2, 1) * s1 + pltpu.roll(z, ROT_DIM // 2, 1) * s2

    for hp in range(HEAD_PAIRS):
        lo, hi = hp * 128, (hp + 1) * 128
        q_ref[0, hp] = rope(proj[:, lo:hi]) * ATTN_SCALE
        kz = rope(proj[:, ATT_WIDTH + lo:ATT_WIDTH + hi])
        k_ref[0, hp] = kz
        kt_ref[0, :, lo:hi] = kz
        vz = proj[:, 2 * ATT_WIDTH + lo:2 * ATT_WIDTH + hi]
        v_ref[0, hp] = vz
        vt_ref[0, :, lo:hi] = vz

    a = proj[:, 3 * ATT_WIDTH:3 * ATT_WIDTH + 512]
    gate = proj[:, 3 * ATT_WIDTH + 512:]
    u = a * jax.nn.sigmoid(gate)
    if not conv:
        u_ref[0] = u
        return

    @pl.when(pl.program_id(1) == 0)
    def _():
        ubuf[0, 0:CONV_HALO, :] = jnp.zeros((CONV_HALO, 512), F32)

    ubuf[0, CONV_HALO:CONV_HALO + tm, :] = u
    span = tm + CONV_HALO - 8
    for s in range(1, 8):
        ubuf[s, 0:span, :] = ubuf[0, s:s + span, :]
    off0 = CONV_HALO - (CONV_WIDTH - 1)
    rows = 32

    def chunk(r, carry):
        base = pl.multiple_of(r * rows, rows)
        acc = jnp.broadcast_to(cb_ref[...], (rows, 512))
        for j in range(CONV_WIDTH):
            a, s = divmod(off0 + j, 8)
            acc = acc + cw_ref[j:j + 1, :] * ubuf[s, pl.ds(base + 8 * a, rows), :]
        mu = jnp.mean(acc, axis=-1, keepdims=True)
        xc = acc - mu
        var = jnp.mean(xc * xc, axis=-1, keepdims=True)
        y = xc * lax.rsqrt(var + EPS) * lg_ref[...] + lb_ref[...]
        cv_ref[0, pl.ds(base, rows), :] = _silu(y).astype(BF16)
        return carry

    lax.fori_loop(0, tm // rows, chunk, 0, unroll=4)
    tail = ubuf[0, tm:tm + CONV_HALO, :]
    ul_ref[0] = tail
    ubuf[0, 0:CONV_HALO, :] = tail


def _mod_spec(rows, chunk, index_fn):
    return pl.BlockSpec((1, rows, 1024), lambda *g: index_fn(*g) + (chunk,))


def _inproj_call(x, mod, mod_rows, mod_idx, g, w_bf, tabs, tab_rows, conv_p, *, tm, keep, name):
    b, t, d = x.shape
    nt = t // tm
    off = (t - keep) // tm
    conv = conv_p is not None
    tab_spec = pl.BlockSpec((1, tab_rows, 128), lambda bi, i: (i, 0, 0))
    in_specs = [pl.BlockSpec((1, tm, d), lambda bi, i: (bi, i, 0)),
                _mod_spec(mod_rows, 0, mod_idx), _mod_spec(mod_rows, 1, mod_idx),
                pl.BlockSpec((1, d), lambda bi, i: (0, 0)),
                pl.BlockSpec(w_bf.shape, lambda bi, i: (0, 0)),
                tab_spec, tab_spec, tab_spec]
    args = [x, mod, mod, g, w_bf, *tabs]
    hp_spec = pl.BlockSpec((1, HEAD_PAIRS, tm, 128), lambda bi, i: (bi, 0, i, 0))
    tail_spec = pl.BlockSpec((1, tm, 512), lambda bi, i: (bi, jnp.maximum(i - off, 0), 0))
    hp_shape = jax.ShapeDtypeStruct((b, HEAD_PAIRS, t, 128), F32)
    tail_shape = jax.ShapeDtypeStruct((b, keep, 512), F32)
    out_specs = [hp_spec, hp_spec, hp_spec, tail_spec, tail_spec]
    out_shape = [hp_shape, hp_shape, hp_shape, tail_shape, tail_shape]
    scratch = []
    if conv:
        small = pl.BlockSpec((1, 512), lambda bi, i: (0, 0))
        in_specs += [pl.BlockSpec((32, 512), lambda bi, i: (0, 0)), small, small, small]
        args += list(conv_p)
        out_specs += [pl.BlockSpec((1, tm, 512), lambda bi, i: (bi, i, 0)),
                      pl.BlockSpec((1, CONV_HALO, 512), lambda bi, i: (bi, 0, 0))]
        out_shape += [jax.ShapeDtypeStruct((b, t, 512), BF16),
                      jax.ShapeDtypeStruct((b, CONV_HALO, 512), F32)]
        scratch = [pltpu.VMEM((8, tm + CONV_HALO, 512), F32)]
    else:
        out_specs += [pl.BlockSpec((1, tm, 512), lambda bi, i: (bi, i, 0))]
        out_shape += [jax.ShapeDtypeStruct((b, t, 512), F32)]
    return pl.pallas_call(
        functools.partial(_inproj_kernel, conv=conv, tm=tm),
        grid=(b, nt), in_specs=in_specs, out_specs=out_specs, out_shape=out_shape,
        scratch_shapes=scratch, compiler_params=_params(2), name=name,
    )(*args)


def _attn_kernel(q_ref, kc_ref, kp_ref, vc_ref, vp_ref, o_ref, acc_s, m_s, l_s, *, tq):
    first_kj = jnp.where(pl.program_id(2) == 0, 128, 0)
    lane_lo = lax.broadcasted_iota(jnp.int32, (128, 128), 1) < HEAD_DIM
    qi = lax.broadcasted_iota(jnp.int32, (256, 256), 0)
    qi = jnp.where(qi >= 128, qi - 128, qi)
    kj = lax.broadcasted_iota(jnp.int32, (256, 256), 1)
    band = (kj >= qi) & (kj <= qi + BAND)
    band_first = band & (kj >= first_kj)

    ones_cols = jnp.ones((256, 128), BF16)
    nt = (((1,), (1,)), ((), ()))

    def strided(start, size, d):
        if d > 1:
            return pl.ds(start, size, stride=d)
        return pl.ds(start if isinstance(start, int) else pl.multiple_of(start, 128), size)

    def blocks(items):
        loaded = []
        for rows, first, c, d, s in items:
            if first:
                take = lambda rp, rc: jnp.concatenate(
                    [rp[0, 0, strided(tq - BAND * d + c, 128, d), :],
                     rc[0, 0, strided(c, 128, d), :]], axis=0)
            else:
                krows = strided(c + d * 128 * (s - 1), 256, d)
                take = lambda rp, rc: rc[0, 0, krows, :]
            loaded.append((q_ref[0, 0, rows, :], take(kp_ref, kc_ref), take(vp_ref, vc_ref),
                           band_first if first else band))
        scores = []
        for q2, k2, _, _ in loaded:
            qq = jnp.concatenate([jnp.where(lane_lo, q2, 0.0), jnp.where(lane_lo, 0.0, q2)], axis=0)
            scores.append(lax.dot_general(qq.astype(BF16), k2.astype(BF16), nt,
                                          preferred_element_type=F32))
        probs = []
        for sc, (_, _, _, mask) in zip(scores, loaded):
            sc = jnp.where(mask, sc, NEG)
            mb = jnp.max(sc, axis=1, keepdims=True)
            probs.append((mb, jnp.exp(sc - mb).astype(BF16)))
        out = []
        for (mb, p), (_, _, v2, _), item in zip(probs, loaded, items):
            pv = jnp.dot(p, jnp.concatenate([v2.astype(BF16), ones_cols], axis=1), preferred_element_type=F32)
            out.append((item[0], jnp.where(lane_lo, mb[:128], mb[128:]),
                        jnp.where(lane_lo, pv[:128, 128:], pv[128:, 128:]),
                        jnp.where(lane_lo, pv[:128, :128], pv[128:, :128])))
        return out

    def merge(rows, mt, lt, pvt):
        mp, lp, ap = m_s[rows, :], l_s[rows, :], acc_s[rows, :]
        mn = jnp.maximum(mp, mt)
        a = jnp.exp(mp - mn)
        b = jnp.exp(mt - mn)
        return mn, a * lp + b * lt, a * ap + b * pvt

    def store_init(results):
        for rows, mt, lt, pvt in results:
            m_s[rows, :] = mt
            l_s[rows, :] = lt
            acc_s[rows, :] = pvt

    def store_merged(results):
        for rows, mt, lt, pvt in results:
            mn, ln, an = merge(rows, mt, lt, pvt)
            m_s[rows, :] = mn
            l_s[rows, :] = ln
            acc_s[rows, :] = an

    def store_output(results):
        for rows, mt, lt, pvt in results:
            _, ln, an = merge(rows, mt, lt, pvt)
            o_ref[0, 0, rows, :] = (an / ln).astype(o_ref.dtype)

    def loop(n, fn):
        def body(i, carry):
            fn(i)
            return carry
        lax.fori_loop(0, n, body, 0)

    group = 4
    loop(16 // group, lambda g: store_init(blocks(
        [(pl.ds(g * group + j, 128, stride=16), True, g * group + j, 16, 0) for j in range(group)])))

    store_merged(blocks([(pl.ds(c, 128, stride=4), True, c, 4, 0) for c in range(4)]))
    loop(tq // 512 - 1, lambda n: store_merged(blocks(
        [(pl.ds(c + 512 * (n + 1), 128, stride=4), False, c, 4, n + 1) for c in range(4)])))

    store_output(blocks([(strided(0, 128, 1), True, 0, 1, 0)]
                        + [(strided(128 * s, 128, 1), False, 0, 1, s) for s in range(1, group)]))
    rest = 4
    assert (tq // 128 - group) % rest == 0
    loop((tq // 128 - group) // rest, lambda g: store_output(blocks(
        [(strided(128 * (group + g * rest + j), 128, 1), False, 0, 1, group + g * rest + j) for j in range(rest)])))


def _attn_call(q, k, v, *, tq=2048):
    b, hp, t, _ = q.shape
    assert tq == BAND * 16 and t % tq == 0
    cur = pl.BlockSpec((1, 1, tq, 128), lambda bi, h, i: (bi, h, i, 0))
    prev = pl.BlockSpec((1, 1, tq, 128), lambda bi, h, i: (bi, h, jnp.maximum(i - 1, 0), 0))
    return pl.pallas_call(
        functools.partial(_attn_kernel, tq=tq),
        grid=(b, hp, t // tq),
        in_specs=[cur, cur, prev, cur, prev],
        out_specs=cur,
        out_shape=jax.ShapeDtypeStruct((b, hp, t, 128), BF16),
        scratch_shapes=[pltpu.VMEM((tq, 128), F32)] * 3,
        compiler_params=_params(3), name="dilated_attn_prompt",
    )(q, k, k, v, v)


SAMPLE_ROWS = 8


def _sample_attn_kernel(q_ref, kn_ref, vn_ref, kt_ref, vt_ref, mh_ref, mn_ref, o_ref):
    mh, mn = mh_ref[...], mn_ref[...]
    nt = (((1,), (1,)), ((), ()))
    for h in range(N_HEADS):
        qh = q_ref[0, h].astype(BF16)
        s = jnp.dot(qh, kt_ref[0, 0, h].astype(BF16), preferred_element_type=F32)
        sn = lax.dot_general(qh, kn_ref[0, h].astype(BF16), nt, preferred_element_type=F32)
        s = jnp.where(mh > 0, s, NEG)
        sn = jnp.where(mn > 0, sn, NEG)
        m = jnp.maximum(jnp.max(s, axis=1, keepdims=True), jnp.max(sn, axis=1, keepdims=True))
        p = mh * jnp.exp(s - m)
        pn = mn * jnp.exp(sn - m)
        den = jnp.sum(p, axis=1, keepdims=True) + jnp.sum(pn, axis=1, keepdims=True)
        num = (lax.dot_general(p.astype(BF16), vt_ref[0, 0, h].astype(BF16), nt, preferred_element_type=F32)
               + jnp.dot(pn.astype(BF16), vn_ref[0, h].astype(BF16), preferred_element_type=F32))
        o_ref[0, h] = num / jnp.where(den > 0, den, 1.0)


def _sample_tables(w_buf, n_new):
    rows = np.arange(w_buf + n_new)
    mult = np.zeros((SAMPLE_ROWS, w_buf + SAMPLE_ROWS), np.float32)
    for t in range(n_new):
        dist = w_buf + t - rows
        for window, dil in DILATION_PATTERNS:
            mult[t, :w_buf + n_new] += (dist >= 0) & (dist <= window) & (dist % dil == 0)
    return jnp.asarray(mult[:, :w_buf]), jnp.asarray(mult[:, w_buf:])


def _sample_attn_call(q, kn, vn, cache_kt, cache_vt, layer, n_new):
    nb = q.shape[0]
    w_buf = cache_kt.shape[-1]
    mh, mn = _sample_tables(w_buf, n_new)
    new_spec = pl.BlockSpec((1, N_HEADS, SAMPLE_ROWS, HEAD_DIM), lambda i: (i, 0, 0, 0))
    cache_spec = pl.BlockSpec((1, 1, N_HEADS, HEAD_DIM, w_buf), lambda i: (layer, i, 0, 0, 0))
    return pl.pallas_call(
        _sample_attn_kernel,
        grid=(nb,),
        in_specs=[new_spec, new_spec, new_spec, cache_spec, cache_spec,
                  pl.BlockSpec(mh.shape, lambda i: (0, 0)), pl.BlockSpec(mn.shape, lambda i: (0, 0))],
        out_specs=new_spec,
        out_shape=jax.ShapeDtypeStruct((nb, N_HEADS, SAMPLE_ROWS, HEAD_DIM), F32),
        compiler_params=_params(1), name="dilated_attn_sample",
    )(q, kn, vn, cache_kt, cache_vt, mh, mn)


def _sample_conv_kernel(uf_ref, cw_ref, cb_ref, lg_ref, lb_ref, o_ref, *, n_new, nb):
    rows = 32

    def body(n, carry):
        t = n // (nb // rows)
        base = pl.multiple_of((n % (nb // rows)) * rows, rows)
        acc = jnp.broadcast_to(cb_ref[...], (rows, 512))
        for j in range(CONV_WIDTH):
            acc = acc + cw_ref[j:j + 1, :] * uf_ref[t + j, pl.ds(base, rows), :]
        mu = jnp.mean(acc, axis=-1, keepdims=True)
        xc = acc - mu
        var = jnp.mean(xc * xc, axis=-1, keepdims=True)
        y = xc * lax.rsqrt(var + EPS) * lg_ref[...] + lb_ref[...]
        o_ref[t, pl.ds(base, rows), :] = _silu(y).astype(BF16)
        return carry

    lax.fori_loop(0, n_new * (nb // rows), body, 0)


def _sample_conv_call(ufull, conv_p):
    n_full, nb, ch = ufull.shape
    n_new = n_full - (CONV_WIDTH - 1)
    return pl.pallas_call(
        functools.partial(_sample_conv_kernel, n_new=n_new, nb=nb),
        out_shape=jax.ShapeDtypeStruct((n_new, nb, ch), BF16),
        compiler_params=pltpu.CompilerParams(vmem_limit_bytes=VMEM_LIMIT),
        name="conv_sample",
    )(ufull, *conv_p)


def _mix_residual(att_ref, cv_ref, x_ref, gt1_ref, sc2_ref, sh2_ref, gpm_ref, gpf_ref, wo_ref):
    a = jnp.concatenate([att_ref[0, hp] for hp in range(HEAD_PAIRS)] + [cv_ref[0]], axis=-1)
    mix = jnp.dot(a, wo_ref[...], preferred_element_type=F32)
    x1 = x_ref[0] + gt1_ref[0] * _rms(mix, gpm_ref[...])
    h2 = _rms(x1, gpf_ref[...]) * (1.0 + sc2_ref[0]) + sh2_ref[0]
    return x1, h2


def _ffn_kernel(att_ref, cv_ref, x_ref, gt1_ref, sc2_ref, sh2_ref, gt2_ref, gpm_ref, gpf_ref, gqf_ref,
                wo_ref, wg_ref, wu_ref, wd_ref, o_ref, x1_s, h2_s, acc_s):
    f = pl.program_id(1)

    @pl.when(f == 0)
    def _():
        x1, h2 = _mix_residual(att_ref, cv_ref, x_ref, gt1_ref, sc2_ref, sh2_ref, gpm_ref, gpf_ref, wo_ref)
        x1_s[...] = x1
        h2_s[...] = h2.astype(BF16)
        acc_s[...] = jnp.zeros_like(acc_s)

    h2 = h2_s[...]
    g = jnp.dot(h2, wg_ref[...].astype(BF16), preferred_element_type=F32)
    u = jnp.dot(h2, wu_ref[...].astype(BF16), preferred_element_type=F32)
    act = (_silu(g) * u).astype(BF16)
    acc_s[...] += jnp.dot(act, wd_ref[...].astype(BF16), preferred_element_type=F32)

    @pl.when(f == pl.num_programs(1) - 1)
    def _():
        o_ref[0] = x1_s[...] + gt2_ref[0] * _rms(acc_s[...], gqf_ref[...])


def _mix_in_specs(b, t, tm, mod_rows, mod_idx, d):
    ntb = t // tm
    row = lambda i, *_: (i // ntb, i % ntb, 0)
    specs = [pl.BlockSpec((1, HEAD_PAIRS, tm, 128), lambda i, *_: (i // ntb, 0, i % ntb, 0)),
             pl.BlockSpec((1, tm, 512), row),
             pl.BlockSpec((1, tm, d), row)]
    return specs, row


def _ffn_call(att, cv, x, mod, mod_rows, mod_idx, gpm, gpf, gqf, wo_bf, wg, wu, wd, *, tm, tf, name):
    b, t, d = x.shape
    ff = wg.shape[1]
    ntb = t // tm
    specs, row = _mix_in_specs(b, t, tm, mod_rows, mod_idx, d)
    vec = pl.BlockSpec((1, d), lambda i, f: (0, 0))
    in_specs = specs + [_mod_spec(mod_rows, 2, mod_idx), _mod_spec(mod_rows, 4, mod_idx),
                        _mod_spec(mod_rows, 3, mod_idx), _mod_spec(mod_rows, 5, mod_idx),
                        vec, vec, vec,
                        pl.BlockSpec((d, d), lambda i, f: (0, 0)),
                        pl.BlockSpec((d, tf), lambda i, f: (0, f)),
                        pl.BlockSpec((d, tf), lambda i, f: (0, f)),
                        pl.BlockSpec((tf, d), lambda i, f: (f, 0))]
    return pl.pallas_call(
        _ffn_kernel,
        grid=(b * ntb, ff // tf),
        in_specs=in_specs,
        out_specs=pl.BlockSpec((1, tm, d), row),
        out_shape=jax.ShapeDtypeStruct((b, t, d), F32),
        scratch_shapes=[pltpu.VMEM((tm, d), F32), pltpu.VMEM((tm, d), BF16), pltpu.VMEM((tm, d), F32)],
        compiler_params=_params(2), name=name,
    )(att, cv, x, mod, mod, mod, mod, gpm, gpf, gqf, wo_bf, wg, wu, wd)


def _router_kernel(att_ref, cv_ref, x_ref, gt1_ref, sc2_ref, sh2_ref, gpm_ref, gpf_ref, wo_ref, wr_ref,
                   x1_ref, h2_ref, gate_ref, sel_ref):
    x1, h2 = _mix_residual(att_ref, cv_ref, x_ref, gt1_ref, sc2_ref, sh2_ref, gpm_ref, gpf_ref, wo_ref)
    x1_ref[0] = x1
    _store_split(h2_ref.at[:, 0], h2)
    logits = jnp.concatenate([jnp.sum(h2 * wr_ref[e:e + 1, :], axis=1, keepdims=True)
                              for e in range(N_EXPERTS)], axis=1)
    lane = lax.broadcasted_iota(jnp.int32, logits.shape, 1).astype(F32)
    v1 = jnp.max(logits, axis=-1, keepdims=True)
    i1 = jnp.min(jnp.where(logits == v1, lane, float(N_EXPERTS)), axis=-1, keepdims=True)
    oh1 = lane == i1
    rest = jnp.where(oh1, -jnp.inf, logits)
    v2 = jnp.max(rest, axis=-1, keepdims=True)
    i2 = jnp.min(jnp.where(rest == v2, lane, float(N_EXPERTS)), axis=-1, keepdims=True)
    oh2 = lane == i2
    e2 = jnp.exp(v2 - v1)
    den = 1.0 + e2
    gate_ref[0] = jnp.where(oh1, 1.0 / den, 0.0) + jnp.where(oh2, e2 / den, 0.0)
    sel_ref[0] = jnp.where(oh1 | oh2, 1.0, 0.0)


def _router_call(att, cv, x, mod, mod_rows, mod_idx, gpm, gpf, wo_bf, wr, *, tm, name):
    b, t, d = x.shape
    specs, row = _mix_in_specs(b, t, tm, mod_rows, mod_idx, d)
    vec = pl.BlockSpec((1, d), lambda i: (0, 0))
    in_specs = specs + [_mod_spec(mod_rows, 2, mod_idx), _mod_spec(mod_rows, 4, mod_idx),
                        _mod_spec(mod_rows, 3, mod_idx), vec, vec,
                        pl.BlockSpec((d, d), lambda i: (0, 0)),
                        pl.BlockSpec((N_EXPERTS, d), lambda i: (0, 0))]
    return pl.pallas_call(
        _router_kernel,
        grid=(b * (t // tm),),
        in_specs=in_specs,
        out_specs=[pl.BlockSpec((1, tm, d), row),
                   pl.BlockSpec((2, 1, tm, d // 4), lambda i: (0,) + row(i)),
                   pl.BlockSpec((1, tm, N_EXPERTS), row), pl.BlockSpec((1, tm, N_EXPERTS), row)],
        out_shape=[jax.ShapeDtypeStruct((b, t, d), F32), jax.ShapeDtypeStruct((2, b, t, d // 4), jnp.int32),
                   jax.ShapeDtypeStruct((b, t, N_EXPERTS), F32), jax.ShapeDtypeStruct((b, t, N_EXPERTS), F32)],
        compiler_params=_params(1), name=name,
    )(att, cv, x, mod, mod, mod, gpm, gpf, wo_bf, wr)


MOE_TILE = 1024
RANK_BLOCK = 1536
GATHER_WINDOW = 128
GATHER_ROWS = 32 * GATHER_WINDOW
MOE_CHUNK = 3
GATHER_DEPTH = 3


def _pack_pairs(x):
    w = x.shape[1] // 2
    lo = pltpu.bitcast(x[:, :w].astype(BF16).astype(F32), jnp.int32)
    hi = pltpu.bitcast(x[:, w:].astype(BF16).astype(F32), jnp.int32)
    return hi | lax.shift_right_logical(lo, 16)


def _unpack_pairs(words):
    lo = pltpu.bitcast(lax.shift_left(words, 16), F32)
    hi = pltpu.bitcast(words & -65536, F32)
    return jnp.concatenate([lo, hi], axis=1)


def _store_split(ref, x):
    words = _pack_pairs(x)
    q = words.shape[1] // 2
    ref[0] = words[:, :q]
    ref[1] = words[:, q:]


def _load_split(ref):
    return _unpack_pairs(jnp.concatenate([ref[0], ref[1]], axis=1))


def _rank_kernel(sel_ref, rank_ref, cnt_ref, carry):
    @pl.when(pl.program_id(0) == 0)
    def _():
        carry[...] = jnp.zeros_like(carry)

    sel = sel_ref[...]
    rb = sel.shape[0]
    before = (lax.broadcasted_iota(jnp.int32, (rb, rb), 1)
              < lax.broadcasted_iota(jnp.int32, (rb, rb), 0)).astype(BF16)
    rank_ref[...] = jnp.dot(before, sel.astype(BF16), preferred_element_type=F32) + carry[...]
    carry[...] += jnp.sum(sel, axis=0, keepdims=True)
    cnt_ref[...] = carry[...]


def _rank_call(sel):
    n, n_e = sel.shape
    rb = RANK_BLOCK
    assert n % rb == 0
    return pl.pallas_call(
        _rank_kernel,
        grid=(n // rb,),
        in_specs=[pl.BlockSpec((rb, n_e), lambda i: (i, 0))],
        out_specs=[pl.BlockSpec((rb, n_e), lambda i: (i, 0)), pl.BlockSpec((1, n_e), lambda i: (0, 0))],
        out_shape=[jax.ShapeDtypeStruct((n, n_e), F32), jax.ShapeDtypeStruct((1, n_e), F32)],
        scratch_shapes=[pltpu.VMEM((1, n_e), F32)],
        compiler_params=_params(1), name="moe_rank",
    )(sel)


def _sc_gather(table, idxs):
    halves, _, width = table.shape
    n_lists, n = len(idxs), idxs[0].shape[0]
    info = plsc.get_sparse_core_info()
    win, depth = GATHER_WINDOW, GATHER_DEPTH
    assert n % GATHER_ROWS == 0 and GATHER_ROWS == info.num_cores * info.num_subcores * win
    wins = n // GATHER_ROWS
    jobs = [(l, j, h) for l in range(n_lists) for j in range(wins) for h in range(halves)]
    mesh = plsc.VectorSubcoreMesh(core_axis_name="core", subcore_axis_name="subcore")
    out_type = [jax.ShapeDtypeStruct((halves, n, width), table.dtype) for _ in idxs]
    scratch = [pltpu.VMEM((n_lists, wins, win), jnp.int32), pltpu.VMEM((depth, win, width), table.dtype),
               pltpu.SemaphoreType.DMA((depth,)), pltpu.SemaphoreType.DMA((depth,))]

    @functools.partial(pl.kernel, out_type=out_type, mesh=mesh, scratch_types=scratch)
    def gather(table_hbm, idx_hbm, *refs):
        out_refs = refs[:n_lists]
        idx_v, buf, sem_in, sem_out = refs[n_lists:]
        worker = lax.axis_index("subcore") * info.num_cores + lax.axis_index("core")
        first = worker * wins
        for l in range(n_lists):
            pltpu.sync_copy(idx_hbm.at[l, worker], idx_v.at[l])
        for g in range(0, len(jobs), depth):
            group = jobs[g:g + depth]
            reads = [pltpu.async_copy(table_hbm.at[h].at[idx_v.at[l, j]], buf.at[k], sem_in.at[k])
                     for k, (l, j, h) in enumerate(group)]
            writes = []
            for k, (l, j, h) in enumerate(group):
                reads[k].wait()
                rows = pl.ds(pl.multiple_of((first + j) * win, win), win)
                writes.append(pltpu.async_copy(buf.at[k], out_refs[l].at[h, rows], sem_out.at[k]))
            for write in writes:
                write.wait()

    return gather(table, jnp.stack(idxs).reshape(n_lists, n // (wins * win), wins, win))


def _expert_kernel(te_ref, nu_ref, xs_ref, wg_ref, wu_ref, wd_ref, *refs):
    ys_ref, xb, acc = refs[-3:]
    t, f = pl.program_id(0), pl.program_id(1)
    last = pl.num_programs(1) - 1
    used = t < nu_ref[0]

    @pl.when(used & (f == 0))
    def _():
        xb[...] = _load_split(xs_ref).astype(BF16)
        acc[...] = jnp.zeros_like(acc)

    @pl.when(used)
    def _():
        x = xb[...]
        g = jnp.dot(x, wg_ref[0].astype(BF16), preferred_element_type=F32)
        u = jnp.dot(x, wu_ref[0].astype(BF16), preferred_element_type=F32)
        act = (_silu(g) * u).astype(BF16)
        acc[...] += jnp.dot(act, wd_ref[0].astype(BF16), preferred_element_type=F32)

    @pl.when(used & (f == last))
    def _():
        _store_split(ys_ref, acc[...])

    @pl.when(jnp.logical_not(used) & (f == last))
    def _():
        ys_ref[...] = jnp.zeros_like(ys_ref)


def _expert_call(tile_expert, n_used, xs, wg, wu, wd, ys, tile0, rows_total, *, tf):
    _, rows, quarter = xs.shape
    n_e, d, ff = wg.shape
    tm = MOE_TILE
    nf = ff // tf
    chunk = lambda t, f, te, nu: jnp.where(t < nu[0], f, nf - 1)
    in_specs = [pl.BlockSpec((2, tm, quarter), lambda t, f, te, nu: (0, t, 0)),
                pl.BlockSpec((1, d, tf), lambda t, f, te, nu: (te[t], 0, chunk(t, f, te, nu))),
                pl.BlockSpec((1, d, tf), lambda t, f, te, nu: (te[t], 0, chunk(t, f, te, nu))),
                pl.BlockSpec((1, tf, d), lambda t, f, te, nu: (te[t], chunk(t, f, te, nu), 0))]
    args = [tile_expert, n_used, xs, wg, wu, wd]
    aliases = {}
    if ys is not None:
        in_specs.append(pl.BlockSpec(memory_space=pl.ANY))
        args.append(ys)
        aliases = {len(args) - 1: 0}
    grid_spec = pltpu.PrefetchScalarGridSpec(
        num_scalar_prefetch=2,
        grid=(rows // tm, nf),
        in_specs=in_specs,
        out_specs=pl.BlockSpec((2, tm, quarter), lambda t, f, te, nu: (0, tile0 + t, 0)),
        scratch_shapes=[pltpu.VMEM((tm, d), BF16), pltpu.VMEM((tm, d), F32)])
    return pl.pallas_call(
        _expert_kernel, grid_spec=grid_spec,
        out_shape=jax.ShapeDtypeStruct((2, rows_total, quarter), jnp.int32),
        input_output_aliases=aliases,
        compiler_params=_params(2), name="moe_experts",
    )(*args)


def _moe_route(h2p, sel, gates, wg, wu, wd, *, tf):
    n, n_e = sel.shape
    tm = MOE_TILE
    rank, cnt = _rank_call(sel)
    counts = cnt[0].astype(jnp.int32)
    padded = (counts + tm - 1) // tm * tm
    seg_end = jnp.cumsum(padded)
    seg_start = seg_end - padded
    rows_max = -(-(2 * n + n_e * tm) // GATHER_ROWS) * GATHER_ROWS
    assert rows_max % tm == 0
    lanes = jnp.arange(n_e, dtype=jnp.int32)[None, :]
    e_lo = jnp.min(jnp.where(sel > 0, lanes, n_e - 1), axis=1)
    e_hi = jnp.max(jnp.where(sel > 0, lanes, 0), axis=1)
    pick = lambda a, e: jnp.take_along_axis(a, e[:, None], axis=1)[:, 0]
    rank_i = rank.astype(jnp.int32)
    d_lo = seg_start[e_lo] + pick(rank_i, e_lo)
    d_hi = seg_start[e_hi] + pick(rank_i, e_hi)
    w = jnp.stack([pick(gates, e_lo), pick(gates, e_hi)], axis=1)
    tok = jnp.arange(n, dtype=jnp.int32)
    src = (jnp.arange(rows_max, dtype=jnp.int32) % n).at[jnp.concatenate([d_lo, d_hi])].set(
        jnp.concatenate([tok, tok]), unique_indices=True)
    n_tiles = rows_max // tm
    n_used = (seg_end[-1] // tm).astype(jnp.int32)
    tile_ids = jnp.minimum(jnp.arange(n_tiles, dtype=jnp.int32), n_used - 1)
    tile_expert = jnp.minimum(jnp.sum(seg_end[None, :] <= (tile_ids * tm)[:, None], axis=1), n_e - 1).astype(jnp.int32)

    step = GATHER_ROWS // tm * MOE_CHUNK
    ys = None
    for t0 in range(0, n_tiles, step):
        t1 = min(t0 + step, n_tiles)
        xs, = _sc_gather(h2p, [src[t0 * tm:t1 * tm]])
        ys = _expert_call(tile_expert[t0:t1], jnp.clip(n_used - t0, 0, t1 - t0).reshape(1), xs, wg, wu, wd,
                          ys, t0, rows_max, tf=tf)
    n_pad = -(-n // GATHER_ROWS) * GATHER_ROWS
    pad_idx = lambda dd: jnp.concatenate([dd.astype(jnp.int32), jnp.arange(n_pad - n, dtype=jnp.int32)])
    ya, yb = _sc_gather(ys, [pad_idx(d_lo), pad_idx(d_hi)])
    return ya, yb, w


def _combine_kernel(ya_ref, yb_ref, w_ref, x1_ref, gt2_ref, gqf_ref, o_ref):
    w = w_ref[...]
    f = w[:, 0:1] * _load_split(ya_ref) + w[:, 1:2] * _load_split(yb_ref)
    o_ref[0] = x1_ref[0] + gt2_ref[0] * _rms(f, gqf_ref[...])


def _combine_call(ya, yb, w, x1, row0, mod, mod_rows, mod_idx, gqf, *, tm, name):
    b, t, d = x1.shape
    ntb = t // tm
    blk0 = row0 // tm
    assert row0 % tm == 0
    pool = lambda i: (blk0 + i, 0)
    pool3 = lambda i: (0, blk0 + i, 0)
    row = lambda i: (i // ntb, i % ntb, 0)
    return pl.pallas_call(
        _combine_kernel,
        grid=(b * ntb,),
        in_specs=[pl.BlockSpec((2, tm, d // 4), pool3), pl.BlockSpec((2, tm, d // 4), pool3),
                  pl.BlockSpec((tm, 2), pool), pl.BlockSpec((1, tm, d), row),
                  _mod_spec(mod_rows, 5, mod_idx), pl.BlockSpec((1, d), lambda i: (0, 0))],
        out_specs=pl.BlockSpec((1, tm, d), row),
        out_shape=jax.ShapeDtypeStruct((b, t, d), F32),
        compiler_params=_params(1), name=name,
    )(ya, yb, w, x1, mod, gqf)


def _rope_tables(pos):
    half = ROT_DIM // 2
    inv_freq = ROPE_THETA ** (-jnp.arange(half, dtype=F32) * 2.0 / ROT_DIM)
    ang = pos.astype(F32)[:, None] * inv_freq[None, :]
    cos, sin = jnp.cos(ang), jnp.sin(ang)
    l64 = np.arange(128) % HEAD_DIM
    idx = l64 % half
    first = (l64 < half)[None, :]
    second = ((l64 >= half) & (l64 < ROT_DIM))[None, :]
    cos_t = jnp.where(first | second, cos[:, idx], 1.0)
    s1_t = jnp.where(first, -sin[:, idx], 0.0)
    s2_t = jnp.where(second, sin[:, idx], 0.0)
    return cos_t, s1_t, s2_t


def kernel(x_prompt, x_sample, cache_k, cache_v, state_conv, c_prompt, c_sample, w_mod, b_mod, g_pre_mix, g_post_mix, g_pre_ffn, g_post_ffn, w_in, conv_w, conv_b, conv_ln_g, conv_ln_b, w_out, ffn_w_gate, ffn_w_up, ffn_w_down, moe_w_router, moe_w_gate, moe_w_up, moe_w_down):
    bp, seq, d = x_prompt.shape
    nb, n_new, _ = x_sample.shape
    depth = w_mod.shape[0]
    w_buf = cache_k.shape[2]
    past_len = PAST_LEN
    assert w_buf == min(DILATION_PATTERNS[-1][0], past_len)
    keep = min(DILATION_PATTERNS[-1][0], seq)
    ns = nb * n_new

    c_all = jnp.concatenate([c_sample, c_prompt, jnp.zeros((MOD_ROWS - nb - bp, d), F32)], axis=0)
    mod_all = _mod_call(c_all, w_mod, b_mod)

    tm_p = 512
    tabs_p = tuple(tb.reshape(seq // tm_p, tm_p, 128) for tb in _rope_tables(jnp.arange(seq, dtype=jnp.int32)))
    tabs_s = tuple(tb.reshape(n_new, 1, 128)
                   for tb in _rope_tables(past_len + jnp.arange(n_new, dtype=jnp.int32)))

    cache_kt = cache_k.transpose(0, 1, 3, 4, 2)
    cache_vt = cache_v.transpose(0, 1, 3, 4, 2)

    yp = x_prompt
    ys = x_sample.transpose(1, 0, 2).reshape(1, ns, d)
    outs = [[] for _ in range(6)]
    for l in range(depth):
        mod_p = mod_all[l].reshape(MOD_ROWS, 1, 6 * d)
        mod_s = jnp.tile(mod_all[l, :nb], (n_new, 1))[None]
        idx_p2 = lambda bi, i: (nb + bi, 0)
        idx_s2 = lambda bi, i: (0, i)
        w_in_bf = w_in[l].astype(BF16)
        w_out_bf = w_out[l].astype(BF16)
        conv_p = (jnp.pad(conv_w[l], ((0, 1), (0, 0))), conv_b[l][None], conv_ln_g[l][None], conv_ln_b[l][None])
        gpre, gpm, gpf, gqf = g_pre_mix[l][None], g_post_mix[l][None], g_pre_ffn[l][None], g_post_ffn[l][None]

        q, k, v, kt, vt, cv, ul = _inproj_call(yp, mod_p, 1, idx_p2, gpre, w_in_bf, tabs_p, tm_p, conv_p,
                                               tm=tm_p, keep=keep, name=f"inproj_prompt_{l}")
        att = _attn_call(q, k, v)
        outs[0].append(kt.reshape(bp, keep, N_HEADS, HEAD_DIM))
        outs[1].append(vt.reshape(bp, keep, N_HEADS, HEAD_DIM))
        outs[2].append(ul[:, CONV_HALO - (CONV_WIDTH - 1):])

        qs, _, _, kts, vts, us = _inproj_call(ys, mod_s, nb, idx_s2, gpre, w_in_bf, tabs_s, 1, None,
                                              tm=nb, keep=ns, name=f"inproj_sample_{l}")
        to_batch_major = lambda z: z.reshape(n_new, nb, N_HEADS, HEAD_DIM).transpose(1, 0, 2, 3)
        q_bm = (qs.reshape(HEAD_PAIRS, n_new, nb, 2, HEAD_DIM).transpose(2, 1, 0, 3, 4)
                .reshape(nb, n_new, N_HEADS, HEAD_DIM))
        k_bm, v_bm = to_batch_major(kts), to_batch_major(vts)
        head_major = lambda z: jnp.pad(z.transpose(0, 2, 1, 3), ((0, 0), (0, 0), (0, SAMPLE_ROWS - n_new), (0, 0)))
        att_s = _sample_attn_call(head_major(q_bm), head_major(k_bm), head_major(v_bm),
                                  cache_kt, cache_vt, l, n_new)
        att_s = (att_s[:, :, :n_new].reshape(nb, HEAD_PAIRS, 2, n_new, HEAD_DIM).transpose(1, 3, 0, 2, 4)
                 .reshape(1, HEAD_PAIRS, ns, 128).astype(BF16))
        ufull = jnp.concatenate([state_conv[l].transpose(1, 0, 2), us.reshape(n_new, nb, 512)], axis=0)
        cv_s = _sample_conv_call(ufull, conv_p).reshape(1, ns, 512)
        outs[3].append(k_bm)
        outs[4].append(v_bm)
        outs[5].append(ufull[n_new:].transpose(1, 0, 2))

        idx_p1 = lambda i, *_: (nb + i // (seq // tm_f), 0)
        idx_s1 = lambda i, *_: (0, i)
        if l % 2 == 0:
            tm_f = 1024
            wg, wu, wd = ffn_w_gate[l // 2], ffn_w_up[l // 2], ffn_w_down[l // 2]
            yp = _ffn_call(att, cv, yp, mod_p, 1, idx_p1, gpm, gpf, gqf, w_out_bf, wg, wu, wd,
                           tm=tm_f, tf=512, name=f"ffn_prompt_{l}")
            ys = _ffn_call(att_s, cv_s, ys, mod_s, ns, idx_s1, gpm, gpf, gqf, w_out_bf, wg, wu, wd,
                           tm=ns, tf=512, name=f"ffn_sample_{l}")
        else:
            tm_f = 1024
            wr = moe_w_router[l // 2].T
            wg, wu, wd = moe_w_gate[l // 2], moe_w_up[l // 2], moe_w_down[l // 2]
            x1, h2, gates, sel = _router_call(att, cv, yp, mod_p, 1, idx_p1, gpm, gpf, w_out_bf, wr,
                                              tm=tm_f, name=f"router_prompt_{l}")
            x1s, h2s, gates_s, sel_s = _router_call(att_s, cv_s, ys, mod_s, ns, idx_s1, gpm, gpf, w_out_bf, wr,
                                                    tm=ns, name=f"router_sample_{l}")
            pool = lambda a, b_: jnp.concatenate([a.reshape(bp * seq, -1), b_.reshape(ns, -1)], axis=0)
            h2_pool = jnp.concatenate([h2.reshape(2, bp * seq, d // 4), h2s.reshape(2, ns, d // 4)], axis=1)
            ya, yb, w12 = _moe_route(h2_pool, pool(sel, sel_s), pool(gates, gates_s), wg, wu, wd, tf=512)
            yp = _combine_call(ya, yb, w12, x1, 0, mod_p, 1, idx_p1, gqf, tm=tm_f, name=f"moe_combine_prompt_{l}")
            ys = _combine_call(ya, yb, w12, x1s, bp * seq, mod_s, ns, idx_s1, gqf, tm=ns,
                               name=f"moe_combine_sample_{l}")

    y_sample = ys.reshape(n_new, nb, d).transpose(1, 0, 2)
    return (yp, y_sample) + tuple(jnp.stack(o) for o in outs)
```

```python
import functools

import numpy as np
import jax
import jax.numpy as jnp
from jax import lax
from jax.experimental import pallas as pl
from jax.experimental.pallas import tpu as pltpu
from jax.experimental.pallas import tpu_sc as plsc

F32 = jnp.float32
BF16 = jnp.bfloat16

HEAD_DIM = 64
N_HEADS = 8
ATT_WIDTH = N_HEADS * HEAD_DIM
HEAD_PAIRS = ATT_WIDTH // 128
CONV_WIDTH = 31
CONV_HALO = 32
DILATION_PATTERNS = ((128, 1), (512, 4), (2048, 16))
BAND = 128
ROT_DIM = HEAD_DIM // 4
ROPE_THETA = 500000.0
ATTN_SCALE = HEAD_DIM ** -0.5
N_EXPERTS = 8
PAST_LEN = 2048
EPS = 1e-6
NEG = -1e30
MOD_ROWS = 136
VMEM_LIMIT = 56 * 1024 * 1024


def _rms(x, g):
    return x * lax.rsqrt(jnp.mean(x * x, axis=-1, keepdims=True) + EPS) * g


def _silu(x):
    return x * jax.nn.sigmoid(x)


def _params(n_axes, vmem=VMEM_LIMIT):
    return pltpu.CompilerParams(dimension_semantics=("arbitrary",) * n_axes, vmem_limit_bytes=vmem)


def _mod_kernel(c_ref, w_ref, b_ref, o_ref):
    a = _silu(c_ref[...]).astype(BF16)
    o_ref[0] = jnp.dot(a, w_ref[0].astype(BF16), preferred_element_type=F32) + b_ref[0]


def _mod_call(c_all, w_mod, b_mod):
    depth, d, d6 = w_mod.shape
    return pl.pallas_call(
        _mod_kernel,
        grid=(depth, d6 // d),
        in_specs=[pl.BlockSpec((MOD_ROWS, d), lambda l, j: (0, 0)),
                  pl.BlockSpec((1, d, d), lambda l, j: (l, 0, j)),
                  pl.BlockSpec((1, 1, d), lambda l, j: (l, 0, j))],
        out_specs=pl.BlockSpec((1, MOD_ROWS, d), lambda l, j: (l, 0, j)),
        out_shape=jax.ShapeDtypeStruct((depth, MOD_ROWS, d6), F32),
        compiler_params=_params(2),
        name="adaln_mod",
    )(c_all, w_mod, b_mod.reshape(depth, 1, d6))


def _inproj_kernel(*refs, conv, tm):
    if conv:
        (x_ref, sh_ref, sc_ref, g_ref, w_ref, cos_ref, s1_ref, s2_ref,
         cw_ref, cb_ref, lg_ref, lb_ref,
         q_ref, k_ref, v_ref, kt_ref, vt_ref, cv_ref, ul_ref, ubuf) = refs
    else:
        (x_ref, sh_ref, sc_ref, g_ref, w_ref, cos_ref, s1_ref, s2_ref,
         q_ref, k_ref, v_ref, kt_ref, vt_ref, u_ref) = refs

    h = _rms(x_ref[0], g_ref[...]) * (1.0 + sc_ref[0]) + sh_ref[0]
    proj = jnp.dot(h.astype(BF16), w_ref[...], preferred_element_type=F32)
    cos, s1, s2 = cos_ref[0], s1_ref[0], s2_ref[0]

    def rope(z):
        return z * cos + pltpu.roll(z, 128 - ROT_DIM // 2, 1) * s1 + pltpu.roll(z, ROT_DIM // 2, 1) * s2

    for hp in range(HEAD_PAIRS):
        lo, hi = hp * 128, (hp + 1) * 128
        q_ref[0, hp] = rope(proj[:, lo:hi]) * ATTN_SCALE
        kz = rope(proj[:, ATT_WIDTH + lo:ATT_WIDTH + hi])
        k_ref[0, hp] = kz
        kt_ref[0, :, lo:hi] = kz
        vz = proj[:, 2 * ATT_WIDTH + lo:2 * ATT_WIDTH + hi]
        v_ref[0, hp] = vz
        vt_ref[0, :, lo:hi] = vz

    a = proj[:, 3 * ATT_WIDTH:3 * ATT_WIDTH + 512]
    gate = proj[:, 3 * ATT_WIDTH + 512:]
    u = a * jax.nn.sigmoid(gate)
    if not conv:
        u_ref[0] = u
        return

    @pl.when(pl.program_id(1) == 0)
    def _():
        ubuf[0, 0:CONV_HALO, :] = jnp.zeros((CONV_HALO, 512), F32)

    ubuf[0, CONV_HALO:CONV_HALO + tm, :] = u
    span = tm + CONV_HALO - 8
    for s in range(1, 8):
        ubuf[s, 0:span, :] = ubuf[0, s:s + span, :]
    off0 = CONV_HALO - (CONV_WIDTH - 1)
    rows = 32

    def chunk(r, carry):
        base = pl.multiple_of(r * rows, rows)
        acc = jnp.broadcast_to(cb_ref[...], (rows, 512))
        for j in range(CONV_WIDTH):
            a, s = divmod(off0 + j, 8)
            acc = acc + cw_ref[j:j + 1, :] * ubuf[s, pl.ds(base + 8 * a, rows), :]
        mu = jnp.mean(acc, axis=-1, keepdims=True)
        xc = acc - mu
        var = jnp.mean(xc * xc, axis=-1, keepdims=True)
        y = xc * lax.rsqrt(var + EPS) * lg_ref[...] + lb_ref[...]
        cv_ref[0, pl.ds(base, rows), :] = _silu(y).astype(BF16)
        return carry

    lax.fori_loop(0, tm // rows, chunk, 0, unroll=4)
    tail = ubuf[0, tm:tm + CONV_HALO, :]
    ul_ref[0] = tail
    ubuf[0, 0:CONV_HALO, :] = tail


def _mod_spec(rows, chunk, index_fn):
    return pl.BlockSpec((1, rows, 1024), lambda *g: index_fn(*g) + (chunk,))


def _inproj_call(x, mod, mod_rows, mod_idx, g, w_bf, tabs, tab_rows, conv_p, *, tm, keep, name):
    b, t, d = x.shape
    nt = t // tm
    off = (t - keep) // tm
    conv = conv_p is not None
    tab_spec = pl.BlockSpec((1, tab_rows, 128), lambda bi, i: (i, 0, 0))
    in_specs = [pl.BlockSpec((1, tm, d), lambda bi, i: (bi, i, 0)),
                _mod_spec(mod_rows, 0, mod_idx), _mod_spec(mod_rows, 1, mod_idx),
                pl.BlockSpec((1, d), lambda bi, i: (0, 0)),
                pl.BlockSpec(w_bf.shape, lambda bi, i: (0, 0)),
                tab_spec, tab_spec, tab_spec]
    args = [x, mod, mod, g, w_bf, *tabs]
    hp_spec = pl.BlockSpec((1, HEAD_PAIRS, tm, 128), lambda bi, i: (bi, 0, i, 0))
    tail_spec = pl.BlockSpec((1, tm, 512), lambda bi, i: (bi, jnp.maximum(i - off, 0), 0))
    hp_shape = jax.ShapeDtypeStruct((b, HEAD_PAIRS, t, 128), F32)
    tail_shape = jax.ShapeDtypeStruct((b, keep, 512), F32)
    out_specs = [hp_spec, hp_spec, hp_spec, tail_spec, tail_spec]
    out_shape = [hp_shape, hp_shape, hp_shape, tail_shape, tail_shape]
    scratch = []
    if conv:
        small = pl.BlockSpec((1, 512), lambda bi, i: (0, 0))
        in_specs += [pl.BlockSpec((32, 512), lambda bi, i: (0, 0)), small, small, small]
        args += list(conv_p)
        out_specs += [pl.BlockSpec((1, tm, 512), lambda bi, i: (bi, i, 0)),
                      pl.BlockSpec((1, CONV_HALO, 512), lambda bi, i: (bi, 0, 0))]
        out_shape += [jax.ShapeDtypeStruct((b, t, 512), BF16),
                      jax.ShapeDtypeStruct((b, CONV_HALO, 512), F32)]
        scratch = [pltpu.VMEM((8, tm + CONV_HALO, 512), F32)]
    else:
        out_specs += [pl.BlockSpec((1, tm, 512), lambda bi, i: (bi, i, 0))]
        out_shape += [jax.ShapeDtypeStruct((b, t, 512), F32)]
    return pl.pallas_call(
        functools.partial(_inproj_kernel, conv=conv, tm=tm),
        grid=(b, nt), in_specs=in_specs, out_specs=out_specs, out_shape=out_shape,
        scratch_shapes=scratch, compiler_params=_params(2), name=name,
    )(*args)


def _attn_kernel(q_ref, kc_ref, kp_ref, vc_ref, vp_ref, o_ref, acc_s, m_s, l_s, *, tq):
    first_kj = jnp.where(pl.program_id(2) == 0, 128, 0)
    lane_lo = lax.broadcasted_iota(jnp.int32, (128, 128), 1) < HEAD_DIM
    qi = lax.broadcasted_iota(jnp.int32, (256, 256), 0)
    qi = jnp.where(qi >= 128, qi - 128, qi)
    kj = lax.broadcasted_iota(jnp.int32, (256, 256), 1)
    band = (kj >= qi) & (kj <= qi + BAND)
    band_first = band & (kj >= first_kj)

    ones_cols = jnp.ones((256, 128), BF16)
    nt = (((1,), (1,)), ((), ()))

    def strided(start, size, d):
        if d > 1:
            return pl.ds(start, size, stride=d)
        return pl.ds(start if isinstance(start, int) else pl.multiple_of(start, 128), size)

    def blocks(items):
        loaded = []
        for rows, first, c, d, s in items:
            if first:
                take = lambda rp, rc: jnp.concatenate(
                    [rp[0, 0, strided(tq - BAND * d + c, 128, d), :],
                     rc[0, 0, strided(c, 128, d), :]], axis=0)
            else:
                krows = strided(c + d * 128 * (s - 1), 256, d)
                take = lambda rp, rc: rc[0, 0, krows, :]
            loaded.append((q_ref[0, 0, rows, :], take(kp_ref, kc_ref), take(vp_ref, vc_ref),
                           band_first if first else band))
        scores = []
        for q2, k2, _, _ in loaded:
            qq = jnp.concatenate([jnp.where(lane_lo, q2, 0.0), jnp.where(lane_lo, 0.0, q2)], axis=0)
            scores.append(lax.dot_general(qq.astype(BF16), k2.astype(BF16), nt,
                                          preferred_element_type=F32))
        probs = []
        for sc, (_, _, _, mask) in zip(scores, loaded):
            sc = jnp.where(mask, sc, NEG)
            mb = jnp.max(sc, axis=1, keepdims=True)
            probs.append((mb, jnp.exp(sc - mb).astype(BF16)))
        out = []
        for (mb, p), (_, _, v2, _), item in zip(probs, loaded, items):
            pv = jnp.dot(p, jnp.concatenate([v2.astype(BF16), ones_cols], axis=1), preferred_element_type=F32)
            out.append((item[0], jnp.where(lane_lo, mb[:128], mb[128:]),
                        jnp.where(lane_lo, pv[:128, 128:], pv[128:, 128:]),
                        jnp.where(lane_lo, pv[:128, :128], pv[128:, :128])))
        return out

    def merge(rows, mt, lt, pvt):
        mp, lp, ap = m_s[rows, :], l_s[rows, :], acc_s[rows, :]
        mn = jnp.maximum(mp, mt)
        a = jnp.exp(mp - mn)
        b = jnp.exp(mt - mn)
        return mn, a * lp + b * lt, a * ap + b * pvt

    def store_init(results):
        for rows, mt, lt, pvt in results:
            m_s[rows, :] = mt
            l_s[rows, :] = lt
            acc_s[rows, :] = pvt

    def store_merged(results):
        for rows, mt, lt, pvt in results:
            mn, ln, an = merge(rows, mt, lt, pvt)
            m_s[rows, :] = mn
            l_s[rows, :] = ln
            acc_s[rows, :] = an

    def store_output(results):
        for rows, mt, lt, pvt in results:
            _, ln, an = merge(rows, mt, lt, pvt)
            o_ref[0, 0, rows, :] = (an / ln).astype(o_ref.dtype)

    def loop(n, fn):
        def body(i, carry):
            fn(i)
            return carry
        lax.fori_loop(0, n, body, 0)

    group = 4
    loop(16 // group, lambda g: store_init(blocks(
        [(pl.ds(g * group + j, 128, stride=16), True, g * group + j, 16, 0) for j in range(group)])))

    store_merged(blocks([(pl.ds(c, 128, stride=4), True, c, 4, 0) for c in range(4)]))
    loop(tq // 512 - 1, lambda n: store_merged(blocks(
        [(pl.ds(c + 512 * (n + 1), 128, stride=4), False, c, 4, n + 1) for c in range(4)])))

    store_output(blocks([(strided(0, 128, 1), True, 0, 1, 0)]
                        + [(strided(128 * s, 128, 1), False, 0, 1, s) for s in range(1, group)]))
    rest = 4
    assert (tq // 128 - group) % rest == 0
    loop((tq // 128 - group) // rest, lambda g: store_output(blocks(
        [(strided(128 * (group + g * rest + j), 128, 1), False, 0, 1, group + g * rest + j) for j in range(rest)])))


def _attn_call(q, k, v, *, tq=2048):
    b, hp, t, _ = q.shape
    assert tq == BAND * 16 and t % tq == 0
    cur = pl.BlockSpec((1, 1, tq, 128), lambda bi, h, i: (bi, h, i, 0))
    prev = pl.BlockSpec((1, 1, tq, 128), lambda bi, h, i: (bi, h, jnp.maximum(i - 1, 0), 0))
    return pl.pallas_call(
        functools.partial(_attn_kernel, tq=tq),
        grid=(b, hp, t // tq),
        in_specs=[cur, cur, prev, cur, prev],
        out_specs=cur,
        out_shape=jax.ShapeDtypeStruct((b, hp, t, 128), BF16),
        scratch_shapes=[pltpu.VMEM((tq, 128), F32)] * 3,
        compiler_params=_params(3), name="dilated_attn_prompt",
    )(q, k, k, v, v)


SAMPLE_ROWS = 8
SAMPLE_STEP = 2


def _sample_attn_kernel(q_ref, kn_ref, vn_ref, kt_ref, vt_ref, mh_ref, mn_ref, o_ref):
    mh, mn = mh_ref[...], mn_ref[...]
    nt = (((1,), (1,)), ((), ()))
    for b, h in [(b, h) for b in range(q_ref.shape[0]) for h in range(N_HEADS)]:
        qh = q_ref[b, h].astype(BF16)
        s = jnp.dot(qh, kt_ref[0, b, h].astype(BF16), preferred_element_type=F32)
        sn = lax.dot_general(qh, kn_ref[b, h].astype(BF16), nt, preferred_element_type=F32)
        s = jnp.where(mh > 0, s, NEG)
        sn = jnp.where(mn > 0, sn, NEG)
        m = jnp.maximum(jnp.max(s, axis=1, keepdims=True), jnp.max(sn, axis=1, keepdims=True))
        p = mh * jnp.exp(s - m)
        pn = mn * jnp.exp(sn - m)
        den = jnp.sum(p, axis=1, keepdims=True) + jnp.sum(pn, axis=1, keepdims=True)
        num = (lax.dot_general(p.astype(BF16), vt_ref[0, b, h].astype(BF16), nt, preferred_element_type=F32)
               + jnp.dot(pn.astype(BF16), vn_ref[b, h].astype(BF16), preferred_element_type=F32))
        o_ref[b, h] = num / jnp.where(den > 0, den, 1.0)


def _sample_tables(w_buf, n_new):
    rows = np.arange(w_buf + n_new)
    mult = np.zeros((SAMPLE_ROWS, w_buf + SAMPLE_ROWS), np.float32)
    for t in range(n_new):
        dist = w_buf + t - rows
        for window, dil in DILATION_PATTERNS:
            mult[t, :w_buf + n_new] += (dist >= 0) & (dist <= window) & (dist % dil == 0)
    return jnp.asarray(mult[:, :w_buf]), jnp.asarray(mult[:, w_buf:])


def _sample_attn_call(q, kn, vn, cache_kt, cache_vt, layer, n_new):
    nb = q.shape[0]
    w_buf = cache_kt.shape[-1]
    mh, mn = _sample_tables(w_buf, n_new)
    step = SAMPLE_STEP
    new_spec = pl.BlockSpec((step, N_HEADS, SAMPLE_ROWS, HEAD_DIM), lambda i: (i, 0, 0, 0))
    cache_spec = pl.BlockSpec((1, step, N_HEADS, HEAD_DIM, w_buf), lambda i: (layer, i, 0, 0, 0))
    return pl.pallas_call(
        _sample_attn_kernel,
        grid=(nb // step,),
        in_specs=[new_spec, new_spec, new_spec, cache_spec, cache_spec,
                  pl.BlockSpec(mh.shape, lambda i: (0, 0)), pl.BlockSpec(mn.shape, lambda i: (0, 0))],
        out_specs=new_spec,
        out_shape=jax.ShapeDtypeStruct((nb, N_HEADS, SAMPLE_ROWS, HEAD_DIM), F32),
        compiler_params=_params(1), name="dilated_attn_sample",
    )(q, kn, vn, cache_kt, cache_vt, mh, mn)


def _sample_conv_kernel(uf_ref, cw_ref, cb_ref, lg_ref, lb_ref, o_ref, *, n_new, nb):
    rows = 32

    def body(n, carry):
        t = n // (nb // rows)
        base = pl.multiple_of((n % (nb // rows)) * rows, rows)
        acc = jnp.broadcast_to(cb_ref[...], (rows, 512))
        for j in range(CONV_WIDTH):
            acc = acc + cw_ref[j:j + 1, :] * uf_ref[t + j, pl.ds(base, rows), :]
        mu = jnp.mean(acc, axis=-1, keepdims=True)
        xc = acc - mu
        var = jnp.mean(xc * xc, axis=-1, keepdims=True)
        y = xc * lax.rsqrt(var + EPS) * lg_ref[...] + lb_ref[...]
        o_ref[t, pl.ds(base, rows), :] = _silu(y).astype(BF16)
        return carry

    lax.fori_loop(0, n_new * (nb // rows), body, 0)


def _sample_conv_call(ufull, conv_p):
    n_full, nb, ch = ufull.shape
    n_new = n_full - (CONV_WIDTH - 1)
    return pl.pallas_call(
        functools.partial(_sample_conv_kernel, n_new=n_new, nb=nb),
        out_shape=jax.ShapeDtypeStruct((n_new, nb, ch), BF16),
        compiler_params=pltpu.CompilerParams(vmem_limit_bytes=VMEM_LIMIT),
        name="conv_sample",
    )(ufull, *conv_p)


def _mix_residual(att_ref, cv_ref, x_ref, gt1_ref, sc2_ref, sh2_ref, gpm_ref, gpf_ref, wo_ref):
    a = jnp.concatenate([att_ref[0, hp] for hp in range(HEAD_PAIRS)] + [cv_ref[0]], axis=-1)
    mix = jnp.dot(a, wo_ref[...], preferred_element_type=F32)
    x1 = x_ref[0] + gt1_ref[0] * _rms(mix, gpm_ref[...])
    h2 = _rms(x1, gpf_ref[...]) * (1.0 + sc2_ref[0]) + sh2_ref[0]
    return x1, h2


def _ffn_kernel(att_ref, cv_ref, x_ref, gt1_ref, sc2_ref, sh2_ref, gt2_ref, gpm_ref, gpf_ref, gqf_ref,
                wo_ref, wg_ref, wu_ref, wd_ref, o_ref, x1_s, h2_s, acc_s):
    f = pl.program_id(1)

    @pl.when(f == 0)
    def _():
        x1, h2 = _mix_residual(att_ref, cv_ref, x_ref, gt1_ref, sc2_ref, sh2_ref, gpm_ref, gpf_ref, wo_ref)
        x1_s[...] = x1
        h2_s[...] = h2.astype(BF16)
        acc_s[...] = jnp.zeros_like(acc_s)

    h2 = h2_s[...]
    g = jnp.dot(h2, wg_ref[...].astype(BF16), preferred_element_type=F32)
    u = jnp.dot(h2, wu_ref[...].astype(BF16), preferred_element_type=F32)
    act = (_silu(g) * u).astype(BF16)
    acc_s[...] += jnp.dot(act, wd_ref[...].astype(BF16), preferred_element_type=F32)

    @pl.when(f == pl.num_programs(1) - 1)
    def _():
        o_ref[0] = x1_s[...] + gt2_ref[0] * _rms(acc_s[...], gqf_ref[...])


def _mix_in_specs(b, t, tm, mod_rows, mod_idx, d):
    ntb = t // tm
    row = lambda i, *_: (i // ntb, i % ntb, 0)
    specs = [pl.BlockSpec((1, HEAD_PAIRS, tm, 128), lambda i, *_: (i // ntb, 0, i % ntb, 0)),
             pl.BlockSpec((1, tm, 512), row),
             pl.BlockSpec((1, tm, d), row)]
    return specs, row


def _ffn_call(att, cv, x, mod, mod_rows, mod_idx, gpm, gpf, gqf, wo_bf, wg, wu, wd, *, tm, tf, name):
    b, t, d = x.shape
    ff = wg.shape[1]
    ntb = t // tm
    specs, row = _mix_in_specs(b, t, tm, mod_rows, mod_idx, d)
    vec = pl.BlockSpec((1, d), lambda i, f: (0, 0))
    in_specs = specs + [_mod_spec(mod_rows, 2, mod_idx), _mod_spec(mod_rows, 4, mod_idx),
                        _mod_spec(mod_rows, 3, mod_idx), _mod_spec(mod_rows, 5, mod_idx),
                        vec, vec, vec,
                        pl.BlockSpec((d, d), lambda i, f: (0, 0)),
                        pl.BlockSpec((d, tf), lambda i, f: (0, f)),
                        pl.BlockSpec((d, tf), lambda i, f: (0, f)),
                        pl.BlockSpec((tf, d), lambda i, f: (f, 0))]
    return pl.pallas_call(
        _ffn_kernel,
        grid=(b * ntb, ff // tf),
        in_specs=in_specs,
        out_specs=pl.BlockSpec((1, tm, d), row),
        out_shape=jax.ShapeDtypeStruct((b, t, d), F32),
        scratch_shapes=[pltpu.VMEM((tm, d), F32), pltpu.VMEM((tm, d), BF16), pltpu.VMEM((tm, d), F32)],
        compiler_params=_params(2), name=name,
    )(att, cv, x, mod, mod, mod, mod, gpm, gpf, gqf, wo_bf, wg, wu, wd)


def _router_kernel(att_ref, cv_ref, x_ref, gt1_ref, sc2_ref, sh2_ref, gpm_ref, gpf_ref, wo_ref, wr_ref,
                   x1_ref, h2_ref, gate_ref, sel_ref):
    x1, h2 = _mix_residual(att_ref, cv_ref, x_ref, gt1_ref, sc2_ref, sh2_ref, gpm_ref, gpf_ref, wo_ref)
    x1_ref[0] = x1
    _store_split(h2_ref.at[:, 0], h2)
    logits = jnp.concatenate([jnp.sum(h2 * wr_ref[e:e + 1, :], axis=1, keepdims=True)
                              for e in range(N_EXPERTS)], axis=1)
    lane = lax.broadcasted_iota(jnp.int32, logits.shape, 1).astype(F32)
    v1 = jnp.max(logits, axis=-1, keepdims=True)
    i1 = jnp.min(jnp.where(logits == v1, lane, float(N_EXPERTS)), axis=-1, keepdims=True)
    oh1 = lane == i1
    rest = jnp.where(oh1, -jnp.inf, logits)
    v2 = jnp.max(rest, axis=-1, keepdims=True)
    i2 = jnp.min(jnp.where(rest == v2, lane, float(N_EXPERTS)), axis=-1, keepdims=True)
    oh2 = lane == i2
    e2 = jnp.exp(v2 - v1)
    den = 1.0 + e2
    gate_ref[0] = jnp.where(oh1, 1.0 / den, 0.0) + jnp.where(oh2, e2 / den, 0.0)
    sel_ref[0] = jnp.where(oh1 | oh2, 1.0, 0.0)


def _router_call(att, cv, x, mod, mod_rows, mod_idx, gpm, gpf, wo_bf, wr, *, tm, name):
    b, t, d = x.shape
    specs, row = _mix_in_specs(b, t, tm, mod_rows, mod_idx, d)
    vec = pl.BlockSpec((1, d), lambda i: (0, 0))
    in_specs = specs + [_mod_spec(mod_rows, 2, mod_idx), _mod_spec(mod_rows, 4, mod_idx),
                        _mod_spec(mod_rows, 3, mod_idx), vec, vec,
                        pl.BlockSpec((d, d), lambda i: (0, 0)),
                        pl.BlockSpec((N_EXPERTS, d), lambda i: (0, 0))]
    return pl.pallas_call(
        _router_kernel,
        grid=(b * (t // tm),),
        in_specs=in_specs,
        out_specs=[pl.BlockSpec((1, tm, d), row),
                   pl.BlockSpec((2, 1, tm, d // 4), lambda i: (0,) + row(i)),
                   pl.BlockSpec((1, tm, N_EXPERTS), row), pl.BlockSpec((1, tm, N_EXPERTS), row)],
        out_shape=[jax.ShapeDtypeStruct((b, t, d), F32), jax.ShapeDtypeStruct((2, b, t, d // 4), jnp.int32),
                   jax.ShapeDtypeStruct((b, t, N_EXPERTS), F32), jax.ShapeDtypeStruct((b, t, N_EXPERTS), F32)],
        compiler_params=_params(1), name=name,
    )(att, cv, x, mod, mod, mod, gpm, gpf, wo_bf, wr)


MOE_TILE = 1024
RANK_BLOCK = 1536
GATHER_WINDOW = 128
GATHER_ROWS = 32 * GATHER_WINDOW
MOE_CHUNK = 3
GATHER_DEPTH = 3


def _pack_pairs(x):
    w = x.shape[1] // 2
    lo = pltpu.bitcast(x[:, :w].astype(BF16).astype(F32), jnp.int32)
    hi = pltpu.bitcast(x[:, w:].astype(BF16).astype(F32), jnp.int32)
    return hi | lax.shift_right_logical(lo, 16)


def _unpack_pairs(words):
    lo = pltpu.bitcast(lax.shift_left(words, 16), F32)
    hi = pltpu.bitcast(words & -65536, F32)
    return jnp.concatenate([lo, hi], axis=1)


def _store_split(ref, x):
    words = _pack_pairs(x)
    q = words.shape[1] // 2
    ref[0] = words[:, :q]
    ref[1] = words[:, q:]


def _load_split(ref):
    return _unpack_pairs(jnp.concatenate([ref[0], ref[1]], axis=1))


def _rank_kernel(sel_ref, rank_ref, cnt_ref, carry):
    @pl.when(pl.program_id(0) == 0)
    def _():
        carry[...] = jnp.zeros_like(carry)

    sel = sel_ref[...]
    rb = sel.shape[0]
    before = (lax.broadcasted_iota(jnp.int32, (rb, rb), 1)
              < lax.broadcasted_iota(jnp.int32, (rb, rb), 0)).astype(BF16)
    rank_ref[...] = jnp.dot(before, sel.astype(BF16), preferred_element_type=F32) + carry[...]
    carry[...] += jnp.sum(sel, axis=0, keepdims=True)
    cnt_ref[...] = carry[...]


def _rank_call(sel):
    n, n_e = sel.shape
    rb = RANK_BLOCK
    assert n % rb == 0
    return pl.pallas_call(
        _rank_kernel,
        grid=(n // rb,),
        in_specs=[pl.BlockSpec((rb, n_e), lambda i: (i, 0))],
        out_specs=[pl.BlockSpec((rb, n_e), lambda i: (i, 0)), pl.BlockSpec((1, n_e), lambda i: (0, 0))],
        out_shape=[jax.ShapeDtypeStruct((n, n_e), F32), jax.ShapeDtypeStruct((1, n_e), F32)],
        scratch_shapes=[pltpu.VMEM((1, n_e), F32)],
        compiler_params=_params(1), name="moe_rank",
    )(sel)


def _sc_gather(table, idxs):
    halves, _, width = table.shape
    n_lists, n = len(idxs), idxs[0].shape[0]
    info = plsc.get_sparse_core_info()
    win, depth = GATHER_WINDOW, GATHER_DEPTH
    assert n % GATHER_ROWS == 0 and GATHER_ROWS == info.num_cores * info.num_subcores * win
    wins = n // GATHER_ROWS
    jobs = [(l, j, h) for l in range(n_lists) for j in range(wins) for h in range(halves)]
    mesh = plsc.VectorSubcoreMesh(core_axis_name="core", subcore_axis_name="subcore")
    out_type = [jax.ShapeDtypeStruct((halves, n, width), table.dtype) for _ in idxs]
    scratch = [pltpu.VMEM((n_lists, wins, win), jnp.int32), pltpu.VMEM((depth, win, width), table.dtype),
               pltpu.SemaphoreType.DMA((depth,)), pltpu.SemaphoreType.DMA((depth,))]

    @functools.partial(pl.kernel, out_type=out_type, mesh=mesh, scratch_types=scratch)
    def gather(table_hbm, idx_hbm, *refs):
        out_refs = refs[:n_lists]
        idx_v, buf, sem_in, sem_out = refs[n_lists:]
        worker = lax.axis_index("subcore") * info.num_cores + lax.axis_index("core")
        first = worker * wins
        for l in range(n_lists):
            pltpu.sync_copy(idx_hbm.at[l, worker], idx_v.at[l])
        for g in range(0, len(jobs), depth):
            group = jobs[g:g + depth]
            reads = [pltpu.async_copy(table_hbm.at[h].at[idx_v.at[l, j]], buf.at[k], sem_in.at[k])
                     for k, (l, j, h) in enumerate(group)]
            writes = []
            for k, (l, j, h) in enumerate(group):
                reads[k].wait()
                rows = pl.ds(pl.multiple_of((first + j) * win, win), win)
                writes.append(pltpu.async_copy(buf.at[k], out_refs[l].at[h, rows], sem_out.at[k]))
            for write in writes:
                write.wait()

    return gather(table, jnp.stack(idxs).reshape(n_lists, n // (wins * win), wins, win))


def _expert_kernel(te_ref, nu_ref, xs_ref, wg_ref, wu_ref, wd_ref, *refs):
    ys_ref, xb, acc = refs[-3:]
    t, f = pl.program_id(0), pl.program_id(1)
    last = pl.num_programs(1) - 1
    used = t < nu_ref[0]

    @pl.when(used & (f == 0))
    def _():
        xb[...] = _load_split(xs_ref).astype(BF16)
        acc[...] = jnp.zeros_like(acc)

    @pl.when(used)
    def _():
        x = xb[...]
        g = jnp.dot(x, wg_ref[0].astype(BF16), preferred_element_type=F32)
        u = jnp.dot(x, wu_ref[0].astype(BF16), preferred_element_type=F32)
        act = (_silu(g) * u).astype(BF16)
        acc[...] += jnp.dot(act, wd_ref[0].astype(BF16), preferred_element_type=F32)

    @pl.when(used & (f == last))
    def _():
        _store_split(ys_ref, acc[...])

    @pl.when(jnp.logical_not(used) & (f == last))
    def _():
        ys_ref[...] = jnp.zeros_like(ys_ref)


def _expert_call(tile_expert, n_used, xs, wg, wu, wd, ys, tile0, rows_total, *, tf):
    _, rows, quarter = xs.shape
    n_e, d, ff = wg.shape
    tm = MOE_TILE
    nf = ff // tf
    chunk = lambda t, f, te, nu: jnp.where(t < nu[0], f, nf - 1)
    in_specs = [pl.BlockSpec((2, tm, quarter), lambda t, f, te, nu: (0, t, 0)),
                pl.BlockSpec((1, d, tf), lambda t, f, te, nu: (te[t], 0, chunk(t, f, te, nu))),
                pl.BlockSpec((1, d, tf), lambda t, f, te, nu: (te[t], 0, chunk(t, f, te, nu))),
                pl.BlockSpec((1, tf, d), lambda t, f, te, nu: (te[t], chunk(t, f, te, nu), 0))]
    args = [tile_expert, n_used, xs, wg, wu, wd]
    aliases = {}
    if ys is not None:
        in_specs.append(pl.BlockSpec(memory_space=pl.ANY))
        args.append(ys)
        aliases = {len(args) - 1: 0}
    grid_spec = pltpu.PrefetchScalarGridSpec(
        num_scalar_prefetch=2,
        grid=(rows // tm, nf),
        in_specs=in_specs,
        out_specs=pl.BlockSpec((2, tm, quarter), lambda t, f, te, nu: (0, tile0 + t, 0)),
        scratch_shapes=[pltpu.VMEM((tm, d), BF16), pltpu.VMEM((tm, d), F32)])
    return pl.pallas_call(
        _expert_kernel, grid_spec=grid_spec,
        out_shape=jax.ShapeDtypeStruct((2, rows_total, quarter), jnp.int32),
        input_output_aliases=aliases,
        compiler_params=_params(2), name="moe_experts",
    )(*args)


def _moe_route(h2p, sel, gates, wg, wu, wd, *, tf):
    n, n_e = sel.shape
    tm = MOE_TILE
    rank, cnt = _rank_call(sel)
    counts = cnt[0].astype(jnp.int32)
    padded = (counts + tm - 1) // tm * tm
    seg_end = jnp.cumsum(padded)
    seg_start = seg_end - padded
    rows_max = -(-(2 * n + n_e * tm) // GATHER_ROWS) * GATHER_ROWS
    assert rows_max % tm == 0
    lanes = jnp.arange(n_e, dtype=jnp.int32)[None, :]
    e_lo = jnp.min(jnp.where(sel > 0, lanes, n_e - 1), axis=1)
    e_hi = jnp.max(jnp.where(sel > 0, lanes, 0), axis=1)
    pick = lambda a, e: jnp.take_along_axis(a, e[:, None], axis=1)[:, 0]
    rank_i = rank.astype(jnp.int32)
    d_lo = seg_start[e_lo] + pick(rank_i, e_lo)
    d_hi = seg_start[e_hi] + pick(rank_i, e_hi)
    w = jnp.stack([pick(gates, e_lo), pick(gates, e_hi)], axis=1)
    tok = jnp.arange(n, dtype=jnp.int32)
    src = (jnp.arange(rows_max, dtype=jnp.int32) % n).at[jnp.concatenate([d_lo, d_hi])].set(
        jnp.concatenate([tok, tok]), unique_indices=True)
    n_tiles = rows_max // tm
    n_used = (seg_end[-1] // tm).astype(jnp.int32)
    tile_ids = jnp.minimum(jnp.arange(n_tiles, dtype=jnp.int32), n_used - 1)
    tile_expert = jnp.minimum(jnp.sum(seg_end[None, :] <= (tile_ids * tm)[:, None], axis=1), n_e - 1).astype(jnp.int32)

    step = GATHER_ROWS // tm * MOE_CHUNK
    ys = None
    for t0 in range(0, n_tiles, step):
        t1 = min(t0 + step, n_tiles)
        xs, = _sc_gather(h2p, [src[t0 * tm:t1 * tm]])
        ys = _expert_call(tile_expert[t0:t1], jnp.clip(n_used - t0, 0, t1 - t0).reshape(1), xs, wg, wu, wd,
                          ys, t0, rows_max, tf=tf)
    n_pad = -(-n // GATHER_ROWS) * GATHER_ROWS
    pad_idx = lambda dd: jnp.concatenate([dd.astype(jnp.int32), jnp.arange(n_pad - n, dtype=jnp.int32)])
    ya, yb = _sc_gather(ys, [pad_idx(d_lo), pad_idx(d_hi)])
    return ya, yb, w


def _combine_kernel(ya_ref, yb_ref, w_ref, x1_ref, gt2_ref, gqf_ref, o_ref):
    w = w_ref[...]
    f = w[:, 0:1] * _load_split(ya_ref) + w[:, 1:2] * _load_split(yb_ref)
    o_ref[0] = x1_ref[0] + gt2_ref[0] * _rms(f, gqf_ref[...])


def _combine_call(ya, yb, w, x1, row0, mod, mod_rows, mod_idx, gqf, *, tm, name):
    b, t, d = x1.shape
    ntb = t // tm
    blk0 = row0 // tm
    assert row0 % tm == 0
    pool = lambda i: (blk0 + i, 0)
    pool3 = lambda i: (0, blk0 + i, 0)
    row = lambda i: (i // ntb, i % ntb, 0)
    return pl.pallas_call(
        _combine_kernel,
        grid=(b * ntb,),
        in_specs=[pl.BlockSpec((2, tm, d // 4), pool3), pl.BlockSpec((2, tm, d // 4), pool3),
                  pl.BlockSpec((tm, 2), pool), pl.BlockSpec((1, tm, d), row),
                  _mod_spec(mod_rows, 5, mod_idx), pl.BlockSpec((1, d), lambda i: (0, 0))],
        out_specs=pl.BlockSpec((1, tm, d), row),
        out_shape=jax.ShapeDtypeStruct((b, t, d), F32),
        compiler_params=_params(1), name=name,
    )(ya, yb, w, x1, mod, gqf)


def _rope_tables(pos):
    half = ROT_DIM // 2
    inv_freq = ROPE_THETA ** (-jnp.arange(half, dtype=F32) * 2.0 / ROT_DIM)
    ang = pos.astype(F32)[:, None] * inv_freq[None, :]
    cos, sin = jnp.cos(ang), jnp.sin(ang)
    l64 = np.arange(128) % HEAD_DIM
    idx = l64 % half
    first = (l64 < half)[None, :]
    second = ((l64 >= half) & (l64 < ROT_DIM))[None, :]
    cos_t = jnp.where(first | second, cos[:, idx], 1.0)
    s1_t = jnp.where(first, -sin[:, idx], 0.0)
    s2_t = jnp.where(second, sin[:, idx], 0.0)
    return cos_t, s1_t, s2_t


def kernel(x_prompt, x_sample, cache_k, cache_v, state_conv, c_prompt, c_sample, w_mod, b_mod, g_pre_mix, g_post_mix, g_pre_ffn, g_post_ffn, w_in, conv_w, conv_b, conv_ln_g, conv_ln_b, w_out, ffn_w_gate, ffn_w_up, ffn_w_down, moe_w_router, moe_w_gate, moe_w_up, moe_w_down):
    bp, seq, d = x_prompt.shape
    nb, n_new, _ = x_sample.shape
    depth = w_mod.shape[0]
    w_buf = cache_k.shape[2]
    past_len = PAST_LEN
    assert w_buf == min(DILATION_PATTERNS[-1][0], past_len)
    keep = min(DILATION_PATTERNS[-1][0], seq)
    ns = nb * n_new

    c_all = jnp.concatenate([c_sample, c_prompt, jnp.zeros((MOD_ROWS - nb - bp, d), F32)], axis=0)
    mod_all = _mod_call(c_all, w_mod, b_mod)

    tm_p = 512
    tabs_p = tuple(tb.reshape(seq // tm_p, tm_p, 128) for tb in _rope_tables(jnp.arange(seq, dtype=jnp.int32)))
    tabs_s = tuple(tb.reshape(n_new, 1, 128)
                   for tb in _rope_tables(past_len + jnp.arange(n_new, dtype=jnp.int32)))

    cache_kt = cache_k.transpose(0, 1, 3, 4, 2)
    cache_vt = cache_v.transpose(0, 1, 3, 4, 2)

    yp = x_prompt
    ys = x_sample.transpose(1, 0, 2).reshape(1, ns, d)
    outs = [[] for _ in range(6)]
    for l in range(depth):
        mod_p = mod_all[l].reshape(MOD_ROWS, 1, 6 * d)
        mod_s = jnp.tile(mod_all[l, :nb], (n_new, 1))[None]
        idx_p2 = lambda bi, i: (nb + bi, 0)
        idx_s2 = lambda bi, i: (0, i)
        w_in_bf = w_in[l].astype(BF16)
        w_out_bf = w_out[l].astype(BF16)
        conv_p = (jnp.pad(conv_w[l], ((0, 1), (0, 0))), conv_b[l][None], conv_ln_g[l][None], conv_ln_b[l][None])
        gpre, gpm, gpf, gqf = g_pre_mix[l][None], g_post_mix[l][None], g_pre_ffn[l][None], g_post_ffn[l][None]

        q, k, v, kt, vt, cv, ul = _inproj_call(yp, mod_p, 1, idx_p2, gpre, w_in_bf, tabs_p, tm_p, conv_p,
                                               tm=tm_p, keep=keep, name=f"inproj_prompt_{l}")
        att = _attn_call(q, k, v)
        outs[0].append(kt.reshape(bp, keep, N_HEADS, HEAD_DIM))
        outs[1].append(vt.reshape(bp, keep, N_HEADS, HEAD_DIM))
        outs[2].append(ul[:, CONV_HALO - (CONV_WIDTH - 1):])

        qs, _, _, kts, vts, us = _inproj_call(ys, mod_s, nb, idx_s2, gpre, w_in_bf, tabs_s, 1, None,
                                              tm=nb, keep=ns, name=f"inproj_sample_{l}")
        to_batch_major = lambda z: z.reshape(n_new, nb, N_HEADS, HEAD_DIM).transpose(1, 0, 2, 3)
        q_bm = (qs.reshape(HEAD_PAIRS, n_new, nb, 2, HEAD_DIM).transpose(2, 1, 0, 3, 4)
                .reshape(nb, n_new, N_HEADS, HEAD_DIM))
        k_bm, v_bm = to_batch_major(kts), to_batch_major(vts)
        head_major = lambda z: jnp.pad(z.transpose(0, 2, 1, 3), ((0, 0), (0, 0), (0, SAMPLE_ROWS - n_new), (0, 0)))
        att_s = _sample_attn_call(head_major(q_bm), head_major(k_bm), head_major(v_bm),
                                  cache_kt, cache_vt, l, n_new)
        att_s = (att_s[:, :, :n_new].reshape(nb, HEAD_PAIRS, 2, n_new, HEAD_DIM).transpose(1, 3, 0, 2, 4)
                 .reshape(1, HEAD_PAIRS, ns, 128).astype(BF16))
        ufull = jnp.concatenate([state_conv[l].transpose(1, 0, 2), us.reshape(n_new, nb, 512)], axis=0)
        cv_s = _sample_conv_call(ufull, conv_p).reshape(1, ns, 512)
        outs[3].append(k_bm)
        outs[4].append(v_bm)
        outs[5].append(ufull[n_new:].transpose(1, 0, 2))

        idx_p1 = lambda i, *_: (nb + i // (seq // tm_f), 0)
        idx_s1 = lambda i, *_: (0, i)
        if l % 2 == 0:
            tm_f = 1024
            wg, wu, wd = ffn_w_gate[l // 2], ffn_w_up[l // 2], ffn_w_down[l // 2]
            yp = _ffn_call(att, cv, yp, mod_p, 1, idx_p1, gpm, gpf, gqf, w_out_bf, wg, wu, wd,
                           tm=tm_f, tf=512, name=f"ffn_prompt_{l}")
            ys = _ffn_call(att_s, cv_s, ys, mod_s, ns, idx_s1, gpm, gpf, gqf, w_out_bf, wg, wu, wd,
                           tm=ns, tf=512, name=f"ffn_sample_{l}")
        else:
            tm_f = 1024
            wr = moe_w_router[l // 2].T
            wg, wu, wd = moe_w_gate[l // 2], moe_w_up[l // 2], moe_w_down[l // 2]
            x1, h2, gates, sel = _router_call(att, cv, yp, mod_p, 1, idx_p1, gpm, gpf, w_out_bf, wr,
                                              tm=tm_f, name=f"router_prompt_{l}")
            x1s, h2s, gates_s, sel_s = _router_call(att_s, cv_s, ys, mod_s, ns, idx_s1, gpm, gpf, w_out_bf, wr,
                                                    tm=ns, name=f"router_sample_{l}")
            pool = lambda a, b_: jnp.concatenate([a.reshape(bp * seq, -1), b_.reshape(ns, -1)], axis=0)
            h2_pool = jnp.concatenate([h2.reshape(2, bp * seq, d // 4), h2s.reshape(2, ns, d // 4)], axis=1)
            ya, yb, w12 = _moe_route(h2_pool, pool(sel, sel_s), pool(gates, gates_s), wg, wu, wd, tf=512)
            yp = _combine_call(ya, yb, w12, x1, 0, mod_p, 1, idx_p1, gqf, tm=tm_f, name=f"moe_combine_prompt_{l}")
            ys = _combine_call(ya, yb, w12, x1s, bp * seq, mod_s, ns, idx_s1, gqf, tm=ns,
                               name=f"moe_combine_sample_{l}")

    y_sample = ys.reshape(n_new, nb, d).transpose(1, 0, 2)
    return (yp, y_sample) + tuple(jnp.stack(o) for o in outs)
```

```python
import functools

import numpy as np
import jax
import jax.numpy as jnp
from jax import lax
from jax.experimental import pallas as pl
from jax.experimental.pallas import tpu as pltpu
from jax.experimental.pallas import tpu_sc as plsc

F32 = jnp.float32
BF16 = jnp.bfloat16

HEAD_DIM = 64
N_HEADS = 8
ATT_WIDTH = N_HEADS * HEAD_DIM
HEAD_PAIRS = ATT_WIDTH // 128
CONV_WIDTH = 31
CONV_HALO = 32
DILATION_PATTERNS = ((128, 1), (512, 4), (2048, 16))
BAND = 128
ROT_DIM = HEAD_DIM // 4
ROPE_THETA = 500000.0
ATTN_SCALE = HEAD_DIM ** -0.5
N_EXPERTS = 8
PAST_LEN = 2048
EPS = 1e-6
NEG = -1e30
MOD_ROWS = 136
VMEM_LIMIT = 56 * 1024 * 1024


def _rms(x, g):
    return x * lax.rsqrt(jnp.mean(x * x, axis=-1, keepdims=True) + EPS) * g


def _silu(x):
    return x * jax.nn.sigmoid(x)


def _params(n_axes, vmem=VMEM_LIMIT):
    return pltpu.CompilerParams(dimension_semantics=("arbitrary",) * n_axes, vmem_limit_bytes=vmem)


def _mod_kernel(c_ref, w_ref, b_ref, o_ref):
    a = _silu(c_ref[...]).astype(BF16)
    o_ref[0] = jnp.dot(a, w_ref[0].astype(BF16), preferred_element_type=F32) + b_ref[0]


def _mod_call(c_all, w_mod, b_mod):
    depth, d, d6 = w_mod.shape
    return pl.pallas_call(
        _mod_kernel,
        grid=(depth, d6 // d),
        in_specs=[pl.BlockSpec((MOD_ROWS, d), lambda l, j: (0, 0)),
                  pl.BlockSpec((1, d, d), lambda l, j: (l, 0, j)),
                  pl.BlockSpec((1, 1, d), lambda l, j: (l, 0, j))],
        out_specs=pl.BlockSpec((1, MOD_ROWS, d), lambda l, j: (l, 0, j)),
        out_shape=jax.ShapeDtypeStruct((depth, MOD_ROWS, d6), F32),
        compiler_params=_params(2),
        name="adaln_mod",
    )(c_all, w_mod, b_mod.reshape(depth, 1, d6))


def _inproj_kernel(*refs, conv, tm):
    if conv:
        (x_ref, sh_ref, sc_ref, g_ref, w_ref, cos_ref, s1_ref, s2_ref,
         cw_ref, cb_ref, lg_ref, lb_ref,
         q_ref, k_ref, v_ref, kt_ref, vt_ref, cv_ref, ul_ref, ubuf) = refs
    else:
        (x_ref, sh_ref, sc_ref, g_ref, w_ref, cos_ref, s1_ref, s2_ref,
         q_ref, k_ref, v_ref, kt_ref, vt_ref, u_ref) = refs

    h = _rms(x_ref[0], g_ref[...]) * (1.0 + sc_ref[0]) + sh_ref[0]
    proj = jnp.dot(h.astype(BF16), w_ref[...], preferred_element_type=F32)
    cos, s1, s2 = cos_ref[0], s1_ref[0], s2_ref[0]

    def rope(z):
        return z * cos + pltpu.roll(z, 128 - ROT_DIM // 2, 1) * s1 + pltpu.roll(z, ROT_DIM // 2, 1) * s2

    for hp in range(HEAD_PAIRS):
        lo, hi = hp * 128, (hp + 1) * 128
        q_ref[0, hp] = rope(proj[:, lo:hi]) * ATTN_SCALE
        kz = rope(proj[:, ATT_WIDTH + lo:ATT_WIDTH + hi])
        k_ref[0, hp] = kz
        kt_ref[0, :, lo:hi] = kz
        vz = proj[:, 2 * ATT_WIDTH + lo:2 * ATT_WIDTH + hi]
        v_ref[0, hp] = vz
        vt_ref[0, :, lo:hi] = vz

    a = proj[:, 3 * ATT_WIDTH:3 * ATT_WIDTH + 512]
    gate = proj[:, 3 * ATT_WIDTH + 512:]
    u = a * jax.nn.sigmoid(gate)
    if not conv:
        u_ref[0] = u
        return

    @pl.when(pl.program_id(1) == 0)
    def _():
        ubuf[0, 0:CONV_HALO, :] = jnp.zeros((CONV_HALO, 512), F32)

    ubuf[0, CONV_HALO:CONV_HALO + tm, :] = u
    span = tm + CONV_HALO - 8
    for s in range(1, 8):
        ubuf[s, 0:span, :] = ubuf[0, s:s + span, :]
    off0 = CONV_HALO - (CONV_WIDTH - 1)
    rows = 32

    def chunk(r, carry):
        base = pl.multiple_of(r * rows, rows)
        acc = jnp.broadcast_to(cb_ref[...], (rows, 512))
        for j in range(CONV_WIDTH):
            a, s = divmod(off0 + j, 8)
            acc = acc + cw_ref[j:j + 1, :] * ubuf[s, pl.ds(base + 8 * a, rows), :]
        mu = jnp.mean(acc, axis=-1, keepdims=True)
        xc = acc - mu
        var = jnp.mean(xc * xc, axis=-1, keepdims=True)
        y = xc * lax.rsqrt(var + EPS) * lg_ref[...] + lb_ref[...]
        cv_ref[0, pl.ds(base, rows), :] = _silu(y).astype(BF16)
        return carry

    lax.fori_loop(0, tm // rows, chunk, 0, unroll=4)
    tail = ubuf[0, tm:tm + CONV_HALO, :]
    ul_ref[0] = tail
    ubuf[0, 0:CONV_HALO, :] = tail


def _mod_spec(rows, chunk, index_fn):
    return pl.BlockSpec((1, rows, 1024), lambda *g: index_fn(*g) + (chunk,))


def _inproj_call(x, mod, mod_rows, mod_idx, g, w_bf, tabs, tab_rows, conv_p, *, tm, keep, name):
    b, t, d = x.shape
    nt = t // tm
    off = (t - keep) // tm
    conv = conv_p is not None
    tab_spec = pl.BlockSpec((1, tab_rows, 128), lambda bi, i: (i, 0, 0))
    in_specs = [pl.BlockSpec((1, tm, d), lambda bi, i: (bi, i, 0)),
                _mod_spec(mod_rows, 0, mod_idx), _mod_spec(mod_rows, 1, mod_idx),
                pl.BlockSpec((1, d), lambda bi, i: (0, 0)),
                pl.BlockSpec(w_bf.shape, lambda bi, i: (0, 0)),
                tab_spec, tab_spec, tab_spec]
    args = [x, mod, mod, g, w_bf, *tabs]
    hp_spec = pl.BlockSpec((1, HEAD_PAIRS, tm, 128), lambda bi, i: (bi, 0, i, 0))
    tail_spec = pl.BlockSpec((1, tm, 512), lambda bi, i: (bi, jnp.maximum(i - off, 0), 0))
    hp_shape = jax.ShapeDtypeStruct((b, HEAD_PAIRS, t, 128), F32)
    tail_shape = jax.ShapeDtypeStruct((b, keep, 512), F32)
    out_specs = [hp_spec, hp_spec, hp_spec, tail_spec, tail_spec]
    out_shape = [hp_shape, hp_shape, hp_shape, tail_shape, tail_shape]
    scratch = []
    if conv:
        small = pl.BlockSpec((1, 512), lambda bi, i: (0, 0))
        in_specs += [pl.BlockSpec((32, 512), lambda bi, i: (0, 0)), small, small, small]
        args += list(conv_p)
        out_specs += [pl.BlockSpec((1, tm, 512), lambda bi, i: (bi, i, 0)),
                      pl.BlockSpec((1, CONV_HALO, 512), lambda bi, i: (bi, 0, 0))]
        out_shape += [jax.ShapeDtypeStruct((b, t, 512), BF16),
                      jax.ShapeDtypeStruct((b, CONV_HALO, 512), F32)]
        scratch = [pltpu.VMEM((8, tm + CONV_HALO, 512), F32)]
    else:
        out_specs += [pl.BlockSpec((1, tm, 512), lambda bi, i: (bi, i, 0))]
        out_shape += [jax.ShapeDtypeStruct((b, t, 512), F32)]
    return pl.pallas_call(
        functools.partial(_inproj_kernel, conv=conv, tm=tm),
        grid=(b, nt), in_specs=in_specs, out_specs=out_specs, out_shape=out_shape,
        scratch_shapes=scratch, compiler_params=_params(2), name=name,
    )(*args)


def _attn_kernel(q_ref, kc_ref, kp_ref, vc_ref, vp_ref, o_ref, acc_s, m_s, l_s, *, tq):
    first_kj = jnp.where(pl.program_id(2) == 0, 128, 0)
    lane_lo = lax.broadcasted_iota(jnp.int32, (128, 128), 1) < HEAD_DIM
    qi = lax.broadcasted_iota(jnp.int32, (256, 256), 0)
    qi = jnp.where(qi >= 128, qi - 128, qi)
    kj = lax.broadcasted_iota(jnp.int32, (256, 256), 1)
    band = (kj >= qi) & (kj <= qi + BAND)
    band_first = band & (kj >= first_kj)

    ones_cols = jnp.ones((256, 128), BF16)
    nt = (((1,), (1,)), ((), ()))

    def strided(start, size, d):
        if d > 1:
            return pl.ds(start, size, stride=d)
        return pl.ds(start if isinstance(start, int) else pl.multiple_of(start, 128), size)

    def blocks(items):
        loaded = []
        for rows, first, c, d, s in items:
            if first:
                take = lambda rp, rc: jnp.concatenate(
                    [rp[0, 0, strided(tq - BAND * d + c, 128, d), :],
                     rc[0, 0, strided(c, 128, d), :]], axis=0)
            else:
                krows = strided(c + d * 128 * (s - 1), 256, d)
                take = lambda rp, rc: rc[0, 0, krows, :]
            loaded.append((q_ref[0, 0, rows, :], take(kp_ref, kc_ref), take(vp_ref, vc_ref),
                           band_first if first else band))
        scores = []
        for q2, k2, _, _ in loaded:
            qq = jnp.concatenate([jnp.where(lane_lo, q2, 0.0), jnp.where(lane_lo, 0.0, q2)], axis=0)
            scores.append(lax.dot_general(qq.astype(BF16), k2.astype(BF16), nt,
                                          preferred_element_type=F32))
        probs = []
        for sc, (_, _, _, mask) in zip(scores, loaded):
            sc = jnp.where(mask, sc, NEG)
            mb = jnp.max(sc, axis=1, keepdims=True)
            probs.append((mb, jnp.exp(sc - mb).astype(BF16)))
        out = []
        for (mb, p), (_, _, v2, _), item in zip(probs, loaded, items):
            pv = jnp.dot(p, jnp.concatenate([v2.astype(BF16), ones_cols], axis=1), preferred_element_type=F32)
            out.append((item[0], jnp.where(lane_lo, mb[:128], mb[128:]),
                        jnp.where(lane_lo, pv[:128, 128:], pv[128:, 128:]),
                        jnp.where(lane_lo, pv[:128, :128], pv[128:, :128])))
        return out

    def merge(rows, mt, lt, pvt):
        mp, lp, ap = m_s[rows, :], l_s[rows, :], acc_s[rows, :]
        mn = jnp.maximum(mp, mt)
        a = jnp.exp(mp - mn)
        b = jnp.exp(mt - mn)
        return mn, a * lp + b * lt, a * ap + b * pvt

    def store_init(results):
        for rows, mt, lt, pvt in results:
            m_s[rows, :] = mt
            l_s[rows, :] = lt
            acc_s[rows, :] = pvt

    def store_merged(results):
        for rows, mt, lt, pvt in results:
            mn, ln, an = merge(rows, mt, lt, pvt)
            m_s[rows, :] = mn
            l_s[rows, :] = ln
            acc_s[rows, :] = an

    def store_output(results):
        for rows, mt, lt, pvt in results:
            _, ln, an = merge(rows, mt, lt, pvt)
            o_ref[0, 0, rows, :] = (an / ln).astype(o_ref.dtype)

    def loop(n, fn):
        def body(i, carry):
            fn(i)
            return carry
        lax.fori_loop(0, n, body, 0)

    group = 4
    loop(16 // group, lambda g: store_init(blocks(
        [(pl.ds(g * group + j, 128, stride=16), True, g * group + j, 16, 0) for j in range(group)])))

    store_merged(blocks([(pl.ds(c, 128, stride=4), True, c, 4, 0) for c in range(4)]))
    loop(tq // 512 - 1, lambda n: store_merged(blocks(
        [(pl.ds(c + 512 * (n + 1), 128, stride=4), False, c, 4, n + 1) for c in range(4)])))

    store_output(blocks([(strided(0, 128, 1), True, 0, 1, 0)]
                        + [(strided(128 * s, 128, 1), False, 0, 1, s) for s in range(1, group)]))
    rest = 4
    assert (tq // 128 - group) % rest == 0
    loop((tq // 128 - group) // rest, lambda g: store_output(blocks(
        [(strided(128 * (group + g * rest + j), 128, 1), False, 0, 1, group + g * rest + j) for j in range(rest)])))


def _attn_call(q, k, v, *, tq=2048):
    b, hp, t, _ = q.shape
    assert tq == BAND * 16 and t % tq == 0
    cur = pl.BlockSpec((1, 1, tq, 128), lambda bi, h, i: (bi, h, i, 0))
    prev = pl.BlockSpec((1, 1, tq, 128), lambda bi, h, i: (bi, h, jnp.maximum(i - 1, 0), 0))
    return pl.pallas_call(
        functools.partial(_attn_kernel, tq=tq),
        grid=(b, hp, t // tq),
        in_specs=[cur, cur, prev, cur, prev],
        out_specs=cur,
        out_shape=jax.ShapeDtypeStruct((b, hp, t, 128), BF16),
        scratch_shapes=[pltpu.VMEM((tq, 128), F32)] * 3,
        compiler_params=_params(3), name="dilated_attn_prompt",
    )(q, k, k, v, v)


SAMPLE_ROWS = 8
SAMPLE_STEP = 2


def _sample_attn_kernel(q_ref, kn_ref, vn_ref, kt_ref, vt_ref, mh_ref, mn_ref, o_ref):
    mh, mn = mh_ref[...], mn_ref[...]
    nt = (((1,), (1,)), ((), ()))
    for b, h in [(b, h) for b in range(q_ref.shape[0]) for h in range(N_HEADS)]:
        qh = q_ref[b, h].astype(BF16)
        s = jnp.dot(qh, kt_ref[0, b, h].astype(BF16), preferred_element_type=F32)
        sn = lax.dot_general(qh, kn_ref[b, h].astype(BF16), nt, preferred_element_type=F32)
        s = jnp.where(mh > 0, s, NEG)
        sn = jnp.where(mn > 0, sn, NEG)
        m = jnp.maximum(jnp.max(s, axis=1, keepdims=True), jnp.max(sn, axis=1, keepdims=True))
        p = mh * jnp.exp(s - m)
        pn = mn * jnp.exp(sn - m)
        den = jnp.sum(p, axis=1, keepdims=True) + jnp.sum(pn, axis=1, keepdims=True)
        num = (lax.dot_general(p.astype(BF16), vt_ref[0, b, h].astype(BF16), nt, preferred_element_type=F32)
               + jnp.dot(pn.astype(BF16), vn_ref[b, h].astype(BF16), preferred_element_type=F32))
        o_ref[b, h] = num / jnp.where(den > 0, den, 1.0)


def _sample_tables(w_buf, n_new):
    rows = np.arange(w_buf + n_new)
    mult = np.zeros((SAMPLE_ROWS, w_buf + SAMPLE_ROWS), np.float32)
    for t in range(n_new):
        dist = w_buf + t - rows
        for window, dil in DILATION_PATTERNS:
            mult[t, :w_buf + n_new] += (dist >= 0) & (dist <= window) & (dist % dil == 0)
    return jnp.asarray(mult[:, :w_buf]), jnp.asarray(mult[:, w_buf:])


def _sample_attn_call(q, kn, vn, cache_kt, cache_vt, layer, n_new):
    nb = q.shape[0]
    w_buf = cache_kt.shape[-1]
    mh, mn = _sample_tables(w_buf, n_new)
    step = SAMPLE_STEP
    new_spec = pl.BlockSpec((step, N_HEADS, SAMPLE_ROWS, HEAD_DIM), lambda i: (i, 0, 0, 0))
    cache_spec = pl.BlockSpec((1, step, N_HEADS, HEAD_DIM, w_buf), lambda i: (layer, i, 0, 0, 0))
    return pl.pallas_call(
        _sample_attn_kernel,
        grid=(nb // step,),
        in_specs=[new_spec, new_spec, new_spec, cache_spec, cache_spec,
                  pl.BlockSpec(mh.shape, lambda i: (0, 0)), pl.BlockSpec(mn.shape, lambda i: (0, 0))],
        out_specs=new_spec,
        out_shape=jax.ShapeDtypeStruct((nb, N_HEADS, SAMPLE_ROWS, HEAD_DIM), F32),
        compiler_params=_params(1), name="dilated_attn_sample",
    )(q, kn, vn, cache_kt, cache_vt, mh, mn)


def _sample_conv_kernel(uf_ref, cw_ref, cb_ref, lg_ref, lb_ref, o_ref, *, n_new, nb):
    rows = 32

    def body(n, carry):
        t = n // (nb // rows)
        base = pl.multiple_of((n % (nb // rows)) * rows, rows)
        acc = jnp.broadcast_to(cb_ref[...], (rows, 512))
        for j in range(CONV_WIDTH):
            acc = acc + cw_ref[j:j + 1, :] * uf_ref[t + j, pl.ds(base, rows), :]
        mu = jnp.mean(acc, axis=-1, keepdims=True)
        xc = acc - mu
        var = jnp.mean(xc * xc, axis=-1, keepdims=True)
        y = xc * lax.rsqrt(var + EPS) * lg_ref[...] + lb_ref[...]
        o_ref[t, pl.ds(base, rows), :] = _silu(y).astype(BF16)
        return carry

    lax.fori_loop(0, n_new * (nb // rows), body, 0)


def _sample_conv_call(ufull, conv_p):
    n_full, nb, ch = ufull.shape
    n_new = n_full - (CONV_WIDTH - 1)
    return pl.pallas_call(
        functools.partial(_sample_conv_kernel, n_new=n_new, nb=nb),
        out_shape=jax.ShapeDtypeStruct((n_new, nb, ch), BF16),
        compiler_params=pltpu.CompilerParams(vmem_limit_bytes=VMEM_LIMIT),
        name="conv_sample",
    )(ufull, *conv_p)


def _mix_residual(att_ref, cv_ref, x_ref, gt1_ref, sc2_ref, sh2_ref, gpm_ref, gpf_ref, wo_ref):
    a = jnp.concatenate([att_ref[0, hp] for hp in range(HEAD_PAIRS)] + [cv_ref[0]], axis=-1)
    mix = jnp.dot(a, wo_ref[...], preferred_element_type=F32)
    x1 = x_ref[0] + gt1_ref[0] * _rms(mix, gpm_ref[...])
    h2 = _rms(x1, gpf_ref[...]) * (1.0 + sc2_ref[0]) + sh2_ref[0]
    return x1, h2


def _ffn_kernel(att_ref, cv_ref, x_ref, gt1_ref, sc2_ref, sh2_ref, gt2_ref, gpm_ref, gpf_ref, gqf_ref,
                wo_ref, wg_ref, wu_ref, wd_ref, o_ref, x1_s, h2_s, acc_s):
    f = pl.program_id(1)

    @pl.when(f == 0)
    def _():
        x1, h2 = _mix_residual(att_ref, cv_ref, x_ref, gt1_ref, sc2_ref, sh2_ref, gpm_ref, gpf_ref, wo_ref)
        x1_s[...] = x1
        h2_s[...] = h2.astype(BF16)
        acc_s[...] = jnp.zeros_like(acc_s)

    h2 = h2_s[...]
    g = jnp.dot(h2, wg_ref[...].astype(BF16), preferred_element_type=F32)
    u = jnp.dot(h2, wu_ref[...].astype(BF16), preferred_element_type=F32)
    act = (_silu(g) * u).astype(BF16)
    acc_s[...] += jnp.dot(act, wd_ref[...].astype(BF16), preferred_element_type=F32)

    @pl.when(f == pl.num_programs(1) - 1)
    def _():
        o_ref[0] = x1_s[...] + gt2_ref[0] * _rms(acc_s[...], gqf_ref[...])


def _mix_in_specs(b, t, tm, mod_rows, mod_idx, d):
    ntb = t // tm
    row = lambda i, *_: (i // ntb, i % ntb, 0)
    specs = [pl.BlockSpec((1, HEAD_PAIRS, tm, 128), lambda i, *_: (i // ntb, 0, i % ntb, 0)),
             pl.BlockSpec((1, tm, 512), row),
             pl.BlockSpec((1, tm, d), row)]
    return specs, row


def _ffn_call(att, cv, x, mod, mod_rows, mod_idx, gpm, gpf, gqf, wo_bf, wg, wu, wd, *, tm, tf, name):
    b, t, d = x.shape
    ff = wg.shape[1]
    ntb = t // tm
    specs, row = _mix_in_specs(b, t, tm, mod_rows, mod_idx, d)
    vec = pl.BlockSpec((1, d), lambda i, f: (0, 0))
    in_specs = specs + [_mod_spec(mod_rows, 2, mod_idx), _mod_spec(mod_rows, 4, mod_idx),
                        _mod_spec(mod_rows, 3, mod_idx), _mod_spec(mod_rows, 5, mod_idx),
                        vec, vec, vec,
                        pl.BlockSpec((d, d), lambda i, f: (0, 0)),
                        pl.BlockSpec((d, tf), lambda i, f: (0, f)),
                        pl.BlockSpec((d, tf), lambda i, f: (0, f)),
                        pl.BlockSpec((tf, d), lambda i, f: (f, 0))]
    return pl.pallas_call(
        _ffn_kernel,
        grid=(b * ntb, ff // tf),
        in_specs=in_specs,
        out_specs=pl.BlockSpec((1, tm, d), row),
        out_shape=jax.ShapeDtypeStruct((b, t, d), F32),
        scratch_shapes=[pltpu.VMEM((tm, d), F32), pltpu.VMEM((tm, d), BF16), pltpu.VMEM((tm, d), F32)],
        compiler_params=_params(2), name=name,
    )(att, cv, x, mod, mod, mod, mod, gpm, gpf, gqf, wo_bf, wg, wu, wd)


def _router_kernel(att_ref, cv_ref, x_ref, gt1_ref, sc2_ref, sh2_ref, gpm_ref, gpf_ref, wo_ref, wr_ref,
                   x1_ref, h2_ref, gate_ref, sel_ref):
    x1, h2 = _mix_residual(att_ref, cv_ref, x_ref, gt1_ref, sc2_ref, sh2_ref, gpm_ref, gpf_ref, wo_ref)
    x1_ref[0] = x1
    _store_split(h2_ref.at[:, 0], h2)
    logits = jnp.concatenate([jnp.sum(h2 * wr_ref[e:e + 1, :], axis=1, keepdims=True)
                              for e in range(N_EXPERTS)], axis=1)
    lane = lax.broadcasted_iota(jnp.int32, logits.shape, 1).astype(F32)
    v1 = jnp.max(logits, axis=-1, keepdims=True)
    i1 = jnp.min(jnp.where(logits == v1, lane, float(N_EXPERTS)), axis=-1, keepdims=True)
    oh1 = lane == i1
    rest = jnp.where(oh1, -jnp.inf, logits)
    v2 = jnp.max(rest, axis=-1, keepdims=True)
    i2 = jnp.min(jnp.where(rest == v2, lane, float(N_EXPERTS)), axis=-1, keepdims=True)
    oh2 = lane == i2
    e2 = jnp.exp(v2 - v1)
    den = 1.0 + e2
    gate_ref[0] = jnp.where(oh1, 1.0 / den, 0.0) + jnp.where(oh2, e2 / den, 0.0)
    sel_ref[0] = jnp.where(oh1 | oh2, 1.0, 0.0)


def _router_call(att, cv, x, mod, mod_rows, mod_idx, gpm, gpf, wo_bf, wr, *, tm, name):
    b, t, d = x.shape
    specs, row = _mix_in_specs(b, t, tm, mod_rows, mod_idx, d)
    vec = pl.BlockSpec((1, d), lambda i: (0, 0))
    in_specs = specs + [_mod_spec(mod_rows, 2, mod_idx), _mod_spec(mod_rows, 4, mod_idx),
                        _mod_spec(mod_rows, 3, mod_idx), vec, vec,
                        pl.BlockSpec((d, d), lambda i: (0, 0)),
                        pl.BlockSpec((N_EXPERTS, d), lambda i: (0, 0))]
    return pl.pallas_call(
        _router_kernel,
        grid=(b * (t // tm),),
        in_specs=in_specs,
        out_specs=[pl.BlockSpec((1, tm, d), row),
                   pl.BlockSpec((2, 1, tm, d // 4), lambda i: (0,) + row(i)),
                   pl.BlockSpec((1, tm, N_EXPERTS), row), pl.BlockSpec((1, tm, N_EXPERTS), row)],
        out_shape=[jax.ShapeDtypeStruct((b, t, d), F32), jax.ShapeDtypeStruct((2, b, t, d // 4), jnp.int32),
                   jax.ShapeDtypeStruct((b, t, N_EXPERTS), F32), jax.ShapeDtypeStruct((b, t, N_EXPERTS), F32)],
        compiler_params=_params(1), name=name,
    )(att, cv, x, mod, mod, mod, gpm, gpf, wo_bf, wr)


MOE_TILE = 1024
RANK_BLOCK = 1536
GATHER_WINDOW = 128
GATHER_ROWS = 32 * GATHER_WINDOW
MOE_CHUNK = 3
GATHER_DEPTH = 3


def _pack_pairs(x):
    w = x.shape[1] // 2
    lo = pltpu.bitcast(x[:, :w].astype(BF16).astype(F32), jnp.int32)
    hi = pltpu.bitcast(x[:, w:].astype(BF16).astype(F32), jnp.int32)
    return hi | lax.shift_right_logical(lo, 16)


def _unpack_pairs(words):
    lo = pltpu.bitcast(lax.shift_left(words, 16), F32)
    hi = pltpu.bitcast(words & -65536, F32)
    return jnp.concatenate([lo, hi], axis=1)


def _store_split(ref, x):
    words = _pack_pairs(x)
    q = words.shape[1] // 2
    ref[0] = words[:, :q]
    ref[1] = words[:, q:]


def _load_split(ref):
    return _unpack_pairs(jnp.concatenate([ref[0], ref[1]], axis=1))


def _rank_kernel(sel_ref, rank_ref, cnt_ref, carry):
    @pl.when(pl.program_id(0) == 0)
    def _():
        carry[...] = jnp.zeros_like(carry)

    sel = sel_ref[...]
    rb = sel.shape[0]
    before = (lax.broadcasted_iota(jnp.int32, (rb, rb), 1)
              < lax.broadcasted_iota(jnp.int32, (rb, rb), 0)).astype(BF16)
    rank_ref[...] = jnp.dot(before, sel.astype(BF16), preferred_element_type=F32) + carry[...]
    carry[...] += jnp.sum(sel, axis=0, keepdims=True)
    cnt_ref[...] = carry[...]


def _rank_call(sel):
    n, n_e = sel.shape
    rb = RANK_BLOCK
    assert n % rb == 0
    return pl.pallas_call(
        _rank_kernel,
        grid=(n // rb,),
        in_specs=[pl.BlockSpec((rb, n_e), lambda i: (i, 0))],
        out_specs=[pl.BlockSpec((rb, n_e), lambda i: (i, 0)), pl.BlockSpec((1, n_e), lambda i: (0, 0))],
        out_shape=[jax.ShapeDtypeStruct((n, n_e), F32), jax.ShapeDtypeStruct((1, n_e), F32)],
        scratch_shapes=[pltpu.VMEM((1, n_e), F32)],
        compiler_params=_params(1), name="moe_rank",
    )(sel)


def _source_kernel(dlo_ref, dhi_ref, gap_lo_ref, gap_hi_ref, src_ref):
    n = dlo_ref.shape[0]

    def place(i, carry):
        src_ref[dlo_ref[i]] = i
        src_ref[dhi_ref[i]] = i
        return carry

    lax.fori_loop(0, n, place, 0, unroll=8)

    mask = (1 << (n.bit_length() - 1)) - 1

    def fill(j, carry):
        src_ref[j] = j & mask
        return carry

    for r in range(gap_lo_ref.shape[0]):
        lax.fori_loop(gap_lo_ref[r], gap_hi_ref[r], fill, 0)


def _source_call(d_lo, d_hi, gap_lo, gap_hi, rows):
    smem = pl.BlockSpec(memory_space=pltpu.SMEM)
    return pl.pallas_call(
        _source_kernel,
        in_specs=[smem] * 4, out_specs=smem,
        out_shape=jax.ShapeDtypeStruct((rows,), jnp.int32),
        name="moe_source_rows",
    )(d_lo.astype(jnp.int32), d_hi.astype(jnp.int32), gap_lo.astype(jnp.int32), gap_hi.astype(jnp.int32))


def _sc_gather(table, idxs):
    halves, _, width = table.shape
    n_lists, n = len(idxs), idxs[0].shape[0]
    info = plsc.get_sparse_core_info()
    win, depth = GATHER_WINDOW, GATHER_DEPTH
    assert n % GATHER_ROWS == 0 and GATHER_ROWS == info.num_cores * info.num_subcores * win
    wins = n // GATHER_ROWS
    jobs = [(l, j, h) for l in range(n_lists) for j in range(wins) for h in range(halves)]
    mesh = plsc.VectorSubcoreMesh(core_axis_name="core", subcore_axis_name="subcore")
    out_type = [jax.ShapeDtypeStruct((halves, n, width), table.dtype) for _ in idxs]
    scratch = [pltpu.VMEM((n_lists, wins, win), jnp.int32), pltpu.VMEM((depth, win, width), table.dtype),
               pltpu.SemaphoreType.DMA((depth,)), pltpu.SemaphoreType.DMA((depth,))]

    @functools.partial(pl.kernel, out_type=out_type, mesh=mesh, scratch_types=scratch)
    def gather(table_hbm, idx_hbm, *refs):
        out_refs = refs[:n_lists]
        idx_v, buf, sem_in, sem_out = refs[n_lists:]
        worker = lax.axis_index("subcore") * info.num_cores + lax.axis_index("core")
        first = worker * wins
        for l in range(n_lists):
            pltpu.sync_copy(idx_hbm.at[l, worker], idx_v.at[l])
        for g in range(0, len(jobs), depth):
            group = jobs[g:g + depth]
            reads = [pltpu.async_copy(table_hbm.at[h].at[idx_v.at[l, j]], buf.at[k], sem_in.at[k])
                     for k, (l, j, h) in enumerate(group)]
            writes = []
            for k, (l, j, h) in enumerate(group):
                reads[k].wait()
                rows = pl.ds(pl.multiple_of((first + j) * win, win), win)
                writes.append(pltpu.async_copy(buf.at[k], out_refs[l].at[h, rows], sem_out.at[k]))
            for write in writes:
                write.wait()

    return gather(table, jnp.stack(idxs).reshape(n_lists, n // (wins * win), wins, win))


def _expert_kernel(te_ref, nu_ref, xs_ref, wg_ref, wu_ref, wd_ref, *refs):
    ys_ref, xb, acc = refs[-3:]
    t, f = pl.program_id(0), pl.program_id(1)
    last = pl.num_programs(1) - 1
    used = t < nu_ref[0]

    @pl.when(used & (f == 0))
    def _():
        xb[...] = _load_split(xs_ref).astype(BF16)
        acc[...] = jnp.zeros_like(acc)

    @pl.when(used)
    def _():
        x = xb[...]
        g = jnp.dot(x, wg_ref[0].astype(BF16), preferred_element_type=F32)
        u = jnp.dot(x, wu_ref[0].astype(BF16), preferred_element_type=F32)
        act = (_silu(g) * u).astype(BF16)
        acc[...] += jnp.dot(act, wd_ref[0].astype(BF16), preferred_element_type=F32)

    @pl.when(used & (f == last))
    def _():
        _store_split(ys_ref, acc[...])

    @pl.when(jnp.logical_not(used) & (f == last))
    def _():
        ys_ref[...] = jnp.zeros_like(ys_ref)


def _expert_call(tile_expert, n_used, xs, wg, wu, wd, ys, tile0, rows_total, *, tf):
    _, rows, quarter = xs.shape
    n_e, d, ff = wg.shape
    tm = MOE_TILE
    nf = ff // tf
    chunk = lambda t, f, te, nu: jnp.where(t < nu[0], f, nf - 1)
    in_specs = [pl.BlockSpec((2, tm, quarter), lambda t, f, te, nu: (0, t, 0)),
                pl.BlockSpec((1, d, tf), lambda t, f, te, nu: (te[t], 0, chunk(t, f, te, nu))),
                pl.BlockSpec((1, d, tf), lambda t, f, te, nu: (te[t], 0, chunk(t, f, te, nu))),
                pl.BlockSpec((1, tf, d), lambda t, f, te, nu: (te[t], chunk(t, f, te, nu), 0))]
    args = [tile_expert, n_used, xs, wg, wu, wd]
    aliases = {}
    if ys is not None:
        in_specs.append(pl.BlockSpec(memory_space=pl.ANY))
        args.append(ys)
        aliases = {len(args) - 1: 0}
    grid_spec = pltpu.PrefetchScalarGridSpec(
        num_scalar_prefetch=2,
        grid=(rows // tm, nf),
        in_specs=in_specs,
        out_specs=pl.BlockSpec((2, tm, quarter), lambda t, f, te, nu: (0, tile0 + t, 0)),
        scratch_shapes=[pltpu.VMEM((tm, d), BF16), pltpu.VMEM((tm, d), F32)])
    return pl.pallas_call(
        _expert_kernel, grid_spec=grid_spec,
        out_shape=jax.ShapeDtypeStruct((2, rows_total, quarter), jnp.int32),
        input_output_aliases=aliases,
        compiler_params=_params(2), name="moe_experts",
    )(*args)


def _moe_route(h2p, sel, gates, wg, wu, wd, *, tf):
    n, n_e = sel.shape
    tm = MOE_TILE
    rank, cnt = _rank_call(sel)
    counts = cnt[0].astype(jnp.int32)
    padded = (counts + tm - 1) // tm * tm
    seg_end = jnp.cumsum(padded)
    seg_start = seg_end - padded
    rows_max = -(-(2 * n + n_e * tm) // GATHER_ROWS) * GATHER_ROWS
    assert rows_max % tm == 0
    lanes = jnp.arange(n_e, dtype=jnp.int32)[None, :]
    e_lo = jnp.min(jnp.where(sel > 0, lanes, n_e - 1), axis=1)
    e_hi = jnp.max(jnp.where(sel > 0, lanes, 0), axis=1)
    pick = lambda a, e: jnp.take_along_axis(a, e[:, None], axis=1)[:, 0]
    rank_i = rank.astype(jnp.int32)
    d_lo = seg_start[e_lo] + pick(rank_i, e_lo)
    d_hi = seg_start[e_hi] + pick(rank_i, e_hi)
    w = jnp.stack([pick(gates, e_lo), pick(gates, e_hi)], axis=1)
    src = _source_call(d_lo, d_hi, jnp.append(seg_start + counts, seg_end[-1]), jnp.append(seg_end, rows_max),
                       rows_max)
    n_tiles = rows_max // tm
    n_used = (seg_end[-1] // tm).astype(jnp.int32)
    tile_ids = jnp.minimum(jnp.arange(n_tiles, dtype=jnp.int32), n_used - 1)
    tile_expert = jnp.minimum(jnp.sum(seg_end[None, :] <= (tile_ids * tm)[:, None], axis=1), n_e - 1).astype(jnp.int32)

    step = GATHER_ROWS // tm * MOE_CHUNK
    ys = None
    for t0 in range(0, n_tiles, step):
        t1 = min(t0 + step, n_tiles)
        xs, = _sc_gather(h2p, [src[t0 * tm:t1 * tm]])
        ys = _expert_call(tile_expert[t0:t1], jnp.clip(n_used - t0, 0, t1 - t0).reshape(1), xs, wg, wu, wd,
                          ys, t0, rows_max, tf=tf)
    n_pad = -(-n // GATHER_ROWS) * GATHER_ROWS
    pad_idx = lambda dd: jnp.concatenate([dd.astype(jnp.int32), jnp.arange(n_pad - n, dtype=jnp.int32)])
    ya, yb = _sc_gather(ys, [pad_idx(d_lo), pad_idx(d_hi)])
    return ya, yb, w


def _combine_kernel(ya_ref, yb_ref, w_ref, x1_ref, gt2_ref, gqf_ref, o_ref):
    w = w_ref[...]
    f = w[:, 0:1] * _load_split(ya_ref) + w[:, 1:2] * _load_split(yb_ref)
    o_ref[0] = x1_ref[0] + gt2_ref[0] * _rms(f, gqf_ref[...])


def _combine_call(ya, yb, w, x1, row0, mod, mod_rows, mod_idx, gqf, *, tm, name):
    b, t, d = x1.shape
    ntb = t // tm
    blk0 = row0 // tm
    assert row0 % tm == 0
    pool = lambda i: (blk0 + i, 0)
    pool3 = lambda i: (0, blk0 + i, 0)
    row = lambda i: (i // ntb, i % ntb, 0)
    return pl.pallas_call(
        _combine_kernel,
        grid=(b * ntb,),
        in_specs=[pl.BlockSpec((2, tm, d // 4), pool3), pl.BlockSpec((2, tm, d // 4), pool3),
                  pl.BlockSpec((tm, 2), pool), pl.BlockSpec((1, tm, d), row),
                  _mod_spec(mod_rows, 5, mod_idx), pl.BlockSpec((1, d), lambda i: (0, 0))],
        out_specs=pl.BlockSpec((1, tm, d), row),
        out_shape=jax.ShapeDtypeStruct((b, t, d), F32),
        compiler_params=_params(1), name=name,
    )(ya, yb, w, x1, mod, gqf)


def _rope_tables(pos):
    half = ROT_DIM // 2
    inv_freq = ROPE_THETA ** (-jnp.arange(half, dtype=F32) * 2.0 / ROT_DIM)
    ang = pos.astype(F32)[:, None] * inv_freq[None, :]
    cos, sin = jnp.cos(ang), jnp.sin(ang)
    l64 = np.arange(128) % HEAD_DIM
    idx = l64 % half
    first = (l64 < half)[None, :]
    second = ((l64 >= half) & (l64 < ROT_DIM))[None, :]
    cos_t = jnp.where(first | second, cos[:, idx], 1.0)
    s1_t = jnp.where(first, -sin[:, idx], 0.0)
    s2_t = jnp.where(second, sin[:, idx], 0.0)
    return cos_t, s1_t, s2_t


def kernel(x_prompt, x_sample, cache_k, cache_v, state_conv, c_prompt, c_sample, w_mod, b_mod, g_pre_mix, g_post_mix, g_pre_ffn, g_post_ffn, w_in, conv_w, conv_b, conv_ln_g, conv_ln_b, w_out, ffn_w_gate, ffn_w_up, ffn_w_down, moe_w_router, moe_w_gate, moe_w_up, moe_w_down):
    bp, seq, d = x_prompt.shape
    nb, n_new, _ = x_sample.shape
    depth = w_mod.shape[0]
    w_buf = cache_k.shape[2]
    past_len = PAST_LEN
    assert w_buf == min(DILATION_PATTERNS[-1][0], past_len)
    keep = min(DILATION_PATTERNS[-1][0], seq)
    ns = nb * n_new

    c_all = jnp.concatenate([c_sample, c_prompt, jnp.zeros((MOD_ROWS - nb - bp, d), F32)], axis=0)
    mod_all = _mod_call(c_all, w_mod, b_mod)

    tm_p = 512
    tabs_p = tuple(tb.reshape(seq // tm_p, tm_p, 128) for tb in _rope_tables(jnp.arange(seq, dtype=jnp.int32)))
    tabs_s = tuple(tb.reshape(n_new, 1, 128)
                   for tb in _rope_tables(past_len + jnp.arange(n_new, dtype=jnp.int32)))

    cache_kt = cache_k.transpose(0, 1, 3, 4, 2)
    cache_vt = cache_v.transpose(0, 1, 3, 4, 2)

    yp = x_prompt
    ys = x_sample.transpose(1, 0, 2).reshape(1, ns, d)
    outs = [[] for _ in range(6)]
    for l in range(depth):
        mod_p = mod_all[l].reshape(MOD_ROWS, 1, 6 * d)
        mod_s = jnp.tile(mod_all[l, :nb], (n_new, 1))[None]
        idx_p2 = lambda bi, i: (nb + bi, 0)
        idx_s2 = lambda bi, i: (0, i)
        w_in_bf = w_in[l].astype(BF16)
        w_out_bf = w_out[l].astype(BF16)
        conv_p = (jnp.pad(conv_w[l], ((0, 1), (0, 0))), conv_b[l][None], conv_ln_g[l][None], conv_ln_b[l][None])
        gpre, gpm, gpf, gqf = g_pre_mix[l][None], g_post_mix[l][None], g_pre_ffn[l][None], g_post_ffn[l][None]

        q, k, v, kt, vt, cv, ul = _inproj_call(yp, mod_p, 1, idx_p2, gpre, w_in_bf, tabs_p, tm_p, conv_p,
                                               tm=tm_p, keep=keep, name=f"inproj_prompt_{l}")
        att = _attn_call(q, k, v)
        outs[0].append(kt.reshape(bp, keep, N_HEADS, HEAD_DIM))
        outs[1].append(vt.reshape(bp, keep, N_HEADS, HEAD_DIM))
        outs[2].append(ul[:, CONV_HALO - (CONV_WIDTH - 1):])

        qs, _, _, kts, vts, us = _inproj_call(ys, mod_s, nb, idx_s2, gpre, w_in_bf, tabs_s, 1, None,
                                              tm=nb, keep=ns, name=f"inproj_sample_{l}")
        to_batch_major = lambda z: z.reshape(n_new, nb, N_HEADS, HEAD_DIM).transpose(1, 0, 2, 3)
        q_bm = (qs.reshape(HEAD_PAIRS, n_new, nb, 2, HEAD_DIM).transpose(2, 1, 0, 3, 4)
                .reshape(nb, n_new, N_HEADS, HEAD_DIM))
        k_bm, v_bm = to_batch_major(kts), to_batch_major(vts)
        head_major = lambda z: jnp.pad(z.transpose(0, 2, 1, 3), ((0, 0), (0, 0), (0, SAMPLE_ROWS - n_new), (0, 0)))
        att_s = _sample_attn_call(head_major(q_bm), head_major(k_bm), head_major(v_bm),
                                  cache_kt, cache_vt, l, n_new)
        att_s = (att_s[:, :, :n_new].reshape(nb, HEAD_PAIRS, 2, n_new, HEAD_DIM).transpose(1, 3, 0, 2, 4)
                 .reshape(1, HEAD_PAIRS, ns, 128).astype(BF16))
        ufull = jnp.concatenate([state_conv[l].transpose(1, 0, 2), us.reshape(n_new, nb, 512)], axis=0)
        cv_s = _sample_conv_call(ufull, conv_p).reshape(1, ns, 512)
        outs[3].append(k_bm)
        outs[4].append(v_bm)
        outs[5].append(ufull[n_new:].transpose(1, 0, 2))

        idx_p1 = lambda i, *_: (nb + i // (seq // tm_f), 0)
        idx_s1 = lambda i, *_: (0, i)
        if l % 2 == 0:
            tm_f = 1024
            wg, wu, wd = ffn_w_gate[l // 2], ffn_w_up[l // 2], ffn_w_down[l // 2]
            yp = _ffn_call(att, cv, yp, mod_p, 1, idx_p1, gpm, gpf, gqf, w_out_bf, wg, wu, wd,
                           tm=tm_f, tf=512, name=f"ffn_prompt_{l}")
            ys = _ffn_call(att_s, cv_s, ys, mod_s, ns, idx_s1, gpm, gpf, gqf, w_out_bf, wg, wu, wd,
                           tm=ns, tf=512, name=f"ffn_sample_{l}")
        else:
            tm_f = 1024
            wr = moe_w_router[l // 2].T
            wg, wu, wd = moe_w_gate[l // 2], moe_w_up[l // 2], moe_w_down[l // 2]
            x1, h2, gates, sel = _router_call(att, cv, yp, mod_p, 1, idx_p1, gpm, gpf, w_out_bf, wr,
                                              tm=tm_f, name=f"router_prompt_{l}")
            x1s, h2s, gates_s, sel_s = _router_call(att_s, cv_s, ys, mod_s, ns, idx_s1, gpm, gpf, w_out_bf, wr,
                                                    tm=ns, name=f"router_sample_{l}")
            pool = lambda a, b_: jnp.concatenate([a.reshape(bp * seq, -1), b_.reshape(ns, -1)], axis=0)
            h2_pool = jnp.concatenate([h2.reshape(2, bp * seq, d // 4), h2s.reshape(2, ns, d // 4)], axis=1)
            ya, yb, w12 = _moe_route(h2_pool, pool(sel, sel_s), pool(gates, gates_s), wg, wu, wd, tf=512)
            yp = _combine_call(ya, yb, w12, x1, 0, mod_p, 1, idx_p1, gqf, tm=tm_f, name=f"moe_combine_prompt_{l}")
            ys = _combine_call(ya, yb, w12, x1s, bp * seq, mod_s, ns, idx_s1, gqf, tm=ns,
                               name=f"moe_combine_sample_{l}")

    y_sample = ys.reshape(n_new, nb, d).transpose(1, 0, 2)
    return (yp, y_sample) + tuple(jnp.stack(o) for o in outs)
```

```python
import functools

import numpy as np
import jax
import jax.numpy as jnp
from jax import lax
from jax.experimental import pallas as pl
from jax.experimental.pallas import tpu as pltpu
from jax.experimental.pallas import tpu_sc as plsc

F32 = jnp.float32
BF16 = jnp.bfloat16

HEAD_DIM = 64
N_HEADS = 8
ATT_WIDTH = N_HEADS * HEAD_DIM
HEAD_PAIRS = ATT_WIDTH // 128
CONV_WIDTH = 31
CONV_HALO = 32
DILATION_PATTERNS = ((128, 1), (512, 4), (2048, 16))
BAND = 128
ROT_DIM = HEAD_DIM // 4
ROPE_THETA = 500000.0
ATTN_SCALE = HEAD_DIM ** -0.5
N_EXPERTS = 8
PAST_LEN = 2048
EPS = 1e-6
NEG = -1e30
MOD_ROWS = 136
VMEM_LIMIT = 56 * 1024 * 1024


def _rms(x, g):
    return x * lax.rsqrt(jnp.mean(x * x, axis=-1, keepdims=True) + EPS) * g


def _silu(x):
    return x * jax.nn.sigmoid(x)


def _params(n_axes, vmem=VMEM_LIMIT):
    return pltpu.CompilerParams(dimension_semantics=("arbitrary",) * n_axes, vmem_limit_bytes=vmem)


def _mod_kernel(c_ref, w_ref, b_ref, o_ref):
    a = _silu(c_ref[...]).astype(BF16)
    o_ref[0] = jnp.dot(a, w_ref[0].astype(BF16), preferred_element_type=F32) + b_ref[0]


def _mod_call(c_all, w_mod, b_mod):
    depth, d, d6 = w_mod.shape
    return pl.pallas_call(
        _mod_kernel,
        grid=(depth, d6 // d),
        in_specs=[pl.BlockSpec((MOD_ROWS, d), lambda l, j: (0, 0)),
                  pl.BlockSpec((1, d, d), lambda l, j: (l, 0, j)),
                  pl.BlockSpec((1, 1, d), lambda l, j: (l, 0, j))],
        out_specs=pl.BlockSpec((1, MOD_ROWS, d), lambda l, j: (l, 0, j)),
        out_shape=jax.ShapeDtypeStruct((depth, MOD_ROWS, d6), F32),
        compiler_params=_params(2),
        name="adaln_mod",
    )(c_all, w_mod, b_mod.reshape(depth, 1, d6))


def _inproj_kernel(*refs, conv, tm):
    if conv:
        (x_ref, sh_ref, sc_ref, g_ref, w_ref, cos_ref, s1_ref, s2_ref,
         cw_ref, cb_ref, lg_ref, lb_ref,
         q_ref, k_ref, v_ref, kt_ref, vt_ref, cv_ref, ul_ref, ubuf) = refs
    else:
        (x_ref, sh_ref, sc_ref, g_ref, w_ref, cos_ref, s1_ref, s2_ref,
         q_ref, k_ref, v_ref, kt_ref, vt_ref, u_ref) = refs

    h = _rms(x_ref[0], g_ref[...]) * (1.0 + sc_ref[0]) + sh_ref[0]
    proj = jnp.dot(h.astype(BF16), w_ref[...], preferred_element_type=F32)
    cos, s1, s2 = cos_ref[0], s1_ref[0], s2_ref[0]

    def rope(z):
        return z * cos + pltpu.roll(z, 128 - ROT_DIM // 2, 1) * s1 + pltpu.roll(z, ROT_DIM // 2, 1) * s2

    for hp in range(HEAD_PAIRS):
        lo, hi = hp * 128, (hp + 1) * 128
        q_ref[0, hp] = rope(proj[:, lo:hi]) * ATTN_SCALE
        kz = rope(proj[:, ATT_WIDTH + lo:ATT_WIDTH + hi])
        k_ref[0, hp] = kz
        kt_ref[0, :, lo:hi] = kz
        vz = proj[:, 2 * ATT_WIDTH + lo:2 * ATT_WIDTH + hi]
        v_ref[0, hp] = vz
        vt_ref[0, :, lo:hi] = vz

    a = proj[:, 3 * ATT_WIDTH:3 * ATT_WIDTH + 512]
    gate = proj[:, 3 * ATT_WIDTH + 512:]
    u = a * jax.nn.sigmoid(gate)
    if not conv:
        u_ref[0] = u
        return

    @pl.when(pl.program_id(1) == 0)
    def _():
        ubuf[0, 0:CONV_HALO, :] = jnp.zeros((CONV_HALO, 512), F32)

    ubuf[0, CONV_HALO:CONV_HALO + tm, :] = u
    span = tm + CONV_HALO - 8
    for s in range(1, 8):
        ubuf[s, 0:span, :] = ubuf[0, s:s + span, :]
    off0 = CONV_HALO - (CONV_WIDTH - 1)
    rows = 32

    def chunk(r, carry):
        base = pl.multiple_of(r * rows, rows)
        acc = jnp.broadcast_to(cb_ref[...], (rows, 512))
        for j in range(CONV_WIDTH):
            a, s = divmod(off0 + j, 8)
            acc = acc + cw_ref[j:j + 1, :] * ubuf[s, pl.ds(base + 8 * a, rows), :]
        mu = jnp.mean(acc, axis=-1, keepdims=True)
        xc = acc - mu
        var = jnp.mean(xc * xc, axis=-1, keepdims=True)
        y = xc * lax.rsqrt(var + EPS) * lg_ref[...] + lb_ref[...]
        cv_ref[0, pl.ds(base, rows), :] = _silu(y).astype(BF16)
        return carry

    lax.fori_loop(0, tm // rows, chunk, 0, unroll=4)
    tail = ubuf[0, tm:tm + CONV_HALO, :]
    ul_ref[0] = tail
    ubuf[0, 0:CONV_HALO, :] = tail


def _mod_spec(rows, chunk, index_fn):
    return pl.BlockSpec((1, rows, 1024), lambda *g: index_fn(*g) + (chunk,))


def _inproj_call(x, mod, mod_rows, mod_idx, g, w_bf, tabs, tab_rows, conv_p, *, tm, keep, name):
    b, t, d = x.shape
    nt = t // tm
    off = (t - keep) // tm
    conv = conv_p is not None
    tab_spec = pl.BlockSpec((1, tab_rows, 128), lambda bi, i: (i, 0, 0))
    in_specs = [pl.BlockSpec((1, tm, d), lambda bi, i: (bi, i, 0)),
                _mod_spec(mod_rows, 0, mod_idx), _mod_spec(mod_rows, 1, mod_idx),
                pl.BlockSpec((1, d), lambda bi, i: (0, 0)),
                pl.BlockSpec(w_bf.shape, lambda bi, i: (0, 0)),
                tab_spec, tab_spec, tab_spec]
    args = [x, mod, mod, g, w_bf, *tabs]
    hp_spec = pl.BlockSpec((1, HEAD_PAIRS, tm, 128), lambda bi, i: (bi, 0, i, 0))
    tail_spec = pl.BlockSpec((1, tm, 512), lambda bi, i: (bi, jnp.maximum(i - off, 0), 0))
    hp_shape = jax.ShapeDtypeStruct((b, HEAD_PAIRS, t, 128), F32)
    tail_shape = jax.ShapeDtypeStruct((b, keep, 512), F32)
    out_specs = [hp_spec, hp_spec, hp_spec, tail_spec, tail_spec]
    out_shape = [hp_shape, hp_shape, hp_shape, tail_shape, tail_shape]
    scratch = []
    if conv:
        small = pl.BlockSpec((1, 512), lambda bi, i: (0, 0))
        in_specs += [pl.BlockSpec((32, 512), lambda bi, i: (0, 0)), small, small, small]
        args += list(conv_p)
        out_specs += [pl.BlockSpec((1, tm, 512), lambda bi, i: (bi, i, 0)),
                      pl.BlockSpec((1, CONV_HALO, 512), lambda bi, i: (bi, 0, 0))]
        out_shape += [jax.ShapeDtypeStruct((b, t, 512), BF16),
                      jax.ShapeDtypeStruct((b, CONV_HALO, 512), F32)]
        scratch = [pltpu.VMEM((8, tm + CONV_HALO, 512), F32)]
    else:
        out_specs += [pl.BlockSpec((1, tm, 512), lambda bi, i: (bi, i, 0))]
        out_shape += [jax.ShapeDtypeStruct((b, t, 512), F32)]
    return pl.pallas_call(
        functools.partial(_inproj_kernel, conv=conv, tm=tm),
        grid=(b, nt), in_specs=in_specs, out_specs=out_specs, out_shape=out_shape,
        scratch_shapes=scratch, compiler_params=_params(2), name=name,
    )(*args)


def _attn_kernel(q_ref, kc_ref, kp_ref, vc_ref, vp_ref, o_ref, acc_s, m_s, l_s, *, tq):
    first_kj = jnp.where(pl.program_id(2) == 0, 128, 0)
    lane_lo = lax.broadcasted_iota(jnp.int32, (128, 128), 1) < HEAD_DIM
    qi = lax.broadcasted_iota(jnp.int32, (256, 256), 0)
    qi = jnp.where(qi >= 128, qi - 128, qi)
    kj = lax.broadcasted_iota(jnp.int32, (256, 256), 1)
    band = (kj >= qi) & (kj <= qi + BAND)
    band_first = band & (kj >= first_kj)

    ones_cols = jnp.ones((256, 128), BF16)
    nt = (((1,), (1,)), ((), ()))

    def strided(start, size, d):
        if d > 1:
            return pl.ds(start, size, stride=d)
        return pl.ds(start if isinstance(start, int) else pl.multiple_of(start, 128), size)

    def blocks(items):
        loaded = []
        for rows, first, c, d, s in items:
            if first:
                take = lambda rp, rc: jnp.concatenate(
                    [rp[0, 0, strided(tq - BAND * d + c, 128, d), :],
                     rc[0, 0, strided(c, 128, d), :]], axis=0)
            else:
                krows = strided(c + d * 128 * (s - 1), 256, d)
                take = lambda rp, rc: rc[0, 0, krows, :]
            loaded.append((q_ref[0, 0, rows, :], take(kp_ref, kc_ref), take(vp_ref, vc_ref),
                           band_first if first else band))
        scores = []
        for q2, k2, _, _ in loaded:
            qq = jnp.concatenate([jnp.where(lane_lo, q2, 0.0), jnp.where(lane_lo, 0.0, q2)], axis=0)
            scores.append(lax.dot_general(qq.astype(BF16), k2.astype(BF16), nt,
                                          preferred_element_type=F32))
        probs = []
        for sc, (_, _, _, mask) in zip(scores, loaded):
            sc = jnp.where(mask, sc, NEG)
            mb = jnp.max(sc, axis=1, keepdims=True)
            probs.append((mb, jnp.exp(sc - mb).astype(BF16)))
        out = []
        for (mb, p), (_, _, v2, _), item in zip(probs, loaded, items):
            pv = jnp.dot(p, jnp.concatenate([v2.astype(BF16), ones_cols], axis=1), preferred_element_type=F32)
            out.append((item[0], jnp.where(lane_lo, mb[:128], mb[128:]),
                        jnp.where(lane_lo, pv[:128, 128:], pv[128:, 128:]),
                        jnp.where(lane_lo, pv[:128, :128], pv[128:, :128])))
        return out

    def merge(rows, mt, lt, pvt):
        mp, lp, ap = m_s[rows, :], l_s[rows, :], acc_s[rows, :]
        mn = jnp.maximum(mp, mt)
        a = jnp.exp(mp - mn)
        b = jnp.exp(mt - mn)
        return mn, a * lp + b * lt, a * ap + b * pvt

    def store_init(results):
        for rows, mt, lt, pvt in results:
            m_s[rows, :] = mt
            l_s[rows, :] = lt
            acc_s[rows, :] = pvt

    def store_merged(results):
        for rows, mt, lt, pvt in results:
            mn, ln, an = merge(rows, mt, lt, pvt)
            m_s[rows, :] = mn
            l_s[rows, :] = ln
            acc_s[rows, :] = an

    def store_output(results):
        for rows, mt, lt, pvt in results:
            _, ln, an = merge(rows, mt, lt, pvt)
            o_ref[0, 0, rows, :] = (an / ln).astype(o_ref.dtype)

    def loop(n, fn):
        def body(i, carry):
            fn(i)
            return carry
        lax.fori_loop(0, n, body, 0)

    group = 4
    loop(16 // group, lambda g: store_init(blocks(
        [(pl.ds(g * group + j, 128, stride=16), True, g * group + j, 16, 0) for j in range(group)])))

    store_merged(blocks([(pl.ds(c, 128, stride=4), True, c, 4, 0) for c in range(4)]))
    loop(tq // 512 - 1, lambda n: store_merged(blocks(
        [(pl.ds(c + 512 * (n + 1), 128, stride=4), False, c, 4, n + 1) for c in range(4)])))

    store_output(blocks([(strided(0, 128, 1), True, 0, 1, 0)]
                        + [(strided(128 * s, 128, 1), False, 0, 1, s) for s in range(1, group)]))
    rest = 4
    assert (tq // 128 - group) % rest == 0
    loop((tq // 128 - group) // rest, lambda g: store_output(blocks(
        [(strided(128 * (group + g * rest + j), 128, 1), False, 0, 1, group + g * rest + j) for j in range(rest)])))


def _attn_call(q, k, v, *, tq=2048):
    b, hp, t, _ = q.shape
    assert tq == BAND * 16 and t % tq == 0
    cur = pl.BlockSpec((1, 1, tq, 128), lambda bi, h, i: (bi, h, i, 0))
    prev = pl.BlockSpec((1, 1, tq, 128), lambda bi, h, i: (bi, h, jnp.maximum(i - 1, 0), 0))
    return pl.pallas_call(
        functools.partial(_attn_kernel, tq=tq),
        grid=(b, hp, t // tq),
        in_specs=[cur, cur, prev, cur, prev],
        out_specs=cur,
        out_shape=jax.ShapeDtypeStruct((b, hp, t, 128), BF16),
        scratch_shapes=[pltpu.VMEM((tq, 128), F32)] * 3,
        compiler_params=_params(3), name="dilated_attn_prompt",
    )(q, k, k, v, v)


SAMPLE_ROWS = 8
SAMPLE_STEP = 2


def _sample_attn_kernel(q_ref, kn_ref, vn_ref, kt_ref, vt_ref, mh_ref, mn_ref, o_ref):
    mh, mn = mh_ref[...], mn_ref[...]
    nt = (((1,), (1,)), ((), ()))
    for b, h in [(b, h) for b in range(q_ref.shape[0]) for h in range(N_HEADS)]:
        qh = q_ref[b, h].astype(BF16)
        s = jnp.dot(qh, kt_ref[0, b, h].astype(BF16), preferred_element_type=F32)
        sn = lax.dot_general(qh, kn_ref[b, h].astype(BF16), nt, preferred_element_type=F32)
        s = jnp.where(mh > 0, s, NEG)
        sn = jnp.where(mn > 0, sn, NEG)
        m = jnp.maximum(jnp.max(s, axis=1, keepdims=True), jnp.max(sn, axis=1, keepdims=True))
        p = mh * jnp.exp(s - m)
        pn = mn * jnp.exp(sn - m)
        den = jnp.sum(p, axis=1, keepdims=True) + jnp.sum(pn, axis=1, keepdims=True)
        num = (lax.dot_general(p.astype(BF16), vt_ref[0, b, h].astype(BF16), nt, preferred_element_type=F32)
               + jnp.dot(pn.astype(BF16), vn_ref[b, h].astype(BF16), preferred_element_type=F32))
        o_ref[b, h] = num / jnp.where(den > 0, den, 1.0)


def _sample_tables(w_buf, n_new):
    rows = np.arange(w_buf + n_new)
    mult = np.zeros((SAMPLE_ROWS, w_buf + SAMPLE_ROWS), np.float32)
    for t in range(n_new):
        dist = w_buf + t - rows
        for window, dil in DILATION_PATTERNS:
            mult[t, :w_buf + n_new] += (dist >= 0) & (dist <= window) & (dist % dil == 0)
    return jnp.asarray(mult[:, :w_buf]), jnp.asarray(mult[:, w_buf:])


def _sample_attn_call(q, kn, vn, cache_kt, cache_vt, layer, n_new):
    nb = q.shape[0]
    w_buf = cache_kt.shape[-1]
    mh, mn = _sample_tables(w_buf, n_new)
    step = SAMPLE_STEP
    new_spec = pl.BlockSpec((step, N_HEADS, SAMPLE_ROWS, HEAD_DIM), lambda i: (i, 0, 0, 0))
    cache_spec = pl.BlockSpec((1, step, N_HEADS, HEAD_DIM, w_buf), lambda i: (layer, i, 0, 0, 0))
    return pl.pallas_call(
        _sample_attn_kernel,
        grid=(nb // step,),
        in_specs=[new_spec, new_spec, new_spec, cache_spec, cache_spec,
                  pl.BlockSpec(mh.shape, lambda i: (0, 0)), pl.BlockSpec(mn.shape, lambda i: (0, 0))],
        out_specs=new_spec,
        out_shape=jax.ShapeDtypeStruct((nb, N_HEADS, SAMPLE_ROWS, HEAD_DIM), F32),
        compiler_params=_params(1), name="dilated_attn_sample",
    )(q, kn, vn, cache_kt, cache_vt, mh, mn)


def _sample_conv_kernel(uf_ref, cw_ref, cb_ref, lg_ref, lb_ref, o_ref, *, n_new, nb):
    rows = 32

    def body(n, carry):
        t = n // (nb // rows)
        base = pl.multiple_of((n % (nb // rows)) * rows, rows)
        acc = jnp.broadcast_to(cb_ref[...], (rows, 512))
        for j in range(CONV_WIDTH):
            acc = acc + cw_ref[j:j + 1, :] * uf_ref[t + j, pl.ds(base, rows), :]
        mu = jnp.mean(acc, axis=-1, keepdims=True)
        xc = acc - mu
        var = jnp.mean(xc * xc, axis=-1, keepdims=True)
        y = xc * lax.rsqrt(var + EPS) * lg_ref[...] + lb_ref[...]
        o_ref[t, pl.ds(base, rows), :] = _silu(y).astype(BF16)
        return carry

    lax.fori_loop(0, n_new * (nb // rows), body, 0)


def _sample_conv_call(ufull, conv_p):
    n_full, nb, ch = ufull.shape
    n_new = n_full - (CONV_WIDTH - 1)
    return pl.pallas_call(
        functools.partial(_sample_conv_kernel, n_new=n_new, nb=nb),
        out_shape=jax.ShapeDtypeStruct((n_new, nb, ch), BF16),
        compiler_params=pltpu.CompilerParams(vmem_limit_bytes=VMEM_LIMIT),
        name="conv_sample",
    )(ufull, *conv_p)


def _mix_residual(att_ref, cv_ref, x_ref, gt1_ref, sc2_ref, sh2_ref, gpm_ref, gpf_ref, wo_ref):
    a = jnp.concatenate([att_ref[0, hp] for hp in range(HEAD_PAIRS)] + [cv_ref[0]], axis=-1)
    mix = jnp.dot(a, wo_ref[...], preferred_element_type=F32)
    x1 = x_ref[0] + gt1_ref[0] * _rms(mix, gpm_ref[...])
    h2 = _rms(x1, gpf_ref[...]) * (1.0 + sc2_ref[0]) + sh2_ref[0]
    return x1, h2


def _ffn_kernel(att_ref, cv_ref, x_ref, gt1_ref, sc2_ref, sh2_ref, gt2_ref, gpm_ref, gpf_ref, gqf_ref,
                wo_ref, wg_ref, wu_ref, wd_ref, o_ref, x1_s, h2_s, acc_s):
    f = pl.program_id(1)

    @pl.when(f == 0)
    def _():
        x1, h2 = _mix_residual(att_ref, cv_ref, x_ref, gt1_ref, sc2_ref, sh2_ref, gpm_ref, gpf_ref, wo_ref)
        x1_s[...] = x1
        h2_s[...] = h2.astype(BF16)
        acc_s[...] = jnp.zeros_like(acc_s)

    h2 = h2_s[...]
    g = jnp.dot(h2, wg_ref[...].astype(BF16), preferred_element_type=F32)
    u = jnp.dot(h2, wu_ref[...].astype(BF16), preferred_element_type=F32)
    act = (_silu(g) * u).astype(BF16)
    acc_s[...] += jnp.dot(act, wd_ref[...].astype(BF16), preferred_element_type=F32)

    @pl.when(f == pl.num_programs(1) - 1)
    def _():
        o_ref[0] = x1_s[...] + gt2_ref[0] * _rms(acc_s[...], gqf_ref[...])


def _mix_in_specs(b, t, tm, mod_rows, mod_idx, d):
    ntb = t // tm
    row = lambda i, *_: (i // ntb, i % ntb, 0)
    specs = [pl.BlockSpec((1, HEAD_PAIRS, tm, 128), lambda i, *_: (i // ntb, 0, i % ntb, 0)),
             pl.BlockSpec((1, tm, 512), row),
             pl.BlockSpec((1, tm, d), row)]
    return specs, row


def _ffn_call(att, cv, x, mod, mod_rows, mod_idx, gpm, gpf, gqf, wo_bf, wg, wu, wd, *, tm, tf, name):
    b, t, d = x.shape
    ff = wg.shape[1]
    ntb = t // tm
    specs, row = _mix_in_specs(b, t, tm, mod_rows, mod_idx, d)
    vec = pl.BlockSpec((1, d), lambda i, f: (0, 0))
    in_specs = specs + [_mod_spec(mod_rows, 2, mod_idx), _mod_spec(mod_rows, 4, mod_idx),
                        _mod_spec(mod_rows, 3, mod_idx), _mod_spec(mod_rows, 5, mod_idx),
                        vec, vec, vec,
                        pl.BlockSpec((d, d), lambda i, f: (0, 0)),
                        pl.BlockSpec((d, tf), lambda i, f: (0, f)),
                        pl.BlockSpec((d, tf), lambda i, f: (0, f)),
                        pl.BlockSpec((tf, d), lambda i, f: (f, 0))]
    return pl.pallas_call(
        _ffn_kernel,
        grid=(b * ntb, ff // tf),
        in_specs=in_specs,
        out_specs=pl.BlockSpec((1, tm, d), row),
        out_shape=jax.ShapeDtypeStruct((b, t, d), F32),
        scratch_shapes=[pltpu.VMEM((tm, d), F32), pltpu.VMEM((tm, d), BF16), pltpu.VMEM((tm, d), F32)],
        compiler_params=_params(2), name=name,
    )(att, cv, x, mod, mod, mod, mod, gpm, gpf, gqf, wo_bf, wg, wu, wd)


def _router_kernel(att_ref, cv_ref, x_ref, gt1_ref, sc2_ref, sh2_ref, gpm_ref, gpf_ref, wo_ref, wr_ref,
                   x1_ref, h2_ref, gate_ref, sel_ref):
    x1, h2 = _mix_residual(att_ref, cv_ref, x_ref, gt1_ref, sc2_ref, sh2_ref, gpm_ref, gpf_ref, wo_ref)
    x1_ref[0] = x1
    _store_split(h2_ref.at[:, 0], h2)
    logits = jnp.concatenate([jnp.sum(h2 * wr_ref[e:e + 1, :], axis=1, keepdims=True)
                              for e in range(N_EXPERTS)], axis=1)
    lane = lax.broadcasted_iota(jnp.int32, logits.shape, 1).astype(F32)
    v1 = jnp.max(logits, axis=-1, keepdims=True)
    i1 = jnp.min(jnp.where(logits == v1, lane, float(N_EXPERTS)), axis=-1, keepdims=True)
    oh1 = lane == i1
    rest = jnp.where(oh1, -jnp.inf, logits)
    v2 = jnp.max(rest, axis=-1, keepdims=True)
    i2 = jnp.min(jnp.where(rest == v2, lane, float(N_EXPERTS)), axis=-1, keepdims=True)
    oh2 = lane == i2
    e2 = jnp.exp(v2 - v1)
    den = 1.0 + e2
    gate_ref[0] = jnp.where(oh1, 1.0 / den, 0.0) + jnp.where(oh2, e2 / den, 0.0)
    sel_ref[0] = jnp.where(oh1 | oh2, 1.0, 0.0)


def _router_call(att, cv, x, mod, mod_rows, mod_idx, gpm, gpf, wo_bf, wr, *, tm, name):
    b, t, d = x.shape
    specs, row = _mix_in_specs(b, t, tm, mod_rows, mod_idx, d)
    vec = pl.BlockSpec((1, d), lambda i: (0, 0))
    in_specs = specs + [_mod_spec(mod_rows, 2, mod_idx), _mod_spec(mod_rows, 4, mod_idx),
                        _mod_spec(mod_rows, 3, mod_idx), vec, vec,
                        pl.BlockSpec((d, d), lambda i: (0, 0)),
                        pl.BlockSpec((N_EXPERTS, d), lambda i: (0, 0))]
    return pl.pallas_call(
        _router_kernel,
        grid=(b * (t // tm),),
        in_specs=in_specs,
        out_specs=[pl.BlockSpec((1, tm, d), row),
                   pl.BlockSpec((2, 1, tm, d // 4), lambda i: (0,) + row(i)),
                   pl.BlockSpec((1, tm, N_EXPERTS), row), pl.BlockSpec((1, tm, N_EXPERTS), row)],
        out_shape=[jax.ShapeDtypeStruct((b, t, d), F32), jax.ShapeDtypeStruct((2, b, t, d // 4), jnp.int32),
                   jax.ShapeDtypeStruct((b, t, N_EXPERTS), F32), jax.ShapeDtypeStruct((b, t, N_EXPERTS), F32)],
        compiler_params=_params(1), name=name,
    )(att, cv, x, mod, mod, mod, gpm, gpf, wo_bf, wr)


MOE_TILE = 1024
RANK_BLOCK = 1536
GATHER_WINDOW = 128
GATHER_ROWS = 32 * GATHER_WINDOW
MOE_CHUNK = 3
GATHER_DEPTH = 3


def _pack_pairs(x):
    w = x.shape[1] // 2
    lo = pltpu.bitcast(x[:, :w].astype(BF16).astype(F32), jnp.int32)
    hi = pltpu.bitcast(x[:, w:].astype(BF16).astype(F32), jnp.int32)
    return hi | lax.shift_right_logical(lo, 16)


def _unpack_pairs(words):
    lo = pltpu.bitcast(lax.shift_left(words, 16), F32)
    hi = pltpu.bitcast(words & -65536, F32)
    return jnp.concatenate([lo, hi], axis=1)


def _store_split(ref, x):
    words = _pack_pairs(x)
    q = words.shape[1] // 2
    ref[0] = words[:, :q]
    ref[1] = words[:, q:]


def _load_split(ref):
    return _unpack_pairs(jnp.concatenate([ref[0], ref[1]], axis=1))


def _rank_kernel(sel_ref, rank_ref, cnt_ref, carry):
    @pl.when(pl.program_id(0) == 0)
    def _():
        carry[...] = jnp.zeros_like(carry)

    sel = sel_ref[...]
    rb = sel.shape[0]
    before = (lax.broadcasted_iota(jnp.int32, (rb, rb), 1)
              < lax.broadcasted_iota(jnp.int32, (rb, rb), 0)).astype(BF16)
    rank_ref[...] = jnp.dot(before, sel.astype(BF16), preferred_element_type=F32) + carry[...]
    carry[...] += jnp.sum(sel, axis=0, keepdims=True)
    cnt_ref[...] = carry[...]


def _rank_call(sel):
    n, n_e = sel.shape
    rb = RANK_BLOCK
    assert n % rb == 0
    return pl.pallas_call(
        _rank_kernel,
        grid=(n // rb,),
        in_specs=[pl.BlockSpec((rb, n_e), lambda i: (i, 0))],
        out_specs=[pl.BlockSpec((rb, n_e), lambda i: (i, 0)), pl.BlockSpec((1, n_e), lambda i: (0, 0))],
        out_shape=[jax.ShapeDtypeStruct((n, n_e), F32), jax.ShapeDtypeStruct((1, n_e), F32)],
        scratch_shapes=[pltpu.VMEM((1, n_e), F32)],
        compiler_params=_params(1), name="moe_rank",
    )(sel)


def _source_kernel(dlo_ref, dhi_ref, init_hbm, src_ref):
    pltpu.sync_copy(init_hbm, src_ref)

    def place(i, carry):
        src_ref[dlo_ref[i]] = i
        src_ref[dhi_ref[i]] = i
        return carry

    lax.fori_loop(0, dlo_ref.shape[0], place, 0, unroll=8)


def _source_call(d_lo, d_hi, init):
    smem = pl.BlockSpec(memory_space=pltpu.SMEM)
    return pl.pallas_call(
        _source_kernel,
        in_specs=[smem, smem, pl.BlockSpec(memory_space=pl.ANY)], out_specs=smem,
        out_shape=jax.ShapeDtypeStruct(init.shape, jnp.int32),
        name="moe_source_rows",
    )(d_lo.astype(jnp.int32), d_hi.astype(jnp.int32), init)


def _sc_gather(table, idxs):
    halves, _, width = table.shape
    n_lists, n = len(idxs), idxs[0].shape[0]
    info = plsc.get_sparse_core_info()
    win, depth = GATHER_WINDOW, GATHER_DEPTH
    assert n % GATHER_ROWS == 0 and GATHER_ROWS == info.num_cores * info.num_subcores * win
    wins = n // GATHER_ROWS
    jobs = [(l, j, h) for l in range(n_lists) for j in range(wins) for h in range(halves)]
    mesh = plsc.VectorSubcoreMesh(core_axis_name="core", subcore_axis_name="subcore")
    out_type = [jax.ShapeDtypeStruct((halves, n, width), table.dtype) for _ in idxs]
    scratch = [pltpu.VMEM((n_lists, wins, win), jnp.int32), pltpu.VMEM((depth, win, width), table.dtype),
               pltpu.SemaphoreType.DMA((depth,)), pltpu.SemaphoreType.DMA((depth,))]

    @functools.partial(pl.kernel, out_type=out_type, mesh=mesh, scratch_types=scratch)
    def gather(table_hbm, idx_hbm, *refs):
        out_refs = refs[:n_lists]
        idx_v, buf, sem_in, sem_out = refs[n_lists:]
        worker = lax.axis_index("subcore") * info.num_cores + lax.axis_index("core")
        first = worker * wins
        for l in range(n_lists):
            pltpu.sync_copy(idx_hbm.at[l, worker], idx_v.at[l])
        for g in range(0, len(jobs), depth):
            group = jobs[g:g + depth]
            reads = [pltpu.async_copy(table_hbm.at[h].at[idx_v.at[l, j]], buf.at[k], sem_in.at[k])
                     for k, (l, j, h) in enumerate(group)]
            writes = []
            for k, (l, j, h) in enumerate(group):
                reads[k].wait()
                rows = pl.ds(pl.multiple_of((first + j) * win, win), win)
                writes.append(pltpu.async_copy(buf.at[k], out_refs[l].at[h, rows], sem_out.at[k]))
            for write in writes:
                write.wait()

    return gather(table, jnp.stack(idxs).reshape(n_lists, n // (wins * win), wins, win))


def _expert_kernel(te_ref, nu_ref, xs_ref, wg_ref, wu_ref, wd_ref, *refs):
    ys_ref, xb, acc = refs[-3:]
    t, f = pl.program_id(0), pl.program_id(1)
    last = pl.num_programs(1) - 1
    used = t < nu_ref[0]

    @pl.when(used & (f == 0))
    def _():
        xb[...] = _load_split(xs_ref).astype(BF16)
        acc[...] = jnp.zeros_like(acc)

    @pl.when(used)
    def _():
        x = xb[...]
        g = jnp.dot(x, wg_ref[0].astype(BF16), preferred_element_type=F32)
        u = jnp.dot(x, wu_ref[0].astype(BF16), preferred_element_type=F32)
        act = (_silu(g) * u).astype(BF16)
        acc[...] += jnp.dot(act, wd_ref[0].astype(BF16), preferred_element_type=F32)

    @pl.when(used & (f == last))
    def _():
        _store_split(ys_ref, acc[...])

    @pl.when(jnp.logical_not(used) & (f == last))
    def _():
        ys_ref[...] = jnp.zeros_like(ys_ref)


def _expert_call(tile_expert, n_used, xs, wg, wu, wd, ys, tile0, rows_total, *, tf):
    _, rows, quarter = xs.shape
    n_e, d, ff = wg.shape
    tm = MOE_TILE
    nf = ff // tf
    chunk = lambda t, f, te, nu: jnp.where(t < nu[0], f, nf - 1)
    in_specs = [pl.BlockSpec((2, tm, quarter), lambda t, f, te, nu: (0, t, 0)),
                pl.BlockSpec((1, d, tf), lambda t, f, te, nu: (te[t], 0, chunk(t, f, te, nu))),
                pl.BlockSpec((1, d, tf), lambda t, f, te, nu: (te[t], 0, chunk(t, f, te, nu))),
                pl.BlockSpec((1, tf, d), lambda t, f, te, nu: (te[t], chunk(t, f, te, nu), 0))]
    args = [tile_expert, n_used, xs, wg, wu, wd]
    aliases = {}
    if ys is not None:
        in_specs.append(pl.BlockSpec(memory_space=pl.ANY))
        args.append(ys)
        aliases = {len(args) - 1: 0}
    grid_spec = pltpu.PrefetchScalarGridSpec(
        num_scalar_prefetch=2,
        grid=(rows // tm, nf),
        in_specs=in_specs,
        out_specs=pl.BlockSpec((2, tm, quarter), lambda t, f, te, nu: (0, tile0 + t, 0)),
        scratch_shapes=[pltpu.VMEM((tm, d), BF16), pltpu.VMEM((tm, d), F32)])
    return pl.pallas_call(
        _expert_kernel, grid_spec=grid_spec,
        out_shape=jax.ShapeDtypeStruct((2, rows_total, quarter), jnp.int32),
        input_output_aliases=aliases,
        compiler_params=_params(2), name="moe_experts",
    )(*args)


def _moe_route(h2p, sel, gates, wg, wu, wd, *, tf):
    n, n_e = sel.shape
    tm = MOE_TILE
    rank, cnt = _rank_call(sel)
    counts = cnt[0].astype(jnp.int32)
    padded = (counts + tm - 1) // tm * tm
    seg_end = jnp.cumsum(padded)
    seg_start = seg_end - padded
    rows_max = -(-(2 * n + n_e * tm) // GATHER_ROWS) * GATHER_ROWS
    assert rows_max % tm == 0
    lanes = jnp.arange(n_e, dtype=jnp.int32)[None, :]
    e_lo = jnp.min(jnp.where(sel > 0, lanes, n_e - 1), axis=1)
    e_hi = jnp.max(jnp.where(sel > 0, lanes, 0), axis=1)
    pick = lambda a, e: jnp.take_along_axis(a, e[:, None], axis=1)[:, 0]
    rank_i = rank.astype(jnp.int32)
    d_lo = seg_start[e_lo] + pick(rank_i, e_lo)
    d_hi = seg_start[e_hi] + pick(rank_i, e_hi)
    w = jnp.stack([pick(gates, e_lo), pick(gates, e_hi)], axis=1)
    src = _source_call(d_lo, d_hi, jnp.arange(rows_max, dtype=jnp.int32) % n)
    n_tiles = rows_max // tm
    n_used = (seg_end[-1] // tm).astype(jnp.int32)
    tile_ids = jnp.minimum(jnp.arange(n_tiles, dtype=jnp.int32), n_used - 1)
    tile_expert = jnp.minimum(jnp.sum(seg_end[None, :] <= (tile_ids * tm)[:, None], axis=1), n_e - 1).astype(jnp.int32)

    step = GATHER_ROWS // tm * MOE_CHUNK
    ys = None
    for t0 in range(0, n_tiles, step):
        t1 = min(t0 + step, n_tiles)
        xs, = _sc_gather(h2p, [src[t0 * tm:t1 * tm]])
        ys = _expert_call(tile_expert[t0:t1], jnp.clip(n_used - t0, 0, t1 - t0).reshape(1), xs, wg, wu, wd,
                          ys, t0, rows_max, tf=tf)
    n_pad = -(-n // GATHER_ROWS) * GATHER_ROWS
    pad_idx = lambda dd: jnp.concatenate([dd.astype(jnp.int32), jnp.arange(n_pad - n, dtype=jnp.int32)])
    ya, yb = _sc_gather(ys, [pad_idx(d_lo), pad_idx(d_hi)])
    return ya, yb, w


def _combine_kernel(ya_ref, yb_ref, w_ref, x1_ref, gt2_ref, gqf_ref, o_ref):
    w = w_ref[...]
    f = w[:, 0:1] * _load_split(ya_ref) + w[:, 1:2] * _load_split(yb_ref)
    o_ref[0] = x1_ref[0] + gt2_ref[0] * _rms(f, gqf_ref[...])


def _combine_call(ya, yb, w, x1, row0, mod, mod_rows, mod_idx, gqf, *, tm, name):
    b, t, d = x1.shape
    ntb = t // tm
    blk0 = row0 // tm
    assert row0 % tm == 0
    pool = lambda i: (blk0 + i, 0)
    pool3 = lambda i: (0, blk0 + i, 0)
    row = lambda i: (i // ntb, i % ntb, 0)
    return pl.pallas_call(
        _combine_kernel,
        grid=(b * ntb,),
        in_specs=[pl.BlockSpec((2, tm, d // 4), pool3), pl.BlockSpec((2, tm, d // 4), pool3),
                  pl.BlockSpec((tm, 2), pool), pl.BlockSpec((1, tm, d), row),
                  _mod_spec(mod_rows, 5, mod_idx), pl.BlockSpec((1, d), lambda i: (0, 0))],
        out_specs=pl.BlockSpec((1, tm, d), row),
        out_shape=jax.ShapeDtypeStruct((b, t, d), F32),
        compiler_params=_params(1), name=name,
    )(ya, yb, w, x1, mod, gqf)


def _rope_tables(pos):
    half = ROT_DIM // 2
    inv_freq = ROPE_THETA ** (-jnp.arange(half, dtype=F32) * 2.0 / ROT_DIM)
    ang = pos.astype(F32)[:, None] * inv_freq[None, :]
    cos, sin = jnp.cos(ang), jnp.sin(ang)
    l64 = np.arange(128) % HEAD_DIM
    idx = l64 % half
    first = (l64 < half)[None, :]
    second = ((l64 >= half) & (l64 < ROT_DIM))[None, :]
    cos_t = jnp.where(first | second, cos[:, idx], 1.0)
    s1_t = jnp.where(first, -sin[:, idx], 0.0)
    s2_t = jnp.where(second, sin[:, idx], 0.0)
    return cos_t, s1_t, s2_t


def kernel(x_prompt, x_sample, cache_k, cache_v, state_conv, c_prompt, c_sample, w_mod, b_mod, g_pre_mix, g_post_mix, g_pre_ffn, g_post_ffn, w_in, conv_w, conv_b, conv_ln_g, conv_ln_b, w_out, ffn_w_gate, ffn_w_up, ffn_w_down, moe_w_router, moe_w_gate, moe_w_up, moe_w_down):
    bp, seq, d = x_prompt.shape
    nb, n_new, _ = x_sample.shape
    depth = w_mod.shape[0]
    w_buf = cache_k.shape[2]
    past_len = PAST_LEN
    assert w_buf == min(DILATION_PATTERNS[-1][0], past_len)
    keep = min(DILATION_PATTERNS[-1][0], seq)
    ns = nb * n_new

    c_all = jnp.concatenate([c_sample, c_prompt, jnp.zeros((MOD_ROWS - nb - bp, d), F32)], axis=0)
    mod_all = _mod_call(c_all, w_mod, b_mod)

    tm_p = 512
    tabs_p = tuple(tb.reshape(seq // tm_p, tm_p, 128) for tb in _rope_tables(jnp.arange(seq, dtype=jnp.int32)))
    tabs_s = tuple(tb.reshape(n_new, 1, 128)
                   for tb in _rope_tables(past_len + jnp.arange(n_new, dtype=jnp.int32)))

    cache_kt = cache_k.transpose(0, 1, 3, 4, 2)
    cache_vt = cache_v.transpose(0, 1, 3, 4, 2)

    yp = x_prompt
    ys = x_sample.transpose(1, 0, 2).reshape(1, ns, d)
    outs = [[] for _ in range(6)]
    for l in range(depth):
        mod_p = mod_all[l].reshape(MOD_ROWS, 1, 6 * d)
        mod_s = jnp.tile(mod_all[l, :nb], (n_new, 1))[None]
        idx_p2 = lambda bi, i: (nb + bi, 0)
        idx_s2 = lambda bi, i: (0, i)
        w_in_bf = w_in[l].astype(BF16)
        w_out_bf = w_out[l].astype(BF16)
        conv_p = (jnp.pad(conv_w[l], ((0, 1), (0, 0))), conv_b[l][None], conv_ln_g[l][None], conv_ln_b[l][None])
        gpre, gpm, gpf, gqf = g_pre_mix[l][None], g_post_mix[l][None], g_pre_ffn[l][None], g_post_ffn[l][None]

        q, k, v, kt, vt, cv, ul = _inproj_call(yp, mod_p, 1, idx_p2, gpre, w_in_bf, tabs_p, tm_p, conv_p,
                                               tm=tm_p, keep=keep, name=f"inproj_prompt_{l}")
        att = _attn_call(q, k, v)
        outs[0].append(kt.reshape(bp, keep, N_HEADS, HEAD_DIM))
        outs[1].append(vt.reshape(bp, keep, N_HEADS, HEAD_DIM))
        outs[2].append(ul[:, CONV_HALO - (CONV_WIDTH - 1):])

        qs, _, _, kts, vts, us = _inproj_call(ys, mod_s, nb, idx_s2, gpre, w_in_bf, tabs_s, 1, None,
                                              tm=nb, keep=ns, name=f"inproj_sample_{l}")
        to_batch_major = lambda z: z.reshape(n_new, nb, N_HEADS, HEAD_DIM).transpose(1, 0, 2, 3)
        q_bm = (qs.reshape(HEAD_PAIRS, n_new, nb, 2, HEAD_DIM).transpose(2, 1, 0, 3, 4)
                .reshape(nb, n_new, N_HEADS, HEAD_DIM))
        k_bm, v_bm = to_batch_major(kts), to_batch_major(vts)
        head_major = lambda z: jnp.pad(z.transpose(0, 2, 1, 3), ((0, 0), (0, 0), (0, SAMPLE_ROWS - n_new), (0, 0)))
        att_s = _sample_attn_call(head_major(q_bm), head_major(k_bm), head_major(v_bm),
                                  cache_kt, cache_vt, l, n_new)
        att_s = (att_s[:, :, :n_new].reshape(nb, HEAD_PAIRS, 2, n_new, HEAD_DIM).transpose(1, 3, 0, 2, 4)
                 .reshape(1, HEAD_PAIRS, ns, 128).astype(BF16))
        ufull = jnp.concatenate([state_conv[l].transpose(1, 0, 2), us.reshape(n_new, nb, 512)], axis=0)
        cv_s = _sample_conv_call(ufull, conv_p).reshape(1, ns, 512)
        outs[3].append(k_bm)
        outs[4].append(v_bm)
        outs[5].append(ufull[n_new:].transpose(1, 0, 2))

        idx_p1 = lambda i, *_: (nb + i // (seq // tm_f), 0)
        idx_s1 = lambda i, *_: (0, i)
        if l % 2 == 0:
            tm_f = 1024
            wg, wu, wd = ffn_w_gate[l // 2], ffn_w_up[l // 2], ffn_w_down[l // 2]
            yp = _ffn_call(att, cv, yp, mod_p, 1, idx_p1, gpm, gpf, gqf, w_out_bf, wg, wu, wd,
                           tm=tm_f, tf=512, name=f"ffn_prompt_{l}")
            ys = _ffn_call(att_s, cv_s, ys, mod_s, ns, idx_s1, gpm, gpf, gqf, w_out_bf, wg, wu, wd,
                           tm=ns, tf=512, name=f"ffn_sample_{l}")
        else:
            tm_f = 1024
            wr = moe_w_router[l // 2].T
            wg, wu, wd = moe_w_gate[l // 2], moe_w_up[l // 2], moe_w_down[l // 2]
            x1, h2, gates, sel = _router_call(att, cv, yp, mod_p, 1, idx_p1, gpm, gpf, w_out_bf, wr,
                                              tm=tm_f, name=f"router_prompt_{l}")
            x1s, h2s, gates_s, sel_s = _router_call(att_s, cv_s, ys, mod_s, ns, idx_s1, gpm, gpf, w_out_bf, wr,
                                                    tm=ns, name=f"router_sample_{l}")
            pool = lambda a, b_: jnp.concatenate([a.reshape(bp * seq, -1), b_.reshape(ns, -1)], axis=0)
            h2_pool = jnp.concatenate([h2.reshape(2, bp * seq, d // 4), h2s.reshape(2, ns, d // 4)], axis=1)
            ya, yb, w12 = _moe_route(h2_pool, pool(sel, sel_s), pool(gates, gates_s), wg, wu, wd, tf=512)
            yp = _combine_call(ya, yb, w12, x1, 0, mod_p, 1, idx_p1, gqf, tm=tm_f, name=f"moe_combine_prompt_{l}")
            ys = _combine_call(ya, yb, w12, x1s, bp * seq, mod_s, ns, idx_s1, gqf, tm=ns,
                               name=f"moe_combine_sample_{l}")

    y_sample = ys.reshape(n_new, nb, d).transpose(1, 0, 2)
    return (yp, y_sample) + tuple(jnp.stack(o) for o in outs)
```

```python
import functools

import numpy as np
import jax
import jax.numpy as jnp
from jax import lax
from jax.experimental import pallas as pl
from jax.experimental.pallas import tpu as pltpu
from jax.experimental.pallas import tpu_sc as plsc

F32 = jnp.float32
BF16 = jnp.bfloat16

HEAD_DIM = 64
N_HEADS = 8
ATT_WIDTH = N_HEADS * HEAD_DIM
HEAD_PAIRS = ATT_WIDTH // 128
CONV_WIDTH = 31
CONV_HALO = 32
DILATION_PATTERNS = ((128, 1), (512, 4), (2048, 16))
BAND = 128
ROT_DIM = HEAD_DIM // 4
ROPE_THETA = 500000.0
ATTN_SCALE = HEAD_DIM ** -0.5
N_EXPERTS = 8
PAST_LEN = 2048
EPS = 1e-6
NEG = -1e30
MOD_ROWS = 136
VMEM_LIMIT = 56 * 1024 * 1024


def _rms(x, g):
    return x * lax.rsqrt(jnp.mean(x * x, axis=-1, keepdims=True) + EPS) * g


def _silu(x):
    return x * jax.nn.sigmoid(x)


def _params(n_axes, vmem=VMEM_LIMIT):
    return pltpu.CompilerParams(dimension_semantics=("arbitrary",) * n_axes, vmem_limit_bytes=vmem)


def _mod_kernel(c_ref, w_ref, b_ref, o_ref):
    a = _silu(c_ref[...]).astype(BF16)
    o_ref[0] = jnp.dot(a, w_ref[0].astype(BF16), preferred_element_type=F32) + b_ref[0]


def _mod_call(c_all, w_mod, b_mod):
    depth, d, d6 = w_mod.shape
    return pl.pallas_call(
        _mod_kernel,
        grid=(depth, d6 // d),
        in_specs=[pl.BlockSpec((MOD_ROWS, d), lambda l, j: (0, 0)),
                  pl.BlockSpec((1, d, d), lambda l, j: (l, 0, j)),
                  pl.BlockSpec((1, 1, d), lambda l, j: (l, 0, j))],
        out_specs=pl.BlockSpec((1, MOD_ROWS, d), lambda l, j: (l, 0, j)),
        out_shape=jax.ShapeDtypeStruct((depth, MOD_ROWS, d6), F32),
        compiler_params=_params(2),
        name="adaln_mod",
    )(c_all, w_mod, b_mod.reshape(depth, 1, d6))


def _inproj_kernel(*refs, conv, tm):
    if conv:
        (x_ref, sh_ref, sc_ref, g_ref, w_ref, cos_ref, s1_ref, s2_ref,
         cw_ref, cb_ref, lg_ref, lb_ref,
         q_ref, k_ref, v_ref, kt_ref, vt_ref, cv_ref, ul_ref, ubuf) = refs
    else:
        (x_ref, sh_ref, sc_ref, g_ref, w_ref, cos_ref, s1_ref, s2_ref,
         q_ref, k_ref, v_ref, kt_ref, vt_ref, u_ref) = refs

    h = _rms(x_ref[0], g_ref[...]) * (1.0 + sc_ref[0]) + sh_ref[0]
    proj = jnp.dot(h.astype(BF16), w_ref[...], preferred_element_type=F32)
    cos, s1, s2 = cos_ref[0], s1_ref[0], s2_ref[0]

    def rope(z):
        return z * cos + pltpu.roll(z, 128 - ROT_DIM // 2, 1) * s1 + pltpu.roll(z, ROT_DIM // 2, 1) * s2

    for hp in range(HEAD_PAIRS):
        lo, hi = hp * 128, (hp + 1) * 128
        q_ref[0, hp] = rope(proj[:, lo:hi]) * ATTN_SCALE
        kz = rope(proj[:, ATT_WIDTH + lo:ATT_WIDTH + hi])
        k_ref[0, hp] = kz
        kt_ref[0, :, lo:hi] = kz
        vz = proj[:, 2 * ATT_WIDTH + lo:2 * ATT_WIDTH + hi]
        v_ref[0, hp] = vz
        vt_ref[0, :, lo:hi] = vz

    a = proj[:, 3 * ATT_WIDTH:3 * ATT_WIDTH + 512]
    gate = proj[:, 3 * ATT_WIDTH + 512:]
    u = a * jax.nn.sigmoid(gate)
    if not conv:
        u_ref[0] = u
        return

    @pl.when(pl.program_id(1) == 0)
    def _():
        ubuf[0, 0:CONV_HALO, :] = jnp.zeros((CONV_HALO, 512), F32)

    ubuf[0, CONV_HALO:CONV_HALO + tm, :] = u
    span = tm + CONV_HALO - 8
    for s in range(1, 8):
        ubuf[s, 0:span, :] = ubuf[0, s:s + span, :]
    off0 = CONV_HALO - (CONV_WIDTH - 1)
    rows = 32

    def chunk(r, carry):
        base = pl.multiple_of(r * rows, rows)
        acc = jnp.broadcast_to(cb_ref[...], (rows, 512))
        for j in range(CONV_WIDTH):
            a, s = divmod(off0 + j, 8)
            acc = acc + cw_ref[j:j + 1, :] * ubuf[s, pl.ds(base + 8 * a, rows), :]
        mu = jnp.mean(acc, axis=-1, keepdims=True)
        xc = acc - mu
        var = jnp.mean(xc * xc, axis=-1, keepdims=True)
        y = xc * lax.rsqrt(var + EPS) * lg_ref[...] + lb_ref[...]
        cv_ref[0, pl.ds(base, rows), :] = _silu(y).astype(BF16)
        return carry

    lax.fori_loop(0, tm // rows, chunk, 0, unroll=4)
    tail = ubuf[0, tm:tm + CONV_HALO, :]
    ul_ref[0] = tail
    ubuf[0, 0:CONV_HALO, :] = tail


def _mod_spec(rows, chunk, index_fn):
    return pl.BlockSpec((1, rows, 1024), lambda *g: index_fn(*g) + (chunk,))


def _inproj_call(x, mod, mod_rows, mod_idx, g, w_bf, tabs, tab_rows, conv_p, *, tm, keep, name):
    b, t, d = x.shape
    nt = t // tm
    off = (t - keep) // tm
    conv = conv_p is not None
    tab_spec = pl.BlockSpec((1, tab_rows, 128), lambda bi, i: (i, 0, 0))
    in_specs = [pl.BlockSpec((1, tm, d), lambda bi, i: (bi, i, 0)),
                _mod_spec(mod_rows, 0, mod_idx), _mod_spec(mod_rows, 1, mod_idx),
                pl.BlockSpec((1, d), lambda bi, i: (0, 0)),
                pl.BlockSpec(w_bf.shape, lambda bi, i: (0, 0)),
                tab_spec, tab_spec, tab_spec]
    args = [x, mod, mod, g, w_bf, *tabs]
    hp_spec = pl.BlockSpec((1, HEAD_PAIRS, tm, 128), lambda bi, i: (bi, 0, i, 0))
    tail_spec = pl.BlockSpec((1, tm, 512), lambda bi, i: (bi, jnp.maximum(i - off, 0), 0))
    hp_shape = jax.ShapeDtypeStruct((b, HEAD_PAIRS, t, 128), F32)
    tail_shape = jax.ShapeDtypeStruct((b, keep, 512), F32)
    out_specs = [hp_spec, hp_spec, hp_spec, tail_spec, tail_spec]
    out_shape = [hp_shape, hp_shape, hp_shape, tail_shape, tail_shape]
    scratch = []
    if conv:
        small = pl.BlockSpec((1, 512), lambda bi, i: (0, 0))
        in_specs += [pl.BlockSpec((32, 512), lambda bi, i: (0, 0)), small, small, small]
        args += list(conv_p)
        out_specs += [pl.BlockSpec((1, tm, 512), lambda bi, i: (bi, i, 0)),
                      pl.BlockSpec((1, CONV_HALO, 512), lambda bi, i: (bi, 0, 0))]
        out_shape += [jax.ShapeDtypeStruct((b, t, 512), BF16),
                      jax.ShapeDtypeStruct((b, CONV_HALO, 512), F32)]
        scratch = [pltpu.VMEM((8, tm + CONV_HALO, 512), F32)]
    else:
        out_specs += [pl.BlockSpec((1, tm, 512), lambda bi, i: (bi, i, 0))]
        out_shape += [jax.ShapeDtypeStruct((b, t, 512), F32)]
    return pl.pallas_call(
        functools.partial(_inproj_kernel, conv=conv, tm=tm),
        grid=(b, nt), in_specs=in_specs, out_specs=out_specs, out_shape=out_shape,
        scratch_shapes=scratch, compiler_params=_params(2), name=name,
    )(*args)


def _attn_kernel(q_ref, kc_ref, kp_ref, vc_ref, vp_ref, o_ref, acc_s, m_s, l_s, *, tq):
    first_kj = jnp.where(pl.program_id(2) == 0, 128, 0)
    lane_lo = lax.broadcasted_iota(jnp.int32, (128, 128), 1) < HEAD_DIM
    qi = lax.broadcasted_iota(jnp.int32, (256, 256), 0)
    qi = jnp.where(qi >= 128, qi - 128, qi)
    kj = lax.broadcasted_iota(jnp.int32, (256, 256), 1)
    band = (kj >= qi) & (kj <= qi + BAND)
    band_first = band & (kj >= first_kj)

    ones_cols = jnp.ones((256, 128), BF16)
    nt = (((1,), (1,)), ((), ()))

    def strided(start, size, d):
        if d > 1:
            return pl.ds(start, size, stride=d)
        return pl.ds(start if isinstance(start, int) else pl.multiple_of(start, 128), size)

    def blocks(items):
        loaded = []
        for rows, first, c, d, s in items:
            if first:
                take = lambda rp, rc: jnp.concatenate(
                    [rp[0, 0, strided(tq - BAND * d + c, 128, d), :],
                     rc[0, 0, strided(c, 128, d), :]], axis=0)
            else:
                krows = strided(c + d * 128 * (s - 1), 256, d)
                take = lambda rp, rc: rc[0, 0, krows, :]
            loaded.append((q_ref[0, 0, rows, :], take(kp_ref, kc_ref), take(vp_ref, vc_ref),
                           band_first if first else band))
        scores = []
        for q2, k2, _, _ in loaded:
            kb = k2.astype(BF16)
            scores.append([lax.dot_general(qh.astype(BF16), kb, nt, preferred_element_type=F32)
                           for qh in (jnp.where(lane_lo, q2, 0.0), jnp.where(lane_lo, 0.0, q2))])
        probs = []
        for pair, (_, _, _, mask) in zip(scores, loaded):
            halves = []
            for sc in pair:
                sc = jnp.where(mask[:128], sc, NEG)
                mb = jnp.max(sc, axis=1, keepdims=True)
                halves.append((mb, jnp.exp(sc - mb).astype(BF16)))
            probs.append(halves)
        out = []
        for ((ma, pa), (mb, pb)), (_, _, v2, _), item in zip(probs, loaded, items):
            vv = jnp.concatenate([v2.astype(BF16), ones_cols], axis=1)
            pva = jnp.dot(pa, vv, preferred_element_type=F32)
            pvb = jnp.dot(pb, vv, preferred_element_type=F32)
            out.append((item[0], jnp.where(lane_lo, ma, mb),
                        jnp.where(lane_lo, pva[:, 128:], pvb[:, 128:]),
                        jnp.where(lane_lo, pva[:, :128], pvb[:, :128])))
        return out

    def merge(rows, mt, lt, pvt):
        mp, lp, ap = m_s[rows, :], l_s[rows, :], acc_s[rows, :]
        mn = jnp.maximum(mp, mt)
        a = jnp.exp(mp - mn)
        b = jnp.exp(mt - mn)
        return mn, a * lp + b * lt, a * ap + b * pvt

    def store_init(results):
        for rows, mt, lt, pvt in results:
            m_s[rows, :] = mt
            l_s[rows, :] = lt
            acc_s[rows, :] = pvt

    def store_merged(results):
        for rows, mt, lt, pvt in results:
            mn, ln, an = merge(rows, mt, lt, pvt)
            m_s[rows, :] = mn
            l_s[rows, :] = ln
            acc_s[rows, :] = an

    def store_output(results):
        for rows, mt, lt, pvt in results:
            _, ln, an = merge(rows, mt, lt, pvt)
            o_ref[0, 0, rows, :] = (an / ln).astype(o_ref.dtype)

    def loop(n, fn):
        def body(i, carry):
            fn(i)
            return carry
        lax.fori_loop(0, n, body, 0)

    group = 4
    loop(16 // group, lambda g: store_init(blocks(
        [(pl.ds(g * group + j, 128, stride=16), True, g * group + j, 16, 0) for j in range(group)])))

    store_merged(blocks([(pl.ds(c, 128, stride=4), True, c, 4, 0) for c in range(4)]))
    loop(tq // 512 - 1, lambda n: store_merged(blocks(
        [(pl.ds(c + 512 * (n + 1), 128, stride=4), False, c, 4, n + 1) for c in range(4)])))

    store_output(blocks([(strided(0, 128, 1), True, 0, 1, 0)]
                        + [(strided(128 * s, 128, 1), False, 0, 1, s) for s in range(1, group)]))
    rest = 4
    assert (tq // 128 - group) % rest == 0
    loop((tq // 128 - group) // rest, lambda g: store_output(blocks(
        [(strided(128 * (group + g * rest + j), 128, 1), False, 0, 1, group + g * rest + j) for j in range(rest)])))


def _attn_call(q, k, v, *, tq=2048):
    b, hp, t, _ = q.shape
    assert tq == BAND * 16 and t % tq == 0
    cur = pl.BlockSpec((1, 1, tq, 128), lambda bi, h, i: (bi, h, i, 0))
    prev = pl.BlockSpec((1, 1, tq, 128), lambda bi, h, i: (bi, h, jnp.maximum(i - 1, 0), 0))
    return pl.pallas_call(
        functools.partial(_attn_kernel, tq=tq),
        grid=(b, hp, t // tq),
        in_specs=[cur, cur, prev, cur, prev],
        out_specs=cur,
        out_shape=jax.ShapeDtypeStruct((b, hp, t, 128), BF16),
        scratch_shapes=[pltpu.VMEM((tq, 128), F32)] * 3,
        compiler_params=_params(3), name="dilated_attn_prompt",
    )(q, k, k, v, v)


SAMPLE_ROWS = 8
SAMPLE_STEP = 2


def _sample_attn_kernel(q_ref, kn_ref, vn_ref, kt_ref, vt_ref, mh_ref, mn_ref, o_ref):
    mh, mn = mh_ref[...], mn_ref[...]
    nt = (((1,), (1,)), ((), ()))
    for b, h in [(b, h) for b in range(q_ref.shape[0]) for h in range(N_HEADS)]:
        qh = q_ref[b, h].astype(BF16)
        s = jnp.dot(qh, kt_ref[0, b, h].astype(BF16), preferred_element_type=F32)
        sn = lax.dot_general(qh, kn_ref[b, h].astype(BF16), nt, preferred_element_type=F32)
        s = jnp.where(mh > 0, s, NEG)
        sn = jnp.where(mn > 0, sn, NEG)
        m = jnp.maximum(jnp.max(s, axis=1, keepdims=True), jnp.max(sn, axis=1, keepdims=True))
        p = mh * jnp.exp(s - m)
        pn = mn * jnp.exp(sn - m)
        den = jnp.sum(p, axis=1, keepdims=True) + jnp.sum(pn, axis=1, keepdims=True)
        num = (lax.dot_general(p.astype(BF16), vt_ref[0, b, h].astype(BF16), nt, preferred_element_type=F32)
               + jnp.dot(pn.astype(BF16), vn_ref[b, h].astype(BF16), preferred_element_type=F32))
        o_ref[b, h] = num / jnp.where(den > 0, den, 1.0)


def _sample_tables(w_buf, n_new):
    rows = np.arange(w_buf + n_new)
    mult = np.zeros((SAMPLE_ROWS, w_buf + SAMPLE_ROWS), np.float32)
    for t in range(n_new):
        dist = w_buf + t - rows
        for window, dil in DILATION_PATTERNS:
            mult[t, :w_buf + n_new] += (dist >= 0) & (dist <= window) & (dist % dil == 0)
    return jnp.asarray(mult[:, :w_buf]), jnp.asarray(mult[:, w_buf:])


def _sample_attn_call(q, kn, vn, cache_kt, cache_vt, layer, n_new):
    nb = q.shape[0]
    w_buf = cache_kt.shape[-1]
    mh, mn = _sample_tables(w_buf, n_new)
    step = SAMPLE_STEP
    new_spec = pl.BlockSpec((step, N_HEADS, SAMPLE_ROWS, HEAD_DIM), lambda i: (i, 0, 0, 0))
    cache_spec = pl.BlockSpec((1, step, N_HEADS, HEAD_DIM, w_buf), lambda i: (layer, i, 0, 0, 0))
    return pl.pallas_call(
        _sample_attn_kernel,
        grid=(nb // step,),
        in_specs=[new_spec, new_spec, new_spec, cache_spec, cache_spec,
                  pl.BlockSpec(mh.shape, lambda i: (0, 0)), pl.BlockSpec(mn.shape, lambda i: (0, 0))],
        out_specs=new_spec,
        out_shape=jax.ShapeDtypeStruct((nb, N_HEADS, SAMPLE_ROWS, HEAD_DIM), F32),
        compiler_params=_params(1), name="dilated_attn_sample",
    )(q, kn, vn, cache_kt, cache_vt, mh, mn)


def _sample_conv_kernel(uf_ref, cw_ref, cb_ref, lg_ref, lb_ref, o_ref, *, n_new, nb):
    rows = 32

    def body(n, carry):
        t = n // (nb // rows)
        base = pl.multiple_of((n % (nb // rows)) * rows, rows)
        acc = jnp.broadcast_to(cb_ref[...], (rows, 512))
        for j in range(CONV_WIDTH):
            acc = acc + cw_ref[j:j + 1, :] * uf_ref[t + j, pl.ds(base, rows), :]
        mu = jnp.mean(acc, axis=-1, keepdims=True)
        xc = acc - mu
        var = jnp.mean(xc * xc, axis=-1, keepdims=True)
        y = xc * lax.rsqrt(var + EPS) * lg_ref[...] + lb_ref[...]
        o_ref[t, pl.ds(base, rows), :] = _silu(y).astype(BF16)
        return carry

    lax.fori_loop(0, n_new * (nb // rows), body, 0)


def _sample_conv_call(ufull, conv_p):
    n_full, nb, ch = ufull.shape
    n_new = n_full - (CONV_WIDTH - 1)
    return pl.pallas_call(
        functools.partial(_sample_conv_kernel, n_new=n_new, nb=nb),
        out_shape=jax.ShapeDtypeStruct((n_new, nb, ch), BF16),
        compiler_params=pltpu.CompilerParams(vmem_limit_bytes=VMEM_LIMIT),
        name="conv_sample",
    )(ufull, *conv_p)


def _mix_residual(att_ref, cv_ref, x_ref, gt1_ref, sc2_ref, sh2_ref, gpm_ref, gpf_ref, wo_ref):
    a = jnp.concatenate([att_ref[0, hp] for hp in range(HEAD_PAIRS)] + [cv_ref[0]], axis=-1)
    mix = jnp.dot(a, wo_ref[...], preferred_element_type=F32)
    x1 = x_ref[0] + gt1_ref[0] * _rms(mix, gpm_ref[...])
    h2 = _rms(x1, gpf_ref[...]) * (1.0 + sc2_ref[0]) + sh2_ref[0]
    return x1, h2


def _ffn_kernel(att_ref, cv_ref, x_ref, gt1_ref, sc2_ref, sh2_ref, gt2_ref, gpm_ref, gpf_ref, gqf_ref,
                wo_ref, wg_ref, wu_ref, wd_ref, o_ref, x1_s, h2_s, acc_s):
    f = pl.program_id(1)

    @pl.when(f == 0)
    def _():
        x1, h2 = _mix_residual(att_ref, cv_ref, x_ref, gt1_ref, sc2_ref, sh2_ref, gpm_ref, gpf_ref, wo_ref)
        x1_s[...] = x1
        h2_s[...] = h2.astype(BF16)
        acc_s[...] = jnp.zeros_like(acc_s)

    h2 = h2_s[...]
    g = jnp.dot(h2, wg_ref[...].astype(BF16), preferred_element_type=F32)
    u = jnp.dot(h2, wu_ref[...].astype(BF16), preferred_element_type=F32)
    act = (_silu(g) * u).astype(BF16)
    acc_s[...] += jnp.dot(act, wd_ref[...].astype(BF16), preferred_element_type=F32)

    @pl.when(f == pl.num_programs(1) - 1)
    def _():
        o_ref[0] = x1_s[...] + gt2_ref[0] * _rms(acc_s[...], gqf_ref[...])


def _mix_in_specs(b, t, tm, mod_rows, mod_idx, d):
    ntb = t // tm
    row = lambda i, *_: (i // ntb, i % ntb, 0)
    specs = [pl.BlockSpec((1, HEAD_PAIRS, tm, 128), lambda i, *_: (i // ntb, 0, i % ntb, 0)),
             pl.BlockSpec((1, tm, 512), row),
             pl.BlockSpec((1, tm, d), row)]
    return specs, row


def _ffn_call(att, cv, x, mod, mod_rows, mod_idx, gpm, gpf, gqf, wo_bf, wg, wu, wd, *, tm, tf, name):
    b, t, d = x.shape
    ff = wg.shape[1]
    ntb = t // tm
    specs, row = _mix_in_specs(b, t, tm, mod_rows, mod_idx, d)
    vec = pl.BlockSpec((1, d), lambda i, f: (0, 0))
    in_specs = specs + [_mod_spec(mod_rows, 2, mod_idx), _mod_spec(mod_rows, 4, mod_idx),
                        _mod_spec(mod_rows, 3, mod_idx), _mod_spec(mod_rows, 5, mod_idx),
                        vec, vec, vec,
                        pl.BlockSpec((d, d), lambda i, f: (0, 0)),
                        pl.BlockSpec((d, tf), lambda i, f: (0, f)),
                        pl.BlockSpec((d, tf), lambda i, f: (0, f)),
                        pl.BlockSpec((tf, d), lambda i, f: (f, 0))]
    return pl.pallas_call(
        _ffn_kernel,
        grid=(b * ntb, ff // tf),
        in_specs=in_specs,
        out_specs=pl.BlockSpec((1, tm, d), row),
        out_shape=jax.ShapeDtypeStruct((b, t, d), F32),
        scratch_shapes=[pltpu.VMEM((tm, d), F32), pltpu.VMEM((tm, d), BF16), pltpu.VMEM((tm, d), F32)],
        compiler_params=_params(2), name=name,
    )(att, cv, x, mod, mod, mod, mod, gpm, gpf, gqf, wo_bf, wg, wu, wd)


def _router_kernel(att_ref, cv_ref, x_ref, gt1_ref, sc2_ref, sh2_ref, gpm_ref, gpf_ref, wo_ref, wr_ref,
                   x1_ref, h2_ref, gate_ref, sel_ref):
    x1, h2 = _mix_residual(att_ref, cv_ref, x_ref, gt1_ref, sc2_ref, sh2_ref, gpm_ref, gpf_ref, wo_ref)
    x1_ref[0] = x1
    _store_split(h2_ref.at[:, 0], h2)
    logits = jnp.concatenate([jnp.sum(h2 * wr_ref[e:e + 1, :], axis=1, keepdims=True)
                              for e in range(N_EXPERTS)], axis=1)
    lane = lax.broadcasted_iota(jnp.int32, logits.shape, 1).astype(F32)
    v1 = jnp.max(logits, axis=-1, keepdims=True)
    i1 = jnp.min(jnp.where(logits == v1, lane, float(N_EXPERTS)), axis=-1, keepdims=True)
    oh1 = lane == i1
    rest = jnp.where(oh1, -jnp.inf, logits)
    v2 = jnp.max(rest, axis=-1, keepdims=True)
    i2 = jnp.min(jnp.where(rest == v2, lane, float(N_EXPERTS)), axis=-1, keepdims=True)
    oh2 = lane == i2
    e2 = jnp.exp(v2 - v1)
    den = 1.0 + e2
    gate_ref[0] = jnp.where(oh1, 1.0 / den, 0.0) + jnp.where(oh2, e2 / den, 0.0)
    sel_ref[0] = jnp.where(oh1 | oh2, 1.0, 0.0)


def _router_call(att, cv, x, mod, mod_rows, mod_idx, gpm, gpf, wo_bf, wr, *, tm, name):
    b, t, d = x.shape
    specs, row = _mix_in_specs(b, t, tm, mod_rows, mod_idx, d)
    vec = pl.BlockSpec((1, d), lambda i: (0, 0))
    in_specs = specs + [_mod_spec(mod_rows, 2, mod_idx), _mod_spec(mod_rows, 4, mod_idx),
                        _mod_spec(mod_rows, 3, mod_idx), vec, vec,
                        pl.BlockSpec((d, d), lambda i: (0, 0)),
                        pl.BlockSpec((N_EXPERTS, d), lambda i: (0, 0))]
    return pl.pallas_call(
        _router_kernel,
        grid=(b * (t // tm),),
        in_specs=in_specs,
        out_specs=[pl.BlockSpec((1, tm, d), row),
                   pl.BlockSpec((2, 1, tm, d // 4), lambda i: (0,) + row(i)),
                   pl.BlockSpec((1, tm, N_EXPERTS), row), pl.BlockSpec((1, tm, N_EXPERTS), row)],
        out_shape=[jax.ShapeDtypeStruct((b, t, d), F32), jax.ShapeDtypeStruct((2, b, t, d // 4), jnp.int32),
                   jax.ShapeDtypeStruct((b, t, N_EXPERTS), F32), jax.ShapeDtypeStruct((b, t, N_EXPERTS), F32)],
        compiler_params=_params(1), name=name,
    )(att, cv, x, mod, mod, mod, gpm, gpf, wo_bf, wr)


MOE_TILE = 1024
RANK_BLOCK = 1536
GATHER_WINDOW = 128
GATHER_ROWS = 32 * GATHER_WINDOW
MOE_CHUNK = 3
GATHER_DEPTH = 3


def _pack_pairs(x):
    w = x.shape[1] // 2
    lo = pltpu.bitcast(x[:, :w].astype(BF16).astype(F32), jnp.int32)
    hi = pltpu.bitcast(x[:, w:].astype(BF16).astype(F32), jnp.int32)
    return hi | lax.shift_right_logical(lo, 16)


def _unpack_pairs(words):
    lo = pltpu.bitcast(lax.shift_left(words, 16), F32)
    hi = pltpu.bitcast(words & -65536, F32)
    return jnp.concatenate([lo, hi], axis=1)


def _store_split(ref, x):
    words = _pack_pairs(x)
    q = words.shape[1] // 2
    ref[0] = words[:, :q]
    ref[1] = words[:, q:]


def _load_split(ref):
    return _unpack_pairs(jnp.concatenate([ref[0], ref[1]], axis=1))


def _rank_kernel(sel_ref, rank_ref, cnt_ref, carry):
    @pl.when(pl.program_id(0) == 0)
    def _():
        carry[...] = jnp.zeros_like(carry)

    sel = sel_ref[...]
    rb = sel.shape[0]
    before = (lax.broadcasted_iota(jnp.int32, (rb, rb), 1)
              < lax.broadcasted_iota(jnp.int32, (rb, rb), 0)).astype(BF16)
    rank_ref[...] = jnp.dot(before, sel.astype(BF16), preferred_element_type=F32) + carry[...]
    carry[...] += jnp.sum(sel, axis=0, keepdims=True)
    cnt_ref[...] = carry[...]


def _rank_call(sel):
    n, n_e = sel.shape
    rb = RANK_BLOCK
    assert n % rb == 0
    return pl.pallas_call(
        _rank_kernel,
        grid=(n // rb,),
        in_specs=[pl.BlockSpec((rb, n_e), lambda i: (i, 0))],
        out_specs=[pl.BlockSpec((rb, n_e), lambda i: (i, 0)), pl.BlockSpec((1, n_e), lambda i: (0, 0))],
        out_shape=[jax.ShapeDtypeStruct((n, n_e), F32), jax.ShapeDtypeStruct((1, n_e), F32)],
        scratch_shapes=[pltpu.VMEM((1, n_e), F32)],
        compiler_params=_params(1), name="moe_rank",
    )(sel)


def _source_kernel(dlo_ref, dhi_ref, init_hbm, src_ref):
    pltpu.sync_copy(init_hbm, src_ref)

    def place(i, carry):
        src_ref[dlo_ref[i]] = i
        src_ref[dhi_ref[i]] = i
        return carry

    lax.fori_loop(0, dlo_ref.shape[0], place, 0, unroll=8)


def _source_call(d_lo, d_hi, init):
    smem = pl.BlockSpec(memory_space=pltpu.SMEM)
    return pl.pallas_call(
        _source_kernel,
        in_specs=[smem, smem, pl.BlockSpec(memory_space=pl.ANY)], out_specs=smem,
        out_shape=jax.ShapeDtypeStruct(init.shape, jnp.int32),
        name="moe_source_rows",
    )(d_lo.astype(jnp.int32), d_hi.astype(jnp.int32), init)


def _sc_gather(table, idxs):
    halves, _, width = table.shape
    n_lists, n = len(idxs), idxs[0].shape[0]
    info = plsc.get_sparse_core_info()
    win, depth = GATHER_WINDOW, GATHER_DEPTH
    assert n % GATHER_ROWS == 0 and GATHER_ROWS == info.num_cores * info.num_subcores * win
    wins = n // GATHER_ROWS
    jobs = [(l, j, h) for l in range(n_lists) for j in range(wins) for h in range(halves)]
    mesh = plsc.VectorSubcoreMesh(core_axis_name="core", subcore_axis_name="subcore")
    out_type = [jax.ShapeDtypeStruct((halves, n, width), table.dtype) for _ in idxs]
    scratch = [pltpu.VMEM((n_lists, wins, win), jnp.int32), pltpu.VMEM((depth, win, width), table.dtype),
               pltpu.SemaphoreType.DMA((depth,)), pltpu.SemaphoreType.DMA((depth,))]

    @functools.partial(pl.kernel, out_type=out_type, mesh=mesh, scratch_types=scratch)
    def gather(table_hbm, idx_hbm, *refs):
        out_refs = refs[:n_lists]
        idx_v, buf, sem_in, sem_out = refs[n_lists:]
        worker = lax.axis_index("subcore") * info.num_cores + lax.axis_index("core")
        first = worker * wins
        for l in range(n_lists):
            pltpu.sync_copy(idx_hbm.at[l, worker], idx_v.at[l])
        for g in range(0, len(jobs), depth):
            group = jobs[g:g + depth]
            reads = [pltpu.async_copy(table_hbm.at[h].at[idx_v.at[l, j]], buf.at[k], sem_in.at[k])
                     for k, (l, j, h) in enumerate(group)]
            writes = []
            for k, (l, j, h) in enumerate(group):
                reads[k].wait()
                rows = pl.ds(pl.multiple_of((first + j) * win, win), win)
                writes.append(pltpu.async_copy(buf.at[k], out_refs[l].at[h, rows], sem_out.at[k]))
            for write in writes:
                write.wait()

    return gather(table, jnp.stack(idxs).reshape(n_lists, n // (wins * win), wins, win))


def _expert_kernel(te_ref, nu_ref, xs_ref, wg_ref, wu_ref, wd_ref, *refs):
    ys_ref, xb, acc = refs[-3:]
    t, f = pl.program_id(0), pl.program_id(1)
    last = pl.num_programs(1) - 1
    used = t < nu_ref[0]

    @pl.when(used & (f == 0))
    def _():
        xb[...] = _load_split(xs_ref).astype(BF16)
        acc[...] = jnp.zeros_like(acc)

    @pl.when(used)
    def _():
        x = xb[...]
        g = jnp.dot(x, wg_ref[0].astype(BF16), preferred_element_type=F32)
        u = jnp.dot(x, wu_ref[0].astype(BF16), preferred_element_type=F32)
        act = (_silu(g) * u).astype(BF16)
        acc[...] += jnp.dot(act, wd_ref[0].astype(BF16), preferred_element_type=F32)

    @pl.when(used & (f == last))
    def _():
        _store_split(ys_ref, acc[...])

    @pl.when(jnp.logical_not(used) & (f == last))
    def _():
        ys_ref[...] = jnp.zeros_like(ys_ref)


def _expert_call(tile_expert, n_used, xs, wg, wu, wd, ys, tile0, rows_total, *, tf):
    _, rows, quarter = xs.shape
    n_e, d, ff = wg.shape
    tm = MOE_TILE
    nf = ff // tf
    chunk = lambda t, f, te, nu: jnp.where(t < nu[0], f, nf - 1)
    in_specs = [pl.BlockSpec((2, tm, quarter), lambda t, f, te, nu: (0, t, 0)),
                pl.BlockSpec((1, d, tf), lambda t, f, te, nu: (te[t], 0, chunk(t, f, te, nu))),
                pl.BlockSpec((1, d, tf), lambda t, f, te, nu: (te[t], 0, chunk(t, f, te, nu))),
                pl.BlockSpec((1, tf, d), lambda t, f, te, nu: (te[t], chunk(t, f, te, nu), 0))]
    args = [tile_expert, n_used, xs, wg, wu, wd]
    aliases = {}
    if ys is not None:
        in_specs.append(pl.BlockSpec(memory_space=pl.ANY))
        args.append(ys)
        aliases = {len(args) - 1: 0}
    grid_spec = pltpu.PrefetchScalarGridSpec(
        num_scalar_prefetch=2,
        grid=(rows // tm, nf),
        in_specs=in_specs,
        out_specs=pl.BlockSpec((2, tm, quarter), lambda t, f, te, nu: (0, tile0 + t, 0)),
        scratch_shapes=[pltpu.VMEM((tm, d), BF16), pltpu.VMEM((tm, d), F32)])
    return pl.pallas_call(
        _expert_kernel, grid_spec=grid_spec,
        out_shape=jax.ShapeDtypeStruct((2, rows_total, quarter), jnp.int32),
        input_output_aliases=aliases,
        compiler_params=_params(2), name="moe_experts",
    )(*args)


def _moe_route(h2p, sel, gates, wg, wu, wd, *, tf):
    n, n_e = sel.shape
    tm = MOE_TILE
    rank, cnt = _rank_call(sel)
    counts = cnt[0].astype(jnp.int32)
    padded = (counts + tm - 1) // tm * tm
    seg_end = jnp.cumsum(padded)
    seg_start = seg_end - padded
    rows_max = -(-(2 * n + n_e * tm) // GATHER_ROWS) * GATHER_ROWS
    assert rows_max % tm == 0
    lanes = jnp.arange(n_e, dtype=jnp.int32)[None, :]
    e_lo = jnp.min(jnp.where(sel > 0, lanes, n_e - 1), axis=1)
    e_hi = jnp.max(jnp.where(sel > 0, lanes, 0), axis=1)
    pick = lambda a, e: jnp.take_along_axis(a, e[:, None], axis=1)[:, 0]
    rank_i = rank.astype(jnp.int32)
    d_lo = seg_start[e_lo] + pick(rank_i, e_lo)
    d_hi = seg_start[e_hi] + pick(rank_i, e_hi)
    w = jnp.stack([pick(gates, e_lo), pick(gates, e_hi)], axis=1)
    src = _source_call(d_lo, d_hi, jnp.arange(rows_max, dtype=jnp.int32) % n)
    n_tiles = rows_max // tm
    n_used = (seg_end[-1] // tm).astype(jnp.int32)
    tile_ids = jnp.minimum(jnp.arange(n_tiles, dtype=jnp.int32), n_used - 1)
    tile_expert = jnp.minimum(jnp.sum(seg_end[None, :] <= (tile_ids * tm)[:, None], axis=1), n_e - 1).astype(jnp.int32)

    step = GATHER_ROWS // tm * MOE_CHUNK
    ys = None
    for t0 in range(0, n_tiles, step):
        t1 = min(t0 + step, n_tiles)
        xs, = _sc_gather(h2p, [src[t0 * tm:t1 * tm]])
        ys = _expert_call(tile_expert[t0:t1], jnp.clip(n_used - t0, 0, t1 - t0).reshape(1), xs, wg, wu, wd,
                          ys, t0, rows_max, tf=tf)
    n_pad = -(-n // GATHER_ROWS) * GATHER_ROWS
    pad_idx = lambda dd: jnp.concatenate([dd.astype(jnp.int32), jnp.arange(n_pad - n, dtype=jnp.int32)])
    ya, yb = _sc_gather(ys, [pad_idx(d_lo), pad_idx(d_hi)])
    return ya, yb, w


def _combine_kernel(ya_ref, yb_ref, w_ref, x1_ref, gt2_ref, gqf_ref, o_ref):
    w = w_ref[...]
    f = w[:, 0:1] * _load_split(ya_ref) + w[:, 1:2] * _load_split(yb_ref)
    o_ref[0] = x1_ref[0] + gt2_ref[0] * _rms(f, gqf_ref[...])


def _combine_call(ya, yb, w, x1, row0, mod, mod_rows, mod_idx, gqf, *, tm, name):
    b, t, d = x1.shape
    ntb = t // tm
    blk0 = row0 // tm
    assert row0 % tm == 0
    pool = lambda i: (blk0 + i, 0)
    pool3 = lambda i: (0, blk0 + i, 0)
    row = lambda i: (i // ntb, i % ntb, 0)
    return pl.pallas_call(
        _combine_kernel,
        grid=(b * ntb,),
        in_specs=[pl.BlockSpec((2, tm, d // 4), pool3), pl.BlockSpec((2, tm, d // 4), pool3),
                  pl.BlockSpec((tm, 2), pool), pl.BlockSpec((1, tm, d), row),
                  _mod_spec(mod_rows, 5, mod_idx), pl.BlockSpec((1, d), lambda i: (0, 0))],
        out_specs=pl.BlockSpec((1, tm, d), row),
        out_shape=jax.ShapeDtypeStruct((b, t, d), F32),
        compiler_params=_params(1), name=name,
    )(ya, yb, w, x1, mod, gqf)


def _rope_tables(pos):
    half = ROT_DIM // 2
    inv_freq = ROPE_THETA ** (-jnp.arange(half, dtype=F32) * 2.0 / ROT_DIM)
    ang = pos.astype(F32)[:, None] * inv_freq[None, :]
    cos, sin = jnp.cos(ang), jnp.sin(ang)
    l64 = np.arange(128) % HEAD_DIM
    idx = l64 % half
    first = (l64 < half)[None, :]
    second = ((l64 >= half) & (l64 < ROT_DIM))[None, :]
    cos_t = jnp.where(first | second, cos[:, idx], 1.0)
    s1_t = jnp.where(first, -sin[:, idx], 0.0)
    s2_t = jnp.where(second, sin[:, idx], 0.0)
    return cos_t, s1_t, s2_t


def kernel(x_prompt, x_sample, cache_k, cache_v, state_conv, c_prompt, c_sample, w_mod, b_mod, g_pre_mix, g_post_mix, g_pre_ffn, g_post_ffn, w_in, conv_w, conv_b, conv_ln_g, conv_ln_b, w_out, ffn_w_gate, ffn_w_up, ffn_w_down, moe_w_router, moe_w_gate, moe_w_up, moe_w_down):
    bp, seq, d = x_prompt.shape
    nb, n_new, _ = x_sample.shape
    depth = w_mod.shape[0]
    w_buf = cache_k.shape[2]
    past_len = PAST_LEN
    assert w_buf == min(DILATION_PATTERNS[-1][0], past_len)
    keep = min(DILATION_PATTERNS[-1][0], seq)
    ns = nb * n_new

    c_all = jnp.concatenate([c_sample, c_prompt, jnp.zeros((MOD_ROWS - nb - bp, d), F32)], axis=0)
    mod_all = _mod_call(c_all, w_mod, b_mod)

    tm_p = 512
    tabs_p = tuple(tb.reshape(seq // tm_p, tm_p, 128) for tb in _rope_tables(jnp.arange(seq, dtype=jnp.int32)))
    tabs_s = tuple(tb.reshape(n_new, 1, 128)
                   for tb in _rope_tables(past_len + jnp.arange(n_new, dtype=jnp.int32)))

    cache_kt = cache_k.transpose(0, 1, 3, 4, 2)
    cache_vt = cache_v.transpose(0, 1, 3, 4, 2)

    yp = x_prompt
    ys = x_sample.transpose(1, 0, 2).reshape(1, ns, d)
    outs = [[] for _ in range(6)]
    for l in range(depth):
        mod_p = mod_all[l].reshape(MOD_ROWS, 1, 6 * d)
        mod_s = jnp.tile(mod_all[l, :nb], (n_new, 1))[None]
        idx_p2 = lambda bi, i: (nb + bi, 0)
        idx_s2 = lambda bi, i: (0, i)
        w_in_bf = w_in[l].astype(BF16)
        w_out_bf = w_out[l].astype(BF16)
        conv_p = (jnp.pad(conv_w[l], ((0, 1), (0, 0))), conv_b[l][None], conv_ln_g[l][None], conv_ln_b[l][None])
        gpre, gpm, gpf, gqf = g_pre_mix[l][None], g_post_mix[l][None], g_pre_ffn[l][None], g_post_ffn[l][None]

        q, k, v, kt, vt, cv, ul = _inproj_call(yp, mod_p, 1, idx_p2, gpre, w_in_bf, tabs_p, tm_p, conv_p,
                                               tm=tm_p, keep=keep, name=f"inproj_prompt_{l}")
        att = _attn_call(q, k, v)
        outs[0].append(kt.reshape(bp, keep, N_HEADS, HEAD_DIM))
        outs[1].append(vt.reshape(bp, keep, N_HEADS, HEAD_DIM))
        outs[2].append(ul[:, CONV_HALO - (CONV_WIDTH - 1):])

        qs, _, _, kts, vts, us = _inproj_call(ys, mod_s, nb, idx_s2, gpre, w_in_bf, tabs_s, 1, None,
                                              tm=nb, keep=ns, name=f"inproj_sample_{l}")
        to_batch_major = lambda z: z.reshape(n_new, nb, N_HEADS, HEAD_DIM).transpose(1, 0, 2, 3)
        q_bm = (qs.reshape(HEAD_PAIRS, n_new, nb, 2, HEAD_DIM).transpose(2, 1, 0, 3, 4)
                .reshape(nb, n_new, N_HEADS, HEAD_DIM))
        k_bm, v_bm = to_batch_major(kts), to_batch_major(vts)
        head_major = lambda z: jnp.pad(z.transpose(0, 2, 1, 3), ((0, 0), (0, 0), (0, SAMPLE_ROWS - n_new), (0, 0)))
        att_s = _sample_attn_call(head_major(q_bm), head_major(k_bm), head_major(v_bm),
                                  cache_kt, cache_vt, l, n_new)
        att_s = (att_s[:, :, :n_new].reshape(nb, HEAD_PAIRS, 2, n_new, HEAD_DIM).transpose(1, 3, 0, 2, 4)
                 .reshape(1, HEAD_PAIRS, ns, 128).astype(BF16))
        ufull = jnp.concatenate([state_conv[l].transpose(1, 0, 2), us.reshape(n_new, nb, 512)], axis=0)
        cv_s = _sample_conv_call(ufull, conv_p).reshape(1, ns, 512)
        outs[3].append(k_bm)
        outs[4].append(v_bm)
        outs[5].append(ufull[n_new:].transpose(1, 0, 2))

        idx_p1 = lambda i, *_: (nb + i // (seq // tm_f), 0)
        idx_s1 = lambda i, *_: (0, i)
        if l % 2 == 0:
            tm_f = 1024
            wg, wu, wd = ffn_w_gate[l // 2], ffn_w_up[l // 2], ffn_w_down[l // 2]
            yp = _ffn_call(att, cv, yp, mod_p, 1, idx_p1, gpm, gpf, gqf, w_out_bf, wg, wu, wd,
                           tm=tm_f, tf=512, name=f"ffn_prompt_{l}")
            ys = _ffn_call(att_s, cv_s, ys, mod_s, ns, idx_s1, gpm, gpf, gqf, w_out_bf, wg, wu, wd,
                           tm=ns, tf=512, name=f"ffn_sample_{l}")
        else:
            tm_f = 1024
            wr = moe_w_router[l // 2].T
            wg, wu, wd = moe_w_gate[l // 2], moe_w_up[l // 2], moe_w_down[l // 2]
            x1, h2, gates, sel = _router_call(att, cv, yp, mod_p, 1, idx_p1, gpm, gpf, w_out_bf, wr,
                                              tm=tm_f, name=f"router_prompt_{l}")
            x1s, h2s, gates_s, sel_s = _router_call(att_s, cv_s, ys, mod_s, ns, idx_s1, gpm, gpf, w_out_bf, wr,
                                                    tm=ns, name=f"router_sample_{l}")
            pool = lambda a, b_: jnp.concatenate([a.reshape(bp * seq, -1), b_.reshape(ns, -1)], axis=0)
            h2_pool = jnp.concatenate([h2.reshape(2, bp * seq, d // 4), h2s.reshape(2, ns, d // 4)], axis=1)
            ya, yb, w12 = _moe_route(h2_pool, pool(sel, sel_s), pool(gates, gates_s), wg, wu, wd, tf=512)
            yp = _combine_call(ya, yb, w12, x1, 0, mod_p, 1, idx_p1, gqf, tm=tm_f, name=f"moe_combine_prompt_{l}")
            ys = _combine_call(ya, yb, w12, x1s, bp * seq, mod_s, ns, idx_s1, gqf, tm=ns,
                               name=f"moe_combine_sample_{l}")

    y_sample = ys.reshape(n_new, nb, d).transpose(1, 0, 2)
    return (yp, y_sample) + tuple(jnp.stack(o) for o in outs)
```
